```python
import math
import jax, jax.numpy as jnp
from jax import lax
import numpy as np

D_MODEL = 2048
BATCH = 8
SEQ = 4096
DEPTH = 4
DEC_BATCH = 2
DEC_SEQ = 4096
PAST_LEN = 128

GRID_W = 64
WIN_H = 8
WIN_W = 16
DH = 128
H_A = D_MODEL // 256
D_A = H_A * DH
D_B = D_MODEL // 2
EMB = 33
FO = 64
FAST_DECAY = 0.3
SLOW_DECAY = 1.5
DECAY_TARGET = 1e-2
D_C = D_MODEL
G_C = 8
DG_C = D_C // G_C
CHUNK = 128
D_FF = 4 * D_MODEL
N_EVEN = (DEPTH + 1) // 2
N_ODD = DEPTH // 2
D_IN_AB = 3 * D_A + 3 * D_B
D_MIX_AB = D_A + D_B
EPS = 1e-6

kernel_name = "hybrid_natten_hyena_gmlp_encoder"


def rmsnorm(x, g):
    xf = x.astype(jnp.float32)
    y = xf * lax.rsqrt(jnp.mean(xf * xf, axis=-1, keepdims=True) + EPS) * g.astype(jnp.float32)
    return y.astype(x.dtype)


def neighbourhood_attention(q, k, v, q_gain, k_gain, rpb):
    B, L, H, Dh = q.shape
    rows = L // GRID_W
    kh = min(WIN_H, rows)
    q = (rmsnorm(q, q_gain) * (Dh ** -0.5)).reshape(B, rows, GRID_W, H, Dh)
    k = rmsnorm(k, k_gain).reshape(B, rows, GRID_W, H, Dh)
    v = v.reshape(B, rows, GRID_W, H, Dh)
    col = jnp.arange(GRID_W)
    cs = jnp.clip(col - WIN_W // 2, 0, GRID_W - WIN_W)
    qc, kc = col[:, None], col[None, :]
    col_ok = (kc >= cs[:, None]) & (kc < cs[:, None] + WIN_W)
    dc = jnp.clip(kc - qc, -(WIN_W - 1), WIN_W - 1) + WIN_W - 1
    bias_c = rpb.astype(jnp.float32)[:, :, dc]
    bias_c = jnp.where(col_ok, bias_c, -jnp.inf)

    def row_block(r):
        rs = jnp.clip(r - kh // 2, 0, rows - kh)
        q_r = lax.dynamic_index_in_dim(q, r, axis=1, keepdims=False)
        k_r = lax.dynamic_slice_in_dim(k, rs, kh, axis=1)
        v_r = lax.dynamic_slice_in_dim(v, rs, kh, axis=1)
        dr = rs + jnp.arange(kh) - r + WIN_H - 1
        bias = jnp.take(bias_c, dr, axis=1).transpose(0, 2, 1, 3)
        s = jnp.einsum('bqhd,bikhd->bhqik', q_r, k_r).astype(jnp.float32) + bias
        p = jax.nn.softmax(s.reshape(B, H, GRID_W, kh * GRID_W), axis=-1)
        p = p.reshape(B, H, GRID_W, kh, GRID_W).astype(v.dtype)
        return jnp.einsum('bhqik,bikhd->bqhd', p, v_r)

    o = lax.map(row_block, jnp.arange(rows))
    return o.transpose(1, 0, 2, 3, 4).reshape(B, L, H * Dh)


def short_conv3(z, w, b):
    zp = jnp.pad(z, ((0, 0), (1, 1), (0, 0)))
    return zp[:, :-2] * w[0] + zp[:, 1:-1] * w[1] + zp[:, 2:] * w[2] + b


def hyena_kernel(L, f_w1, f_b1, f_w2, f_b2, f_w3, f_b3, f_wout, f_freq):
    f32 = jnp.float32
    t = jnp.linspace(0.0, 1.0, L, dtype=f32)[:, None]
    bands = (EMB - 1) // 2
    w = 2.0 * math.pi * jnp.arange(L, dtype=f32)[:, None] / L
    f = jnp.linspace(1e-4, bands - 1, bands, dtype=f32)[None, :]
    z = jnp.concatenate([t, jnp.cos(f * w), -jnp.sin(f * w)], axis=-1)
    fr = f_freq.astype(f32)
    h = jnp.sin(fr * (z @ f_w1.astype(f32) + f_b1.astype(f32)))
    h = jnp.sin(fr * (h @ f_w2.astype(f32) + f_b2.astype(f32)))
    h = jnp.sin(fr * (h @ f_w3.astype(f32) + f_b3.astype(f32)))
    h = h @ f_wout.astype(f32)
    deltas = jnp.abs(jnp.linspace(math.log(DECAY_TARGET) / FAST_DECAY,
                                  math.log(DECAY_TARGET) / SLOW_DECAY, D_B, dtype=f32))
    decay = jnp.exp(-t * deltas)
    h_fwd = h[:, :D_B] * decay
    h_bwd = h[:, D_B:] * decay
    kern = jnp.concatenate([h_fwd, jnp.zeros((1, D_B), f32), h_bwd[:0:-1]], axis=0)
    return kern / jnp.sum(jnp.abs(kern), axis=0, keepdims=True)


def hyena(z, kern, h_bias):
    B, L, _ = z.shape
    x0, x1, v = jnp.split(z, 3, axis=-1)
    u = (v * x1).astype(jnp.float32)
    y = jnp.fft.irfft(jnp.fft.rfft(u, n=2 * L, axis=1) * jnp.fft.rfft(kern, axis=0)[None],
                      n=2 * L, axis=1)[:, :L]
    y = y + u * h_bias.astype(jnp.float32)
    return (y * x0.astype(jnp.float32)).astype(z.dtype)


def mixer_ab(h, w_in, sc_w, sc_b, q_gain, k_gain, rpb, f_w1, f_b1, f_w2, f_b2, f_w3, f_b3,
             f_wout, f_freq, h_bias, w_out):
    B, L, _ = h.shape
    p = h @ w_in
    q = p[..., :D_A].reshape(B, L, H_A, DH)
    k = p[..., D_A:2 * D_A].reshape(B, L, H_A, DH)
    v = p[..., 2 * D_A:3 * D_A].reshape(B, L, H_A, DH)
    a = neighbourhood_attention(q, k, v, q_gain, k_gain, rpb)
    z = short_conv3(p[..., 3 * D_A:], sc_w, sc_b)
    kern = hyena_kernel(L, f_w1, f_b1, f_w2, f_b2, f_w3, f_b3, f_wout, f_freq)
    b = hyena(z, kern, h_bias)
    return jnp.concatenate([a, b], axis=-1) @ w_out


def mixer_c(h, w_in, v_gain, w_s, b_s, w_out):
    B, L, _ = h.shape
    zz = jax.nn.gelu(h @ w_in)
    u, v = jnp.split(zz, 2, axis=-1)
    v = rmsnorm(v, v_gain).reshape(B, L // CHUNK, CHUNK, G_C, DG_C)
    s = jnp.einsum('gpq,bnqgc->bnpgc', w_s, v) + b_s.T[None, None, :, :, None]
    return (u * s.reshape(B, L, D_C)) @ w_out


def mlp_sqrelu(h, w1, w2):
    return jnp.square(jax.nn.relu(h @ w1)) @ w2


def trunk(x, norm_mix, norm_mlp, w_in_ab, sc_w, sc_b, q_gain, k_gain, rpb, f_w1, f_b1, f_w2, f_b2,
          f_w3, f_b3, f_wout, f_freq, h_bias, w_out_ab, w_in_c, v_gain, w_s, b_s, w_out_c,
          w_mlp1, w_mlp2):
    for i in range(DEPTH):
        j = i // 2
        h = rmsnorm(x, norm_mix[i])
        if i % 2 == 0:
            x = x + mixer_ab(h, w_in_ab[j], sc_w[j], sc_b[j], q_gain[j], k_gain[j], rpb[j],
                             f_w1[j], f_b1[j], f_w2[j], f_b2[j], f_w3[j], f_b3[j], f_wout[j],
                             f_freq[j], h_bias[j], w_out_ab[j])
        else:
            x = x + mixer_c(h, w_in_c[j], v_gain[j], w_s[j], b_s[j], w_out_c[j])
        x = x + mlp_sqrelu(rmsnorm(x, norm_mlp[i]), w_mlp1[i], w_mlp2[i])
    return x


def setup_inputs(seed: int = 0) -> dict:
    key = jax.random.key(seed)
    ks = iter(jax.random.split(key, 40))

    def nrm(shape, scale):
        return jax.random.normal(next(ks), shape, jnp.float32) * scale

    def gain(shape):
        return 1.0 + nrm(shape, 0.05)

    return {
        "x_prompt": nrm((BATCH, SEQ, D_MODEL), 1.0),
        "x_sample": nrm((DEC_BATCH, DEC_SEQ, D_MODEL), 1.0),
        "norm_mix": gain((DEPTH, D_MODEL)),
        "norm_mlp": gain((DEPTH, D_MODEL)),
        "w_in_ab": nrm((N_EVEN, D_MODEL, D_IN_AB), D_MODEL ** -0.5),
        "sc_w": nrm((N_EVEN, 3, 3 * D_B), 3 ** -0.5),
        "sc_b": nrm((N_EVEN, 3 * D_B), 0.02),
        "q_gain": gain((N_EVEN, DH)),
        "k_gain": gain((N_EVEN, DH)),
        "rpb": nrm((N_EVEN, H_A, 2 * WIN_H - 1, 2 * WIN_W - 1), 0.1),
        "f_w1": nrm((N_EVEN, EMB, FO), EMB ** -0.5),
        "f_b1": nrm((N_EVEN, FO), 0.1),
        "f_w2": nrm((N_EVEN, FO, FO), FO ** -0.5),
        "f_b2": nrm((N_EVEN, FO), 0.1),
        "f_w3": nrm((N_EVEN, FO, FO), FO ** -0.5),
        "f_b3": nrm((N_EVEN, FO), 0.1),
        "f_wout": nrm((N_EVEN, FO, 2 * D_B), FO ** -0.5),
        "f_freq": 1.0 + nrm((N_EVEN, FO), 0.1),
        "h_bias": nrm((N_EVEN, D_B), 0.5),
        "w_out_ab": nrm((N_EVEN, D_MIX_AB, D_MODEL), D_MIX_AB ** -0.5),
        "w_in_c": nrm((N_ODD, D_MODEL, 2 * D_C), D_MODEL ** -0.5),
        "v_gain": gain((N_ODD, D_C)),
        "w_s": nrm((N_ODD, G_C, CHUNK, CHUNK), CHUNK ** -0.5),
        "b_s": 1.0 + nrm((N_ODD, G_C, CHUNK), 0.1),
        "w_out_c": nrm((N_ODD, D_C, D_MODEL), D_C ** -0.5),
        "w_mlp1": nrm((DEPTH, D_MODEL, D_FF), D_MODEL ** -0.5),
        "w_mlp2": nrm((DEPTH, D_FF, D_MODEL), D_FF ** -0.5),
    }


def reference(x_prompt, x_sample, norm_mix, norm_mlp, w_in_ab, sc_w, sc_b, q_gain, k_gain, rpb,
              f_w1, f_b1, f_w2, f_b2, f_w3, f_b3, f_wout, f_freq, h_bias, w_out_ab, w_in_c,
              v_gain, w_s, b_s, w_out_c, w_mlp1, w_mlp2):
    y_prompt = trunk(x_prompt, norm_mix, norm_mlp, w_in_ab, sc_w, sc_b, q_gain, k_gain, rpb,
                     f_w1, f_b1, f_w2, f_b2, f_w3, f_b3, f_wout, f_freq, h_bias, w_out_ab,
                     w_in_c, v_gain, w_s, b_s, w_out_c, w_mlp1, w_mlp2)
    y_sample = trunk(x_sample, norm_mix, norm_mlp, w_in_ab, sc_w, sc_b, q_gain, k_gain, rpb,
                     f_w1, f_b1, f_w2, f_b2, f_w3, f_b3, f_wout, f_freq, h_bias, w_out_ab,
                     w_in_c, v_gain, w_s, b_s, w_out_c, w_mlp1, w_mlp2)
    return (y_prompt, y_sample)
```

```python
import functools
import math

import numpy as np
import jax
import jax.numpy as jnp
from jax import lax
from jax.experimental import pallas as pl
from jax.experimental.pallas import tpu as pltpu

F32 = jnp.float32
BF16 = jnp.bfloat16

D_MODEL = 2048
DEPTH = 4
SEQ = 4096
GRID_W = 64
WIN_H = 8
WIN_W = 16
DH = 128
H_A = 8
D_A = H_A * DH
D_B = 1024
EMB = 33
FO = 64
FAST_DECAY = 0.3
SLOW_DECAY = 1.5
DECAY_TARGET = 1e-2
D_C = D_MODEL
G_C = 8
DG_C = D_C // G_C
CHUNK = 128
D_FF = 4 * D_MODEL
EPS = 1e-6

LANES = 128
VMEM_LIMIT_BYTES = 52 * 1024 * 1024

N_FFT = 2 * SEQ
N_FAST = 128
N_SLOW = N_FFT // N_FAST
N_IN = SEQ // N_FAST
N_K1 = N_SLOW // 2 + 1


def _cparams(sem):
    return pltpu.CompilerParams(dimension_semantics=sem, vmem_limit_bytes=VMEM_LIMIT_BYTES)


def _rms_scale(x, g):
    return x * lax.rsqrt(jnp.mean(x * x, axis=-1, keepdims=True) + EPS) * g


def _gelu_tanh(x):
    return 0.5 * x * (1.0 + jnp.tanh(math.sqrt(2.0 / math.pi) * (x + 0.044715 * (x * x * x))))


def _split_bf16(x):
    hi = x.astype(BF16)
    lo = (x - hi.astype(F32)).astype(BF16)
    return hi, lo


def _dot3(a, b):
    ah, al = _split_bf16(a)
    bh, bl = _split_bf16(b)
    d = functools.partial(jnp.dot, preferred_element_type=F32)
    return d(ah, bh) + (d(ah, bl) + d(al, bh))


def _norm_matmul_kernel(x_ref, g_ref, w_ref, o_ref, h_ref, *, act):
    @pl.when(pl.program_id(1) == 0)
    def _():
        h_ref[...] = _rms_scale(x_ref[...], g_ref[...]).astype(BF16)

    acc = jnp.dot(h_ref[...], w_ref[...], preferred_element_type=F32)
    if act == "gelu":
        acc = _gelu_tanh(acc)
    o_ref[...] = acc.astype(o_ref.dtype)


def norm_matmul(x, g, w, *, act=None, tm=1024, tn=1024, out_dtype=F32):
    t, d = x.shape
    n = w.shape[1]
    return pl.pallas_call(
        functools.partial(_norm_matmul_kernel, act=act),
        out_shape=jax.ShapeDtypeStruct((t, n), out_dtype),
        grid=(t // tm, n // tn),
        in_specs=[
            pl.BlockSpec((tm, d), lambda i, j: (i, 0)),
            pl.BlockSpec((1, d), lambda i, j: (0, 0)),
            pl.BlockSpec((d, tn), lambda i, j: (0, j)),
        ],
        out_specs=pl.BlockSpec((tm, tn), lambda i, j: (i, j)),
        scratch_shapes=[pltpu.VMEM((tm, d), BF16)],
        compiler_params=_cparams(("parallel", "arbitrary")),
        name="norm_matmul",
    )(x, g.reshape(1, d), w)


def _mlp_kernel(x_ref, g_ref, w1_ref, w2_ref, o_ref, h_ref):
    @pl.when(pl.program_id(1) == 0)
    def _():
        x = x_ref[...]
        h_ref[...] = _rms_scale(x, g_ref[...]).astype(BF16)
        o_ref[...] = x

    a = jnp.dot(h_ref[...], w1_ref[...], preferred_element_type=F32)
    a = jnp.square(jnp.maximum(a, 0.0)).astype(BF16)
    o_ref[...] += jnp.dot(a, w2_ref[...], preferred_element_type=F32)


def mlp_residual(x, g, w1, w2, *, tm=512, tf=1024):
    t, d = x.shape
    f = w1.shape[1]
    return pl.pallas_call(
        _mlp_kernel,
        out_shape=jax.ShapeDtypeStruct((t, d), F32),
        grid=(t // tm, f // tf),
        in_specs=[
            pl.BlockSpec((tm, d), lambda i, k: (i, 0)),
            pl.BlockSpec((1, d), lambda i, k: (0, 0)),
            pl.BlockSpec((d, tf), lambda i, k: (0, k)),
            pl.BlockSpec((tf, d), lambda i, k: (k, 0)),
        ],
        out_specs=pl.BlockSpec((tm, d), lambda i, k: (i, 0)),
        scratch_shapes=[pltpu.VMEM((tm, d), BF16)],
        compiler_params=_cparams(("parallel", "arbitrary")),
        name="mlp_residual",
    )(x, g.reshape(1, d), w1, w2)


def _proj_residual_kernel(x_ref, a_ref, b_ref, wa_ref, wb_ref, o_ref):
    acc = jnp.dot(a_ref[...], wa_ref[...], preferred_element_type=F32)
    acc += jnp.dot(b_ref[...], wb_ref[...], preferred_element_type=F32)
    o_ref[...] = x_ref[...] + acc


def proj_residual(x, a, b, wa, wb, *, a_col=0, b_col=0, tm=1024, tn=1024):
    t, n = x.shape
    k = wa.shape[0]
    return pl.pallas_call(
        _proj_residual_kernel,
        out_shape=jax.ShapeDtypeStruct((t, n), F32),
        grid=(t // tm, n // tn),
        in_specs=[
            pl.BlockSpec((tm, tn), lambda i, j: (i, j)),
            pl.BlockSpec((tm, k), lambda i, j: (i, a_col)),
            pl.BlockSpec((tm, k), lambda i, j: (i, b_col)),
            pl.BlockSpec((k, tn), lambda i, j: (0, j)),
            pl.BlockSpec((k, tn), lambda i, j: (0, j)),
        ],
        out_specs=pl.BlockSpec((tm, tn), lambda i, j: (i, j)),
        compiler_params=_cparams(("parallel", "arbitrary")),
        name="proj_residual",
    )(x, a, b, wa, wb)


def _gate_kernel(zz_ref, vg_ref, ws_ref, bs_ref, o_ref, vn_ref, *, tm):
    vn_ref[...] = _rms_scale(zz_ref[:, D_C:], vg_ref[...]).astype(BF16)
    for c in range(tm // CHUNK):
        rows = slice(c * CHUNK, (c + 1) * CHUNK)
        for g in range(G_C):
            cols = slice(g * DG_C, (g + 1) * DG_C)
            s = jnp.dot(ws_ref[g], vn_ref[rows, cols], preferred_element_type=F32)
            bias = bs_ref[g]
            s = s + jnp.concatenate([bias] * (DG_C // LANES), axis=1)
            o_ref[rows, cols] = (zz_ref[rows, cols] * s).astype(BF16)


def gmlp_gate(zz, v_gain, w_s, b_s_lanes, *, tm=512):
    t = zz.shape[0]
    return pl.pallas_call(
        functools.partial(_gate_kernel, tm=tm),
        out_shape=jax.ShapeDtypeStruct((t, D_C), BF16),
        grid=(t // tm,),
        in_specs=[
            pl.BlockSpec((tm, 2 * D_C), lambda i: (i, 0)),
            pl.BlockSpec((1, D_C), lambda i: (0, 0)),
            pl.BlockSpec((G_C, CHUNK, CHUNK), lambda i: (0, 0, 0)),
            pl.BlockSpec((G_C, CHUNK, LANES), lambda i: (0, 0, 0)),
        ],
        out_specs=pl.BlockSpec((tm, D_C), lambda i: (i, 0)),
        scratch_shapes=[pltpu.VMEM((tm, D_C), BF16)],
        compiler_params=_cparams(("parallel",)),
        name="gmlp_gate",
    )(zz, v_gain.reshape(1, D_C), w_s, b_s_lanes)


N_ROWS = SEQ // GRID_W
N_WIN = WIN_H * GRID_W
NORM_ROWS = 512


def _natten_kernel(q_ref, k_ref, v_ref, qg_ref, kg_ref, bias_ref, o_ref, qn_ref, kn_ref, vb_ref):
    def prep(i, carry):
        rows = pl.ds(pl.multiple_of(i * NORM_ROWS, NORM_ROWS), NORM_ROWS)
        qn_ref[rows, :] = (_rms_scale(q_ref[0, rows, :], qg_ref[...]) * (DH ** -0.5)).astype(BF16)
        kn_ref[rows, :] = _rms_scale(k_ref[0, rows, :], kg_ref[...]).astype(BF16)
        vb_ref[rows, :] = v_ref[0, rows, :].astype(BF16)
        return carry

    lax.fori_loop(0, SEQ // NORM_ROWS, prep, 0)

    def row(r, carry):
        rs = jnp.clip(r - WIN_H // 2, 0, N_ROWS - WIN_H)
        qrows = pl.ds(pl.multiple_of(r * GRID_W, GRID_W), GRID_W)
        krows = pl.ds(pl.multiple_of(rs * GRID_W, GRID_W), N_WIN)
        s = lax.dot_general(qn_ref[qrows, :], kn_ref[krows, :], (((1,), (1,)), ((), ())),
                            preferred_element_type=F32)
        s = s + bias_ref[0, r - rs]
        p = jnp.exp(s - jnp.max(s, axis=-1, keepdims=True))
        l = jnp.sum(p, axis=-1, keepdims=True)
        o = jnp.dot(p.astype(BF16), vb_ref[krows, :], preferred_element_type=F32)
        o_ref[0, qrows, :] = (o / l).astype(BF16)
        return carry

    lax.fori_loop(0, N_ROWS, row, 0)


def natten(p3, q_gain, k_gain, bias_tab):
    b = p3.shape[0]
    blk = (1, SEQ, DH)
    return pl.pallas_call(
        _natten_kernel,
        out_shape=jax.ShapeDtypeStruct((b, SEQ, D_A), BF16),
        grid=(b, H_A),
        in_specs=[
            pl.BlockSpec(blk, lambda i, h: (i, 0, h)),
            pl.BlockSpec(blk, lambda i, h: (i, 0, H_A + h)),
            pl.BlockSpec(blk, lambda i, h: (i, 0, 2 * H_A + h)),
            pl.BlockSpec((1, DH), lambda i, h: (0, 0)),
            pl.BlockSpec((1, DH), lambda i, h: (0, 0)),
            pl.BlockSpec((1, WIN_H, GRID_W, N_WIN), lambda i, h: (h, 0, 0, 0)),
        ],
        out_specs=pl.BlockSpec(blk, lambda i, h: (i, 0, h)),
        scratch_shapes=[pltpu.VMEM((SEQ, DH), BF16)] * 3,
        compiler_params=_cparams(("parallel", "arbitrary")),
        name="natten",
    )(p3, p3, p3, q_gain.reshape(1, DH), k_gain.reshape(1, DH), bias_tab)


def natten_bias_table(rpb):
    col = np.arange(GRID_W)
    cs = np.clip(col - WIN_W // 2, 0, GRID_W - WIN_W)
    qc, kc = col[:, None], col[None, :]
    col_ok = (kc >= cs[:, None]) & (kc < cs[:, None] + WIN_W)
    dc = np.clip(kc - qc, -(WIN_W - 1), WIN_W - 1) + WIN_W - 1
    bias_c = rpb.astype(F32)[:, :, dc]
    bias_c = jnp.where(col_ok, bias_c, -jnp.inf)
    off = np.arange(WIN_H)[:, None]
    dr = np.arange(WIN_H)[None, :] - off + WIN_H - 1
    tab = bias_c[:, dr]
    return tab.transpose(0, 1, 3, 2, 4).reshape(H_A, WIN_H, GRID_W, N_WIN)


FILT_ROWS = 512


def _filter_kernel(z_ref, t_ref, w1_ref, b1_ref, w2_ref, b2_ref, w3_ref, b3_ref, fr_ref,
                   wf_ref, wb_ref, dl_ref, hf_ref, hb_ref, h3_ref):
    @pl.when(pl.program_id(0) == 0)
    def _():
        def mlp(i, carry):
            rows = pl.ds(pl.multiple_of(i * FILT_ROWS, FILT_ROWS), FILT_ROWS)
            fr = fr_ref[...]
            h = jnp.sin(fr * (_dot3(z_ref[rows, :], w1_ref[...]) + b1_ref[...]))
            h = jnp.sin(fr * (_dot3(h, w2_ref[...]) + b2_ref[...]))
            h3_ref[rows, :] = jnp.sin(fr * (_dot3(h, w3_ref[...]) + b3_ref[...]))
            return carry

        lax.fori_loop(0, SEQ // FILT_ROWS, mlp, 0)

    def taps(i, carry):
        sf, sb = carry
        rows = pl.ds(pl.multiple_of(i * FILT_ROWS, FILT_ROWS), FILT_ROWS)
        h3 = h3_ref[rows, :]
        decay = jnp.exp(-t_ref[rows, :] * dl_ref[...])
        hf = _dot3(h3, wf_ref[...]) * decay
        hb = _dot3(h3, wb_ref[...]) * decay
        t_idx = lax.broadcasted_iota(jnp.int32, hb.shape, 0) + i * FILT_ROWS
        hb = jnp.where(t_idx == 0, 0.0, hb)
        hf_ref[rows, :] = hf
        hb_ref[rows, :] = hb
        return (sf + jnp.sum(jnp.abs(hf), axis=0, keepdims=True),
                sb + jnp.sum(jnp.abs(hb), axis=0, keepdims=True))

    zero = jnp.zeros((1, LANES), F32)
    sf, sb = lax.fori_loop(0, SEQ // FILT_ROWS, taps, (zero, zero))
    inv = 1.0 / (sf + sb)

    def scale(i, carry):
        rows = pl.ds(pl.multiple_of(i * FILT_ROWS, FILT_ROWS), FILT_ROWS)
        hf_ref[rows, :] = hf_ref[rows, :] * inv
        hb_ref[rows, :] = hb_ref[rows, :] * inv
        return carry

    lax.fori_loop(0, SEQ // FILT_ROWS, scale, 0)


def _filter_constants():
    t = np.linspace(0.0, 1.0, SEQ, dtype=np.float32).astype(np.float64)[:, None]
    bands = (EMB - 1) // 2
    w = 2.0 * math.pi * np.arange(SEQ, dtype=np.float64)[:, None] / SEQ
    f = np.linspace(1e-4, bands - 1, bands, dtype=np.float32).astype(np.float64)[None, :]
    z = np.concatenate([t, np.cos(f * w), -np.sin(f * w)], axis=-1)
    z_pad = np.zeros((SEQ, LANES), np.float32)
    z_pad[:, :EMB] = z
    t_lanes = np.broadcast_to(t.astype(np.float32), (SEQ, LANES)).copy()
    deltas = np.abs(np.linspace(math.log(DECAY_TARGET) / FAST_DECAY,
                                math.log(DECAY_TARGET) / SLOW_DECAY, D_B, dtype=np.float32))
    return z_pad, t_lanes, deltas.reshape(1, D_B)


def hyena_filter(f_w1, f_b1, f_w2, f_b2, f_w3, f_b3, f_wout, f_freq):
    z_pad, t_lanes, deltas = _filter_constants()
    w1_pad = jnp.zeros((LANES, FO), F32).at[:EMB].set(f_w1)
    nblk = D_B // LANES
    const = lambda shape: pl.BlockSpec(shape, lambda c: (0,) * len(shape))
    return pl.pallas_call(
        _filter_kernel,
        out_shape=[jax.ShapeDtypeStruct((SEQ, D_B), F32)] * 2,
        grid=(nblk,),
        in_specs=[
            const((SEQ, LANES)), const((SEQ, LANES)),
            const((LANES, FO)), const((1, FO)),
            const((FO, FO)), const((1, FO)),
            const((FO, FO)), const((1, FO)),
            const((1, FO)),
            pl.BlockSpec((FO, LANES), lambda c: (0, c)),
            pl.BlockSpec((FO, LANES), lambda c: (0, nblk + c)),
            pl.BlockSpec((1, LANES), lambda c: (0, c)),
        ],
        out_specs=[pl.BlockSpec((SEQ, LANES), lambda c: (0, c))] * 2,
        scratch_shapes=[pltpu.VMEM((SEQ, FO), F32)],
        compiler_params=_cparams(("arbitrary",)),
        name="hyena_filter",
    )(jnp.asarray(z_pad), jnp.asarray(t_lanes), w1_pad, f_b1.reshape(1, FO), f_w2,
      f_b2.reshape(1, FO), f_w3, f_b3.reshape(1, FO), f_freq.reshape(1, FO), f_wout, f_wout,
      jnp.asarray(deltas))


@functools.lru_cache(maxsize=None)
def _dft_tables():
    k1 = np.arange(N_K1)
    n1 = np.arange(N_IN)
    ang = 2.0 * math.pi * np.outer(k1, n1) / N_SLOW
    fwd_c = np.cos(ang)
    fwd_s = -np.sin(ang)
    wgt = np.full(N_K1, 2.0)
    wgt[0] = 1.0
    wgt[-1] = 1.0
    inv_c = (np.cos(ang) * wgt[:, None]).T
    inv_s = (-np.sin(ang) * wgt[:, None]).T
    inv_s[:, 0] = 0.0
    inv_s[:, -1] = 0.0
    n2 = np.arange(N_FAST)
    k = k1[:, None] + N_SLOW * np.arange(N_FAST)[None, :]
    m = (k[:, :, None] * n2[None, None, :]) % N_FFT
    ang2 = 2.0 * math.pi * m / N_FFT
    gr, gi = np.cos(ang2), -np.sin(ang2)
    g_fwd = np.concatenate([np.concatenate([gr, -gi], axis=2),
                            np.concatenate([gi, gr], axis=2)], axis=1)
    grt, git = gr.transpose(0, 2, 1), -gi.transpose(0, 2, 1)
    g_inv = np.concatenate([np.concatenate([grt, -git], axis=2),
                            np.concatenate([git, grt], axis=2)], axis=1)
    f32 = lambda a: np.ascontiguousarray(a, dtype=np.float32)
    return (f32(fwd_c).reshape(-1), f32(fwd_s).reshape(-1), f32(inv_c).reshape(-1),
            f32(inv_s).reshape(-1), f32(g_fwd), f32(g_inv))


def _slow_forward(u_ref, c_ref, s_ref, k1):
    shape = u_ref.shape[1:]
    ar = jnp.zeros(shape, F32)
    ai = jnp.zeros(shape, F32)
    for n1 in range(N_IN):
        x = u_ref[n1]
        ar = ar + x * c_ref[k1 * N_IN + n1]
        ai = ai + x * s_ref[k1 * N_IN + n1]
    return jnp.concatenate([ar, ai], axis=0)


def _spectrum_kernel(c_ref, s_ref, hf_ref, hb_ref, g_ref, o_ref):
    def body(k1, carry):
        g = g_ref[k1]
        xf = _dot3(g, _slow_forward(hf_ref, c_ref, s_ref, k1))
        xb = _dot3(g, _slow_forward(hb_ref, c_ref, s_ref, k1))
        re = xf[:N_FAST] + xb[:N_FAST]
        im = xf[N_FAST:] - xb[N_FAST:]
        o_ref[k1] = jnp.concatenate([re, im], axis=0) * (1.0 / N_FFT)
        return carry

    lax.fori_loop(0, N_K1, body, 0)


def filter_spectrum(hf, hb, fwd_c, fwd_s, g_fwd):
    nblk = D_B // LANES
    slabs = lambda a: a.reshape(N_IN, N_FAST, D_B)
    smem = pl.BlockSpec(memory_space=pltpu.SMEM)
    return pl.pallas_call(
        _spectrum_kernel,
        out_shape=jax.ShapeDtypeStruct((N_K1, 2 * N_FAST, D_B), F32),
        grid=(nblk,),
        in_specs=[
            smem, smem,
            pl.BlockSpec((N_IN, N_FAST, LANES), lambda c: (0, 0, c)),
            pl.BlockSpec((N_IN, N_FAST, LANES), lambda c: (0, 0, c)),
            pl.BlockSpec((N_K1, 2 * N_FAST, 2 * N_FAST), lambda c: (0, 0, 0)),
        ],
        out_specs=pl.BlockSpec((N_K1, 2 * N_FAST, LANES), lambda c: (0, 0, c)),
        compiler_params=_cparams(("arbitrary",)),
        name="filter_spectrum",
    )(fwd_c, fwd_s, slabs(hf), slabs(hb), g_fwd)


def _short_conv_slab(x_ref, w_ref, b_ref, n1):
    r0 = pl.multiple_of(n1 * N_FAST, N_FAST)
    cur = x_ref[0, pl.ds(r0, N_FAST), :]
    before = x_ref[0, pl.ds(pl.multiple_of(jnp.maximum(r0 - 8, 0), 8), 8), :][7:8]
    after = x_ref[0, pl.ds(pl.multiple_of(jnp.minimum(r0 + N_FAST, SEQ - 8), 8), 8), :][0:1]
    before = before * (n1 > 0).astype(F32)
    after = after * (n1 < N_IN - 1).astype(F32)
    row = lax.broadcasted_iota(jnp.int32, cur.shape, 0)
    prev = jnp.where(row == 0, before, pltpu.roll(cur, 1, axis=0))
    nxt = jnp.where(row == N_FAST - 1, after, pltpu.roll(cur, N_FAST - 1, axis=0))
    return prev * w_ref[0:1, :] + cur * w_ref[1:2, :] + nxt * w_ref[2:3, :] + b_ref[...]


def _hyena_kernel(fc_ref, fs_ref, ic_ref, is_ref,
                  x0_ref, x1_ref, v_ref, w0_ref, w1_ref, wv_ref, b0_ref, b1_ref, bv_ref,
                  hs_ref, hbias_ref, gf_ref, gi_ref, o_ref, u_ref, g0_ref, z_ref):
    def conv(n1, carry):
        u_ref[n1] = _short_conv_slab(v_ref, wv_ref, bv_ref, n1) * _short_conv_slab(x1_ref, w1_ref, b1_ref, n1)
        g0_ref[n1] = _short_conv_slab(x0_ref, w0_ref, b0_ref, n1)
        return carry

    lax.fori_loop(0, N_IN, conv, 0)

    def freq(k1, carry):
        a = _slow_forward(u_ref, fc_ref, fs_ref, k1).astype(BF16)
        x = jnp.dot(gf_ref[k1], a, preferred_element_type=F32)
        h = hs_ref[k1]
        xr, xi, hr, hi = x[:N_FAST], x[N_FAST:], h[:N_FAST], h[N_FAST:]
        y = jnp.concatenate([xr * hr - xi * hi, xr * hi + xi * hr], axis=0).astype(BF16)
        z_ref[k1] = jnp.dot(gi_ref[k1], y, preferred_element_type=F32)
        return carry

    lax.fori_loop(0, N_K1, freq, 0)

    def out(n1, carry):
        y = jnp.zeros(u_ref.shape[1:], F32)
        for k1 in range(N_K1):
            y = y + z_ref[k1, :N_FAST, :] * ic_ref[n1 * N_K1 + k1]
            if 0 < k1 < N_K1 - 1:
                y = y + z_ref[k1, N_FAST:, :] * is_ref[n1 * N_K1 + k1]
        res = (y + u_ref[n1] * hbias_ref[...]) * g0_ref[n1]
        o_ref[0, pl.ds(pl.multiple_of(n1 * N_FAST, N_FAST), N_FAST), :] = res.astype(BF16)
        return carry

    lax.fori_loop(0, N_IN, out, 0)


def hyena(p3, sc_w, sc_b, h_spec, h_bias, tables):
    fwd_c, fwd_s, inv_c, inv_s, g_fwd, g_inv = tables
    b = p3.shape[0]
    nblk = D_B // LANES
    base = 3 * D_A // LANES
    smem = pl.BlockSpec(memory_space=pltpu.SMEM)
    xblk = lambda off: pl.BlockSpec((1, SEQ, LANES), lambda c, i: (i, 0, base + off * nblk + c))
    wblk = lambda off: pl.BlockSpec((3, LANES), lambda c, i: (0, off * nblk + c))
    bblk = lambda off: pl.BlockSpec((1, LANES), lambda c, i: (0, off * nblk + c))
    whole = pl.BlockSpec(memory_space=pltpu.VMEM)
    sc_b2 = sc_b.reshape(1, 3 * D_B)
    return pl.pallas_call(
        _hyena_kernel,
        out_shape=jax.ShapeDtypeStruct((b, SEQ, D_B), BF16),
        grid=(nblk, b),
        in_specs=[
            smem, smem, smem, smem,
            xblk(0), xblk(1), xblk(2),
            wblk(0), wblk(1), wblk(2),
            bblk(0), bblk(1), bblk(2),
            pl.BlockSpec((N_K1, 2 * N_FAST, LANES), lambda c, i: (0, 0, c)),
            pl.BlockSpec((1, LANES), lambda c, i: (0, c)),
            whole, whole,
        ],
        out_specs=pl.BlockSpec((1, SEQ, LANES), lambda c, i: (i, 0, c)),
        scratch_shapes=[
            pltpu.VMEM((N_IN, N_FAST, LANES), F32),
            pltpu.VMEM((N_IN, N_FAST, LANES), F32),
            pltpu.VMEM((N_K1, 2 * N_FAST, LANES), F32),
        ],
        compiler_params=_cparams(("parallel", "arbitrary")),
        name="hyena",
    )(fwd_c, fwd_s, inv_c, inv_s, p3, p3, p3, sc_w, sc_w, sc_w, sc_b2, sc_b2, sc_b2,
      h_spec, h_bias.reshape(1, D_B), g_fwd.astype(BF16), g_inv.astype(BF16))


def kernel(x_prompt, x_sample, norm_mix, norm_mlp, w_in_ab, sc_w, sc_b, q_gain, k_gain, rpb,
           f_w1, f_b1, f_w2, f_b2, f_w3, f_b3, f_wout, f_freq, h_bias, w_out_ab, w_in_c,
           v_gain, w_s, b_s, w_out_c, w_mlp1, w_mlp2):
    nb_p = x_prompt.shape[0]
    x = jnp.concatenate([x_prompt, x_sample], axis=0)
    nb = x.shape[0]
    x = x.reshape(nb * SEQ, D_MODEL)
    tables = tuple(jnp.asarray(a) for a in _dft_tables())

    for i in range(DEPTH):
        j = i // 2
        if i % 2 == 0:
            p = norm_matmul(x, norm_mix[i], w_in_ab[j].astype(BF16))
            p3 = p.reshape(nb, SEQ, p.shape[1])
            a = natten(p3, q_gain[j], k_gain[j], natten_bias_table(rpb[j]))
            hf, hb = hyena_filter(f_w1[j], f_b1[j], f_w2[j], f_b2[j], f_w3[j], f_b3[j],
                                  f_wout[j], f_freq[j])
            h_spec = filter_spectrum(hf, hb, tables[0], tables[1], tables[4])
            bo = hyena(p3, sc_w[j], sc_b[j], h_spec, h_bias[j], tables)
            wo = w_out_ab[j].astype(BF16)
            x = proj_residual(x, a.reshape(nb * SEQ, D_A), bo.reshape(nb * SEQ, D_B),
                              wo[:D_A], wo[D_A:])
        else:
            zz = norm_matmul(x, norm_mix[i], w_in_c[j].astype(BF16), act="gelu")
            b_lanes = jnp.broadcast_to(b_s[j][:, :, None], (G_C, CHUNK, LANES))
            m = gmlp_gate(zz, v_gain[j], w_s[j].astype(BF16), b_lanes)
            wo = w_out_c[j].astype(BF16)
            half = D_C // 2
            x = proj_residual(x, m, m, wo[:half], wo[half:], a_col=0, b_col=1)
        x = mlp_residual(x, norm_mlp[i], w_mlp1[i].astype(BF16), w_mlp2[i].astype(BF16))

    x = x.reshape(nb, SEQ, D_MODEL)
    return x[:nb_p], x[nb_p:]
```

```python
import functools
import math

import numpy as np
import jax
import jax.numpy as jnp
from jax import lax
from jax.experimental import pallas as pl
from jax.experimental.pallas import tpu as pltpu

F32 = jnp.float32
BF16 = jnp.bfloat16

D_MODEL = 2048
DEPTH = 4
SEQ = 4096
GRID_W = 64
WIN_H = 8
WIN_W = 16
DH = 128
H_A = 8
D_A = H_A * DH
D_B = 1024
EMB = 33
FO = 64
FAST_DECAY = 0.3
SLOW_DECAY = 1.5
DECAY_TARGET = 1e-2
D_C = D_MODEL
G_C = 8
DG_C = D_C // G_C
CHUNK = 128
D_FF = 4 * D_MODEL
EPS = 1e-6

LANES = 128
VMEM_LIMIT_BYTES = 52 * 1024 * 1024

N_FFT = 2 * SEQ
N_FAST = 128
N_SLOW = N_FFT // N_FAST
N_IN = SEQ // N_FAST
N_K1 = N_SLOW // 2 + 1


def _cparams(sem):
    return pltpu.CompilerParams(dimension_semantics=sem, vmem_limit_bytes=VMEM_LIMIT_BYTES)


def _rms_scale(x, g):
    return x * lax.rsqrt(jnp.mean(x * x, axis=-1, keepdims=True) + EPS) * g


def _gelu_tanh(x):
    return 0.5 * x * (1.0 + jnp.tanh(math.sqrt(2.0 / math.pi) * (x + 0.044715 * (x * x * x))))


def _split_bf16(x):
    hi = x.astype(BF16)
    lo = (x - hi.astype(F32)).astype(BF16)
    return hi, lo


def _dot3(a, b):
    ah, al = _split_bf16(a)
    bh, bl = _split_bf16(b)
    d = functools.partial(jnp.dot, preferred_element_type=F32)
    return d(ah, bh) + (d(ah, bl) + d(al, bh))


def _norm_matmul_kernel(x_ref, g_ref, w_ref, o_ref, h_ref, *, act):
    @pl.when(pl.program_id(1) == 0)
    def _():
        h_ref[...] = _rms_scale(x_ref[...], g_ref[...]).astype(BF16)

    acc = jnp.dot(h_ref[...], w_ref[...], preferred_element_type=F32)
    if act == "gelu":
        acc = _gelu_tanh(acc)
    o_ref[...] = acc.astype(o_ref.dtype)


def norm_matmul(x, g, w, *, act=None, tm=1024, tn=1024, out_dtype=F32):
    t, d = x.shape
    n = w.shape[1]
    return pl.pallas_call(
        functools.partial(_norm_matmul_kernel, act=act),
        out_shape=jax.ShapeDtypeStruct((t, n), out_dtype),
        grid=(t // tm, n // tn),
        in_specs=[
            pl.BlockSpec((tm, d), lambda i, j: (i, 0)),
            pl.BlockSpec((1, d), lambda i, j: (0, 0)),
            pl.BlockSpec((d, tn), lambda i, j: (0, j)),
        ],
        out_specs=pl.BlockSpec((tm, tn), lambda i, j: (i, j)),
        scratch_shapes=[pltpu.VMEM((tm, d), BF16)],
        compiler_params=_cparams(("parallel", "arbitrary")),
        name="norm_matmul",
    )(x, g.reshape(1, d), w)


def _mlp_kernel(x_ref, g_ref, w1_ref, w2_ref, o_ref, h_ref):
    @pl.when(pl.program_id(1) == 0)
    def _():
        x = x_ref[...]
        h_ref[...] = _rms_scale(x, g_ref[...]).astype(BF16)
        o_ref[...] = x

    a = jnp.dot(h_ref[...], w1_ref[...], preferred_element_type=F32)
    a = jnp.square(jnp.maximum(a, 0.0)).astype(BF16)
    o_ref[...] += jnp.dot(a, w2_ref[...], preferred_element_type=F32)


def mlp_residual(x, g, w1, w2, *, tm=512, tf=1024):
    t, d = x.shape
    f = w1.shape[1]
    return pl.pallas_call(
        _mlp_kernel,
        out_shape=jax.ShapeDtypeStruct((t, d), F32),
        grid=(t // tm, f // tf),
        in_specs=[
            pl.BlockSpec((tm, d), lambda i, k: (i, 0)),
            pl.BlockSpec((1, d), lambda i, k: (0, 0)),
            pl.BlockSpec((d, tf), lambda i, k: (0, k)),
            pl.BlockSpec((tf, d), lambda i, k: (k, 0)),
        ],
        out_specs=pl.BlockSpec((tm, d), lambda i, k: (i, 0)),
        scratch_shapes=[pltpu.VMEM((tm, d), BF16)],
        compiler_params=_cparams(("parallel", "arbitrary")),
        name="mlp_residual",
    )(x, g.reshape(1, d), w1, w2)


def _proj_residual_kernel(x_ref, a_ref, b_ref, wa_ref, wb_ref, o_ref):
    acc = jnp.dot(a_ref[...], wa_ref[...], preferred_element_type=F32)
    acc += jnp.dot(b_ref[...], wb_ref[...], preferred_element_type=F32)
    o_ref[...] = x_ref[...] + acc


def proj_residual(x, a, b, wa, wb, *, a_col=0, b_col=0, tm=1024, tn=1024):
    t, n = x.shape
    k = wa.shape[0]
    return pl.pallas_call(
        _proj_residual_kernel,
        out_shape=jax.ShapeDtypeStruct((t, n), F32),
        grid=(t // tm, n // tn),
        in_specs=[
            pl.BlockSpec((tm, tn), lambda i, j: (i, j)),
            pl.BlockSpec((tm, k), lambda i, j: (i, a_col)),
            pl.BlockSpec((tm, k), lambda i, j: (i, b_col)),
            pl.BlockSpec((k, tn), lambda i, j: (0, j)),
            pl.BlockSpec((k, tn), lambda i, j: (0, j)),
        ],
        out_specs=pl.BlockSpec((tm, tn), lambda i, j: (i, j)),
        compiler_params=_cparams(("parallel", "arbitrary")),
        name="proj_residual",
    )(x, a, b, wa, wb)


def _gate_kernel(zz_ref, vg_ref, ws_ref, bs_ref, o_ref, vn_ref, *, tm):
    vn_ref[...] = _rms_scale(zz_ref[:, D_C:], vg_ref[...]).astype(BF16)
    for c in range(tm // CHUNK):
        rows = slice(c * CHUNK, (c + 1) * CHUNK)
        for g in range(G_C):
            cols = slice(g * DG_C, (g + 1) * DG_C)
            s = jnp.dot(ws_ref[g], vn_ref[rows, cols], preferred_element_type=F32)
            bias = bs_ref[g]
            s = s + jnp.concatenate([bias] * (DG_C // LANES), axis=1)
            o_ref[rows, cols] = (zz_ref[rows, cols] * s).astype(BF16)


def gmlp_gate(zz, v_gain, w_s, b_s_lanes, *, tm=512):
    t = zz.shape[0]
    return pl.pallas_call(
        functools.partial(_gate_kernel, tm=tm),
        out_shape=jax.ShapeDtypeStruct((t, D_C), BF16),
        grid=(t // tm,),
        in_specs=[
            pl.BlockSpec((tm, 2 * D_C), lambda i: (i, 0)),
            pl.BlockSpec((1, D_C), lambda i: (0, 0)),
            pl.BlockSpec((G_C, CHUNK, CHUNK), lambda i: (0, 0, 0)),
            pl.BlockSpec((G_C, CHUNK, LANES), lambda i: (0, 0, 0)),
        ],
        out_specs=pl.BlockSpec((tm, D_C), lambda i: (i, 0)),
        scratch_shapes=[pltpu.VMEM((tm, D_C), BF16)],
        compiler_params=_cparams(("parallel",)),
        name="gmlp_gate",
    )(zz, v_gain.reshape(1, D_C), w_s, b_s_lanes)


def _head_norm_store(acc_ref, gain, o_ref):
    for h in range(H_A):
        cols = slice(h * DH, (h + 1) * DH)
        o_ref[:, cols] = _rms_scale(acc_ref[:, cols], gain).astype(BF16)


def _qkv_kernel(x_ref, g_ref, w_ref, qg_ref, kg_ref, o_ref, h_ref, acc_ref):
    j = pl.program_id(1)

    @pl.when(j == 0)
    def _():
        h_ref[...] = _rms_scale(x_ref[...], g_ref[...]).astype(BF16)

    acc_ref[...] = jnp.dot(h_ref[...], w_ref[...], preferred_element_type=F32)

    @pl.when(j == 0)
    def _():
        _head_norm_store(acc_ref, qg_ref[...] * (DH ** -0.5), o_ref)

    @pl.when(j == 1)
    def _():
        _head_norm_store(acc_ref, kg_ref[...], o_ref)

    @pl.when(j == 2)
    def _():
        o_ref[...] = acc_ref[...].astype(BF16)


def qkv_proj(x, g, w, q_gain, k_gain, *, tm=1024):
    t, d = x.shape
    return pl.pallas_call(
        _qkv_kernel,
        out_shape=jax.ShapeDtypeStruct((t, 3 * D_A), BF16),
        grid=(t // tm, 3),
        in_specs=[
            pl.BlockSpec((tm, d), lambda i, j: (i, 0)),
            pl.BlockSpec((1, d), lambda i, j: (0, 0)),
            pl.BlockSpec((d, D_A), lambda i, j: (0, j)),
            pl.BlockSpec((1, DH), lambda i, j: (0, 0)),
            pl.BlockSpec((1, DH), lambda i, j: (0, 0)),
        ],
        out_specs=pl.BlockSpec((tm, D_A), lambda i, j: (i, j)),
        scratch_shapes=[pltpu.VMEM((tm, d), BF16), pltpu.VMEM((tm, D_A), F32)],
        compiler_params=_cparams(("parallel", "arbitrary")),
        name="qkv_proj",
    )(x, g.reshape(1, d), w, q_gain.reshape(1, DH), k_gain.reshape(1, DH))


N_ROWS = SEQ // GRID_W
GROUP_ROWS = 4
N_GROUPS = N_ROWS // GROUP_ROWS
KEY_ROWS = 12
NQ = GROUP_ROWS * GRID_W
NK = KEY_ROWS * GRID_W
SOFTMAX_ROWS = 64


def _natten_group(g, slot, q_ref, k_ref, v_ref, bias_ref, o_ref, s_ref, p_ref):
    ws = jnp.clip(GROUP_ROWS * g - WIN_H // 2, 0, N_ROWS - KEY_ROWS)
    kind = jnp.where(g == 0, 0, jnp.where(g == N_GROUPS - 1, 2, 1))
    qrows = pl.ds(pl.multiple_of(g * NQ, NQ), NQ)
    krows = pl.ds(pl.multiple_of(ws * GRID_W, GRID_W), NK)
    s_ref[slot] = lax.dot_general(q_ref[0, qrows, :], k_ref[0, krows, :], (((1,), (1,)), ((), ())),
                                  preferred_element_type=F32)
    for c in range(NQ // SOFTMAX_ROWS):
        rows = slice(c * SOFTMAX_ROWS, (c + 1) * SOFTMAX_ROWS)
        s = s_ref[slot, rows, :] + bias_ref[0, kind, rows, :]
        p = jnp.exp(s - jnp.max(s, axis=-1, keepdims=True))
        p = p * (1.0 / jnp.sum(p, axis=-1, keepdims=True))
        p_ref[slot, rows, :] = p.astype(BF16)
    o = jnp.dot(p_ref[slot], v_ref[0, krows, :], preferred_element_type=F32)
    o_ref[0, qrows, :] = o.astype(BF16)


def _natten_kernel(q_ref, k_ref, v_ref, bias_ref, o_ref, s_ref, p_ref):
    def body(i, carry):
        for slot in range(2):
            _natten_group(2 * i + slot, slot, q_ref, k_ref, v_ref, bias_ref, o_ref, s_ref, p_ref)
        return carry

    lax.fori_loop(0, N_GROUPS // 2, body, 0)


def natten(qkv3, bias_tab):
    b = qkv3.shape[0]
    blk = (1, SEQ, DH)
    return pl.pallas_call(
        _natten_kernel,
        out_shape=jax.ShapeDtypeStruct((b, SEQ, D_A), BF16),
        grid=(H_A, b),
        in_specs=[
            pl.BlockSpec(blk, lambda h, i: (i, 0, h)),
            pl.BlockSpec(blk, lambda h, i: (i, 0, H_A + h)),
            pl.BlockSpec(blk, lambda h, i: (i, 0, 2 * H_A + h)),
            pl.BlockSpec((1, 3, NQ, NK), lambda h, i: (h, 0, 0, 0)),
        ],
        out_specs=pl.BlockSpec(blk, lambda h, i: (i, 0, h)),
        scratch_shapes=[pltpu.VMEM((2, NQ, NK), F32), pltpu.VMEM((2, NQ, NK), BF16)],
        compiler_params=_cparams(("parallel", "arbitrary")),
        name="natten",
    )(qkv3, qkv3, qkv3, bias_tab)


def natten_bias_table(rpb):
    col = np.arange(GRID_W)
    cs = np.clip(col - WIN_W // 2, 0, GRID_W - WIN_W)
    qc, kc = col[:, None], col[None, :]
    col_ok = (kc >= cs[:, None]) & (kc < cs[:, None] + WIN_W)
    dc = np.clip(kc - qc, -(WIN_W - 1), WIN_W - 1) + WIN_W - 1
    bias_c = rpb.astype(F32)[:, :, dc]
    a = np.arange(GROUP_ROWS)
    q_off = np.stack([a, a + WIN_H // 2, a + KEY_ROWS - GROUP_ROWS])
    w_off = np.stack([0 * a, a, 0 * a + KEY_ROWS - WIN_H])
    i = np.arange(KEY_ROWS)
    row_ok = (i >= w_off[..., None]) & (i < w_off[..., None] + WIN_H)
    dr = np.clip(i - q_off[..., None] + WIN_H - 1, 0, 2 * WIN_H - 2)
    tab = bias_c[:, dr]
    ok = row_ok[None, :, :, :, None, None] & col_ok[None, None, None, None]
    tab = jnp.where(ok, tab, -jnp.inf)
    return tab.transpose(0, 1, 2, 4, 3, 5).reshape(H_A, 3, NQ, NK)


FILT_ROWS = 512


def _filter_kernel(z_ref, t_ref, w1_ref, b1_ref, w2_ref, b2_ref, w3_ref, b3_ref, fr_ref,
                   wf_ref, wb_ref, dl_ref, hf_ref, hb_ref, h3_ref):
    @pl.when(pl.program_id(0) == 0)
    def _():
        def mlp(i, carry):
            rows = pl.ds(pl.multiple_of(i * FILT_ROWS, FILT_ROWS), FILT_ROWS)
            fr = fr_ref[...]
            h = jnp.sin(fr * (_dot3(z_ref[rows, :], w1_ref[...]) + b1_ref[...]))
            h = jnp.sin(fr * (_dot3(h, w2_ref[...]) + b2_ref[...]))
            h3_ref[rows, :] = jnp.sin(fr * (_dot3(h, w3_ref[...]) + b3_ref[...]))
            return carry

        lax.fori_loop(0, SEQ // FILT_ROWS, mlp, 0)

    def taps(i, carry):
        sf, sb = carry
        rows = pl.ds(pl.multiple_of(i * FILT_ROWS, FILT_ROWS), FILT_ROWS)
        h3 = h3_ref[rows, :]
        decay = jnp.exp(-t_ref[rows, :] * dl_ref[...])
        hf = _dot3(h3, wf_ref[...]) * decay
        hb = _dot3(h3, wb_ref[...]) * decay
        t_idx = lax.broadcasted_iota(jnp.int32, hb.shape, 0) + i * FILT_ROWS
        hb = jnp.where(t_idx == 0, 0.0, hb)
        hf_ref[rows, :] = hf
        hb_ref[rows, :] = hb
        return (sf + jnp.sum(jnp.abs(hf), axis=0, keepdims=True),
                sb + jnp.sum(jnp.abs(hb), axis=0, keepdims=True))

    zero = jnp.zeros((1, LANES), F32)
    sf, sb = lax.fori_loop(0, SEQ // FILT_ROWS, taps, (zero, zero))
    inv = 1.0 / (sf + sb)

    def scale(i, carry):
        rows = pl.ds(pl.multiple_of(i * FILT_ROWS, FILT_ROWS), FILT_ROWS)
        hf_ref[rows, :] = hf_ref[rows, :] * inv
        hb_ref[rows, :] = hb_ref[rows, :] * inv
        return carry

    lax.fori_loop(0, SEQ // FILT_ROWS, scale, 0)


def _filter_constants():
    t = np.linspace(0.0, 1.0, SEQ, dtype=np.float32).astype(np.float64)[:, None]
    bands = (EMB - 1) // 2
    w = 2.0 * math.pi * np.arange(SEQ, dtype=np.float64)[:, None] / SEQ
    f = np.linspace(1e-4, bands - 1, bands, dtype=np.float32).astype(np.float64)[None, :]
    z = np.concatenate([t, np.cos(f * w), -np.sin(f * w)], axis=-1)
    z_pad = np.zeros((SEQ, LANES), np.float32)
    z_pad[:, :EMB] = z
    t_lanes = np.broadcast_to(t.astype(np.float32), (SEQ, LANES)).copy()
    deltas = np.abs(np.linspace(math.log(DECAY_TARGET) / FAST_DECAY,
                                math.log(DECAY_TARGET) / SLOW_DECAY, D_B, dtype=np.float32))
    return z_pad, t_lanes, deltas.reshape(1, D_B)


def hyena_filter(f_w1, f_b1, f_w2, f_b2, f_w3, f_b3, f_wout, f_freq):
    z_pad, t_lanes, deltas = _filter_constants()
    w1_pad = jnp.zeros((LANES, FO), F32).at[:EMB].set(f_w1)
    nblk = D_B // LANES
    const = lambda shape: pl.BlockSpec(shape, lambda c: (0,) * len(shape))
    return pl.pallas_call(
        _filter_kernel,
        out_shape=[jax.ShapeDtypeStruct((SEQ, D_B), F32)] * 2,
        grid=(nblk,),
        in_specs=[
            const((SEQ, LANES)), const((SEQ, LANES)),
            const((LANES, FO)), const((1, FO)),
            const((FO, FO)), const((1, FO)),
            const((FO, FO)), const((1, FO)),
            const((1, FO)),
            pl.BlockSpec((FO, LANES), lambda c: (0, c)),
            pl.BlockSpec((FO, LANES), lambda c: (0, nblk + c)),
            pl.BlockSpec((1, LANES), lambda c: (0, c)),
        ],
        out_specs=[pl.BlockSpec((SEQ, LANES), lambda c: (0, c))] * 2,
        scratch_shapes=[pltpu.VMEM((SEQ, FO), F32)],
        compiler_params=_cparams(("arbitrary",)),
        name="hyena_filter",
    )(jnp.asarray(z_pad), jnp.asarray(t_lanes), w1_pad, f_b1.reshape(1, FO), f_w2,
      f_b2.reshape(1, FO), f_w3, f_b3.reshape(1, FO), f_freq.reshape(1, FO), f_wout, f_wout,
      jnp.asarray(deltas))


@functools.lru_cache(maxsize=None)
def _dft_tables():
    k1 = np.arange(N_K1)
    n1 = np.arange(N_IN)
    ang = 2.0 * math.pi * np.outer(k1, n1) / N_SLOW
    fwd_c = np.cos(ang)
    fwd_s = -np.sin(ang)
    wgt = np.full(N_K1, 2.0)
    wgt[0] = 1.0
    wgt[-1] = 1.0
    inv_c = (np.cos(ang) * wgt[:, None]).T
    inv_s = (-np.sin(ang) * wgt[:, None]).T
    inv_s[:, 0] = 0.0
    inv_s[:, -1] = 0.0
    n2 = np.arange(N_FAST)
    k = k1[:, None] + N_SLOW * np.arange(N_FAST)[None, :]
    m = (k[:, :, None] * n2[None, None, :]) % N_FFT
    ang2 = 2.0 * math.pi * m / N_FFT
    gr, gi = np.cos(ang2), -np.sin(ang2)
    g_fwd = np.concatenate([np.concatenate([gr, -gi], axis=2),
                            np.concatenate([gi, gr], axis=2)], axis=1)
    grt, git = gr.transpose(0, 2, 1), -gi.transpose(0, 2, 1)
    g_inv = np.concatenate([np.concatenate([grt, -git], axis=2),
                            np.concatenate([git, grt], axis=2)], axis=1)
    f32 = lambda a: np.ascontiguousarray(a, dtype=np.float32)
    return (f32(fwd_c).reshape(-1), f32(fwd_s).reshape(-1), f32(inv_c).reshape(-1),
            f32(inv_s).reshape(-1), f32(g_fwd), f32(g_inv))


def _slow_forward(u_ref, c_ref, s_ref, k1):
    shape = u_ref.shape[1:]
    ar = jnp.zeros(shape, F32)
    ai = jnp.zeros(shape, F32)
    for n1 in range(N_IN):
        x = u_ref[n1]
        ar = ar + x * c_ref[k1 * N_IN + n1]
        ai = ai + x * s_ref[k1 * N_IN + n1]
    return jnp.concatenate([ar, ai], axis=0)


def _spectrum_kernel(c_ref, s_ref, hf_ref, hb_ref, g_ref, o_ref):
    def body(k1, carry):
        g = g_ref[k1]
        xf = _dot3(g, _slow_forward(hf_ref, c_ref, s_ref, k1))
        xb = _dot3(g, _slow_forward(hb_ref, c_ref, s_ref, k1))
        re = xf[:N_FAST] + xb[:N_FAST]
        im = xf[N_FAST:] - xb[N_FAST:]
        o_ref[k1] = jnp.concatenate([re, im], axis=0) * (1.0 / N_FFT)
        return carry

    lax.fori_loop(0, N_K1, body, 0)


def filter_spectrum(hf, hb, fwd_c, fwd_s, g_fwd):
    nblk = D_B // LANES
    slabs = lambda a: a.reshape(N_IN, N_FAST, D_B)
    smem = pl.BlockSpec(memory_space=pltpu.SMEM)
    return pl.pallas_call(
        _spectrum_kernel,
        out_shape=jax.ShapeDtypeStruct((N_K1, 2 * N_FAST, D_B), F32),
        grid=(nblk,),
        in_specs=[
            smem, smem,
            pl.BlockSpec((N_IN, N_FAST, LANES), lambda c: (0, 0, c)),
            pl.BlockSpec((N_IN, N_FAST, LANES), lambda c: (0, 0, c)),
            pl.BlockSpec((N_K1, 2 * N_FAST, 2 * N_FAST), lambda c: (0, 0, 0)),
        ],
        out_specs=pl.BlockSpec((N_K1, 2 * N_FAST, LANES), lambda c: (0, 0, c)),
        compiler_params=_cparams(("arbitrary",)),
        name="filter_spectrum",
    )(fwd_c, fwd_s, slabs(hf), slabs(hb), g_fwd)


def _short_conv_slab(x_ref, w_ref, b_ref, n1):
    r0 = pl.multiple_of(n1 * N_FAST, N_FAST)
    cur = x_ref[0, pl.ds(r0, N_FAST), :]
    before = x_ref[0, pl.ds(pl.multiple_of(jnp.maximum(r0 - 8, 0), 8), 8), :][7:8]
    after = x_ref[0, pl.ds(pl.multiple_of(jnp.minimum(r0 + N_FAST, SEQ - 8), 8), 8), :][0:1]
    before = before * (n1 > 0).astype(F32)
    after = after * (n1 < N_IN - 1).astype(F32)
    row = lax.broadcasted_iota(jnp.int32, cur.shape, 0)
    prev = jnp.where(row == 0, before, pltpu.roll(cur, 1, axis=0))
    nxt = jnp.where(row == N_FAST - 1, after, pltpu.roll(cur, N_FAST - 1, axis=0))
    return prev * w_ref[0:1, :] + cur * w_ref[1:2, :] + nxt * w_ref[2:3, :] + b_ref[...]


def _hyena_kernel(fc_ref, fs_ref, ic_ref, is_ref,
                  x0_ref, x1_ref, v_ref, w0_ref, w1_ref, wv_ref, b0_ref, b1_ref, bv_ref,
                  hs_ref, hbias_ref, gf_ref, gi_ref, o_ref, u_ref, g0_ref, z_ref):
    def conv(n1, carry):
        u_ref[n1] = _short_conv_slab(v_ref, wv_ref, bv_ref, n1) * _short_conv_slab(x1_ref, w1_ref, b1_ref, n1)
        g0_ref[n1] = _short_conv_slab(x0_ref, w0_ref, b0_ref, n1)
        return carry

    lax.fori_loop(0, N_IN, conv, 0)

    def freq(k1, carry):
        a = _slow_forward(u_ref, fc_ref, fs_ref, k1).astype(BF16)
        x = jnp.dot(gf_ref[k1], a, preferred_element_type=F32)
        h = hs_ref[k1]
        xr, xi, hr, hi = x[:N_FAST], x[N_FAST:], h[:N_FAST], h[N_FAST:]
        y = jnp.concatenate([xr * hr - xi * hi, xr * hi + xi * hr], axis=0).astype(BF16)
        z_ref[k1] = jnp.dot(gi_ref[k1], y, preferred_element_type=F32)
        return carry

    lax.fori_loop(0, N_K1, freq, 0)

    def out(n1, carry):
        y = jnp.zeros(u_ref.shape[1:], F32)
        for k1 in range(N_K1):
            y = y + z_ref[k1, :N_FAST, :] * ic_ref[n1 * N_K1 + k1]
            if 0 < k1 < N_K1 - 1:
                y = y + z_ref[k1, N_FAST:, :] * is_ref[n1 * N_K1 + k1]
        res = (y + u_ref[n1] * hbias_ref[...]) * g0_ref[n1]
        o_ref[0, pl.ds(pl.multiple_of(n1 * N_FAST, N_FAST), N_FAST), :] = res.astype(BF16)
        return carry

    lax.fori_loop(0, N_IN, out, 0)


def hyena(p3, sc_w, sc_b, h_spec, h_bias, tables):
    fwd_c, fwd_s, inv_c, inv_s, g_fwd, g_inv = tables
    b = p3.shape[0]
    nblk = D_B // LANES
    smem = pl.BlockSpec(memory_space=pltpu.SMEM)
    xblk = lambda off: pl.BlockSpec((1, SEQ, LANES), lambda c, i: (i, 0, off * nblk + c))
    wblk = lambda off: pl.BlockSpec((3, LANES), lambda c, i: (0, off * nblk + c))
    bblk = lambda off: pl.BlockSpec((1, LANES), lambda c, i: (0, off * nblk + c))
    whole = pl.BlockSpec(memory_space=pltpu.VMEM)
    sc_b2 = sc_b.reshape(1, 3 * D_B)
    return pl.pallas_call(
        _hyena_kernel,
        out_shape=jax.ShapeDtypeStruct((b, SEQ, D_B), BF16),
        grid=(nblk, b),
        in_specs=[
            smem, smem, smem, smem,
            xblk(0), xblk(1), xblk(2),
            wblk(0), wblk(1), wblk(2),
            bblk(0), bblk(1), bblk(2),
            pl.BlockSpec((N_K1, 2 * N_FAST, LANES), lambda c, i: (0, 0, c)),
            pl.BlockSpec((1, LANES), lambda c, i: (0, c)),
            whole, whole,
        ],
        out_specs=pl.BlockSpec((1, SEQ, LANES), lambda c, i: (i, 0, c)),
        scratch_shapes=[
            pltpu.VMEM((N_IN, N_FAST, LANES), F32),
            pltpu.VMEM((N_IN, N_FAST, LANES), F32),
            pltpu.VMEM((N_K1, 2 * N_FAST, LANES), F32),
        ],
        compiler_params=_cparams(("parallel", "arbitrary")),
        name="hyena",
    )(fwd_c, fwd_s, inv_c, inv_s, p3, p3, p3, sc_w, sc_w, sc_w, sc_b2, sc_b2, sc_b2,
      h_spec, h_bias.reshape(1, D_B), g_fwd.astype(BF16), g_inv.astype(BF16))


def kernel(x_prompt, x_sample, norm_mix, norm_mlp, w_in_ab, sc_w, sc_b, q_gain, k_gain, rpb,
           f_w1, f_b1, f_w2, f_b2, f_w3, f_b3, f_wout, f_freq, h_bias, w_out_ab, w_in_c,
           v_gain, w_s, b_s, w_out_c, w_mlp1, w_mlp2):
    nb_p = x_prompt.shape[0]
    x = jnp.concatenate([x_prompt, x_sample], axis=0)
    nb = x.shape[0]
    x = x.reshape(nb * SEQ, D_MODEL)
    tables = tuple(jnp.asarray(a) for a in _dft_tables())

    for i in range(DEPTH):
        j = i // 2
        if i % 2 == 0:
            w_in = w_in_ab[j].astype(BF16)
            qkv = qkv_proj(x, norm_mix[i], w_in, q_gain[j], k_gain[j])
            a = natten(qkv.reshape(nb, SEQ, 3 * D_A), natten_bias_table(rpb[j]))
            pz = norm_matmul(x, norm_mix[i], w_in[:, 3 * D_A:])
            hf, hb = hyena_filter(f_w1[j], f_b1[j], f_w2[j], f_b2[j], f_w3[j], f_b3[j],
                                  f_wout[j], f_freq[j])
            h_spec = filter_spectrum(hf, hb, tables[0], tables[1], tables[4])
            bo = hyena(pz.reshape(nb, SEQ, 3 * D_B), sc_w[j], sc_b[j], h_spec, h_bias[j], tables)
            wo = w_out_ab[j].astype(BF16)
            x = proj_residual(x, a.reshape(nb * SEQ, D_A), bo.reshape(nb * SEQ, D_B),
                              wo[:D_A], wo[D_A:])
        else:
            zz = norm_matmul(x, norm_mix[i], w_in_c[j].astype(BF16), act="gelu")
            b_lanes = jnp.broadcast_to(b_s[j][:, :, None], (G_C, CHUNK, LANES))
            m = gmlp_gate(zz, v_gain[j], w_s[j].astype(BF16), b_lanes)
            wo = w_out_c[j].astype(BF16)
            half = D_C // 2
            x = proj_residual(x, m, m, wo[:half], wo[half:], a_col=0, b_col=1)
        x = mlp_residual(x, norm_mlp[i], w_mlp1[i].astype(BF16), w_mlp2[i].astype(BF16))

    x = x.reshape(nb, SEQ, D_MODEL)
    return x[:nb_p], x[nb_p:]
```

```python
import functools
import math

import numpy as np
import jax
import jax.numpy as jnp
from jax import lax
from jax.experimental import pallas as pl
from jax.experimental.pallas import tpu as pltpu

F32 = jnp.float32
BF16 = jnp.bfloat16

D_MODEL = 2048
DEPTH = 4
SEQ = 4096
GRID_W = 64
WIN_H = 8
WIN_W = 16
DH = 128
H_A = 8
D_A = H_A * DH
D_B = 1024
EMB = 33
FO = 64
FAST_DECAY = 0.3
SLOW_DECAY = 1.5
DECAY_TARGET = 1e-2
D_C = D_MODEL
G_C = 8
DG_C = D_C // G_C
CHUNK = 128
D_FF = 4 * D_MODEL
EPS = 1e-6

LANES = 128
VMEM_LIMIT_BYTES = 52 * 1024 * 1024

N_FFT = 2 * SEQ
N_FAST = 128
N_SLOW = N_FFT // N_FAST
N_IN = SEQ // N_FAST
N_K1 = N_SLOW // 2 + 1


def _cparams(sem):
    return pltpu.CompilerParams(dimension_semantics=sem, vmem_limit_bytes=VMEM_LIMIT_BYTES)


def _rms_scale(x, g):
    return x * lax.rsqrt(jnp.mean(x * x, axis=-1, keepdims=True) + EPS) * g


def _gelu_tanh(x):
    return 0.5 * x * (1.0 + jnp.tanh(math.sqrt(2.0 / math.pi) * (x + 0.044715 * (x * x * x))))


def _split_bf16(x):
    hi = x.astype(BF16)
    lo = (x - hi.astype(F32)).astype(BF16)
    return hi, lo


def _dot3(a, b):
    ah, al = _split_bf16(a)
    bh, bl = _split_bf16(b)
    d = functools.partial(jnp.dot, preferred_element_type=F32)
    return d(ah, bh) + (d(ah, bl) + d(al, bh))


def _norm_matmul_kernel(x_ref, g_ref, w_ref, o_ref, h_ref, *, act):
    @pl.when(pl.program_id(1) == 0)
    def _():
        h_ref[...] = _rms_scale(x_ref[...], g_ref[...]).astype(BF16)

    acc = jnp.dot(h_ref[...], w_ref[...], preferred_element_type=F32)
    if act == "gelu":
        acc = _gelu_tanh(acc)
    o_ref[...] = acc.astype(o_ref.dtype)


def norm_matmul(x, g, w, *, act=None, tm=1024, tn=1024, out_dtype=F32):
    t, d = x.shape
    n = w.shape[1]
    return pl.pallas_call(
        functools.partial(_norm_matmul_kernel, act=act),
        out_shape=jax.ShapeDtypeStruct((t, n), out_dtype),
        grid=(t // tm, n // tn),
        in_specs=[
            pl.BlockSpec((tm, d), lambda i, j: (i, 0)),
            pl.BlockSpec((1, d), lambda i, j: (0, 0)),
            pl.BlockSpec((d, tn), lambda i, j: (0, j)),
        ],
        out_specs=pl.BlockSpec((tm, tn), lambda i, j: (i, j)),
        scratch_shapes=[pltpu.VMEM((tm, d), BF16)],
        compiler_params=_cparams(("parallel", "arbitrary")),
        name="norm_matmul",
    )(x, g.reshape(1, d), w)


def _mlp_kernel(x_ref, g_ref, w1_ref, w2_ref, o_ref, h_ref):
    @pl.when(pl.program_id(1) == 0)
    def _():
        x = x_ref[...]
        h_ref[...] = _rms_scale(x, g_ref[...]).astype(BF16)
        o_ref[...] = x

    a = jnp.dot(h_ref[...], w1_ref[...], preferred_element_type=F32)
    a = jnp.square(jnp.maximum(a, 0.0)).astype(BF16)
    o_ref[...] += jnp.dot(a, w2_ref[...], preferred_element_type=F32)


def mlp_residual(x, g, w1, w2, *, tm=512, tf=1024):
    t, d = x.shape
    f = w1.shape[1]
    return pl.pallas_call(
        _mlp_kernel,
        out_shape=jax.ShapeDtypeStruct((t, d), F32),
        grid=(t // tm, f // tf),
        in_specs=[
            pl.BlockSpec((tm, d), lambda i, k: (i, 0)),
            pl.BlockSpec((1, d), lambda i, k: (0, 0)),
            pl.BlockSpec((d, tf), lambda i, k: (0, k)),
            pl.BlockSpec((tf, d), lambda i, k: (k, 0)),
        ],
        out_specs=pl.BlockSpec((tm, d), lambda i, k: (i, 0)),
        scratch_shapes=[pltpu.VMEM((tm, d), BF16)],
        compiler_params=_cparams(("parallel", "arbitrary")),
        name="mlp_residual",
    )(x, g.reshape(1, d), w1, w2)


def _proj_residual_kernel(x_ref, a_ref, b_ref, wa_ref, wb_ref, o_ref):
    acc = jnp.dot(a_ref[...], wa_ref[...], preferred_element_type=F32)
    acc += jnp.dot(b_ref[...], wb_ref[...], preferred_element_type=F32)
    o_ref[...] = x_ref[...] + acc


def proj_residual(x, a, b, wa, wb, *, a_col=0, b_col=0, tm=1024, tn=1024):
    t, n = x.shape
    k = wa.shape[0]
    return pl.pallas_call(
        _proj_residual_kernel,
        out_shape=jax.ShapeDtypeStruct((t, n), F32),
        grid=(t // tm, n // tn),
        in_specs=[
            pl.BlockSpec((tm, tn), lambda i, j: (i, j)),
            pl.BlockSpec((tm, k), lambda i, j: (i, a_col)),
            pl.BlockSpec((tm, k), lambda i, j: (i, b_col)),
            pl.BlockSpec((k, tn), lambda i, j: (0, j)),
            pl.BlockSpec((k, tn), lambda i, j: (0, j)),
        ],
        out_specs=pl.BlockSpec((tm, tn), lambda i, j: (i, j)),
        compiler_params=_cparams(("parallel", "arbitrary")),
        name="proj_residual",
    )(x, a, b, wa, wb)


def _gate_kernel(zz_ref, vg_ref, ws_ref, bs_ref, o_ref, vn_ref, *, tm):
    vn_ref[...] = _rms_scale(zz_ref[:, D_C:], vg_ref[...]).astype(BF16)
    for c in range(tm // CHUNK):
        rows = slice(c * CHUNK, (c + 1) * CHUNK)
        for g in range(G_C):
            cols = slice(g * DG_C, (g + 1) * DG_C)
            s = jnp.dot(ws_ref[g], vn_ref[rows, cols], preferred_element_type=F32)
            bias = bs_ref[g]
            s = s + jnp.concatenate([bias] * (DG_C // LANES), axis=1)
            o_ref[rows, cols] = (zz_ref[rows, cols] * s).astype(BF16)


def gmlp_gate(zz, v_gain, w_s, b_s_lanes, *, tm=512):
    t = zz.shape[0]
    return pl.pallas_call(
        functools.partial(_gate_kernel, tm=tm),
        out_shape=jax.ShapeDtypeStruct((t, D_C), BF16),
        grid=(t // tm,),
        in_specs=[
            pl.BlockSpec((tm, 2 * D_C), lambda i: (i, 0)),
            pl.BlockSpec((1, D_C), lambda i: (0, 0)),
            pl.BlockSpec((G_C, CHUNK, CHUNK), lambda i: (0, 0, 0)),
            pl.BlockSpec((G_C, CHUNK, LANES), lambda i: (0, 0, 0)),
        ],
        out_specs=pl.BlockSpec((tm, D_C), lambda i: (i, 0)),
        scratch_shapes=[pltpu.VMEM((tm, D_C), BF16)],
        compiler_params=_cparams(("parallel",)),
        name="gmlp_gate",
    )(zz, v_gain.reshape(1, D_C), w_s, b_s_lanes)


def _head_norm_store(acc_ref, gain, o_ref):
    for h in range(H_A):
        cols = slice(h * DH, (h + 1) * DH)
        o_ref[:, cols] = _rms_scale(acc_ref[:, cols], gain).astype(BF16)


def _qkv_kernel(x_ref, g_ref, w_ref, qg_ref, kg_ref, o_ref, h_ref, acc_ref):
    j = pl.program_id(1)

    @pl.when(j == 0)
    def _():
        h_ref[...] = _rms_scale(x_ref[...], g_ref[...]).astype(BF16)

    acc_ref[...] = jnp.dot(h_ref[...], w_ref[...], preferred_element_type=F32)

    @pl.when(j == 0)
    def _():
        _head_norm_store(acc_ref, qg_ref[...] * (DH ** -0.5), o_ref)

    @pl.when(j == 1)
    def _():
        _head_norm_store(acc_ref, kg_ref[...], o_ref)

    @pl.when(j == 2)
    def _():
        o_ref[...] = acc_ref[...].astype(BF16)


def qkv_proj(x, g, w, q_gain, k_gain, *, tm=1024):
    t, d = x.shape
    return pl.pallas_call(
        _qkv_kernel,
        out_shape=jax.ShapeDtypeStruct((t, 3 * D_A), BF16),
        grid=(t // tm, 3),
        in_specs=[
            pl.BlockSpec((tm, d), lambda i, j: (i, 0)),
            pl.BlockSpec((1, d), lambda i, j: (0, 0)),
            pl.BlockSpec((d, D_A), lambda i, j: (0, j)),
            pl.BlockSpec((1, DH), lambda i, j: (0, 0)),
            pl.BlockSpec((1, DH), lambda i, j: (0, 0)),
        ],
        out_specs=pl.BlockSpec((tm, D_A), lambda i, j: (i, j)),
        scratch_shapes=[pltpu.VMEM((tm, d), BF16), pltpu.VMEM((tm, D_A), F32)],
        compiler_params=_cparams(("parallel", "arbitrary")),
        name="qkv_proj",
    )(x, g.reshape(1, d), w, q_gain.reshape(1, DH), k_gain.reshape(1, DH))


N_ROWS = SEQ // GRID_W
GROUP_ROWS = 4
N_GROUPS = N_ROWS // GROUP_ROWS
KEY_ROWS = 12
NQ = GROUP_ROWS * GRID_W
NK = KEY_ROWS * GRID_W
SOFTMAX_ROWS = 64


def _natten_group(g, slot, q_ref, k_ref, v_ref, bias_ref, o_ref, s_ref, p_ref):
    ws = jnp.clip(GROUP_ROWS * g - WIN_H // 2, 0, N_ROWS - KEY_ROWS)
    kind = jnp.where(g == 0, 0, jnp.where(g == N_GROUPS - 1, 2, 1))
    qrows = pl.ds(pl.multiple_of(g * NQ, NQ), NQ)
    krows = pl.ds(pl.multiple_of(ws * GRID_W, GRID_W), NK)
    s_ref[slot] = lax.dot_general(q_ref[0, qrows, :], k_ref[0, krows, :], (((1,), (1,)), ((), ())),
                                  preferred_element_type=F32)
    for c in range(NQ // SOFTMAX_ROWS):
        rows = slice(c * SOFTMAX_ROWS, (c + 1) * SOFTMAX_ROWS)
        s = s_ref[slot, rows, :] + bias_ref[0, kind, rows, :]
        p = jnp.exp(s - jnp.max(s, axis=-1, keepdims=True))
        p = p * (1.0 / jnp.sum(p, axis=-1, keepdims=True))
        p_ref[slot, rows, :] = p.astype(BF16)
    o = jnp.dot(p_ref[slot], v_ref[0, krows, :], preferred_element_type=F32)
    o_ref[0, qrows, :] = o.astype(BF16)


def _natten_kernel(q_ref, k_ref, v_ref, bias_ref, o_ref, s_ref, p_ref):
    def body(i, carry):
        for slot in range(2):
            _natten_group(2 * i + slot, slot, q_ref, k_ref, v_ref, bias_ref, o_ref, s_ref, p_ref)
        return carry

    lax.fori_loop(0, N_GROUPS // 2, body, 0)


def natten(qkv3, bias_tab):
    b = qkv3.shape[0]
    blk = (1, SEQ, DH)
    return pl.pallas_call(
        _natten_kernel,
        out_shape=jax.ShapeDtypeStruct((b, SEQ, D_A), BF16),
        grid=(H_A, b),
        in_specs=[
            pl.BlockSpec(blk, lambda h, i: (i, 0, h)),
            pl.BlockSpec(blk, lambda h, i: (i, 0, H_A + h)),
            pl.BlockSpec(blk, lambda h, i: (i, 0, 2 * H_A + h)),
            pl.BlockSpec((1, 3, NQ, NK), lambda h, i: (h, 0, 0, 0)),
        ],
        out_specs=pl.BlockSpec(blk, lambda h, i: (i, 0, h)),
        scratch_shapes=[pltpu.VMEM((2, NQ, NK), F32), pltpu.VMEM((2, NQ, NK), BF16)],
        compiler_params=_cparams(("parallel", "arbitrary")),
        name="natten",
    )(qkv3, qkv3, qkv3, bias_tab)


def natten_bias_table(rpb):
    col = np.arange(GRID_W)
    cs = np.clip(col - WIN_W // 2, 0, GRID_W - WIN_W)
    qc, kc = col[:, None], col[None, :]
    col_ok = (kc >= cs[:, None]) & (kc < cs[:, None] + WIN_W)
    dc = np.clip(kc - qc, -(WIN_W - 1), WIN_W - 1) + WIN_W - 1
    bias_c = rpb.astype(F32)[:, :, dc]
    a = np.arange(GROUP_ROWS)
    q_off = np.stack([a, a + WIN_H // 2, a + KEY_ROWS - GROUP_ROWS])
    w_off = np.stack([0 * a, a, 0 * a + KEY_ROWS - WIN_H])
    i = np.arange(KEY_ROWS)
    row_ok = (i >= w_off[..., None]) & (i < w_off[..., None] + WIN_H)
    dr = np.clip(i - q_off[..., None] + WIN_H - 1, 0, 2 * WIN_H - 2)
    tab = bias_c[:, dr]
    ok = row_ok[None, :, :, :, None, None] & col_ok[None, None, None, None]
    tab = jnp.where(ok, tab, -jnp.inf)
    return tab.transpose(0, 1, 2, 4, 3, 5).reshape(H_A, 3, NQ, NK)


FILT_ROWS = 512


def _filter_kernel(z_ref, t_ref, w1_ref, b1_ref, w2_ref, b2_ref, w3_ref, b3_ref, fr_ref,
                   wf_ref, wb_ref, dl_ref, hf_ref, hb_ref, h3_ref):
    @pl.when(pl.program_id(0) == 0)
    def _():
        def mlp(i, carry):
            rows = pl.ds(pl.multiple_of(i * FILT_ROWS, FILT_ROWS), FILT_ROWS)
            fr = fr_ref[...]
            h = jnp.sin(fr * (_dot3(z_ref[rows, :], w1_ref[...]) + b1_ref[...]))
            h = jnp.sin(fr * (_dot3(h, w2_ref[...]) + b2_ref[...]))
            h3_ref[rows, :] = jnp.sin(fr * (_dot3(h, w3_ref[...]) + b3_ref[...]))
            return carry

        lax.fori_loop(0, SEQ // FILT_ROWS, mlp, 0)

    def taps(i, carry):
        sf, sb = carry
        rows = pl.ds(pl.multiple_of(i * FILT_ROWS, FILT_ROWS), FILT_ROWS)
        h3 = h3_ref[rows, :]
        decay = jnp.exp(-t_ref[rows, :] * dl_ref[...])
        hf = _dot3(h3, wf_ref[...]) * decay
        hb = _dot3(h3, wb_ref[...]) * decay
        t_idx = lax.broadcasted_iota(jnp.int32, hb.shape, 0) + i * FILT_ROWS
        hb = jnp.where(t_idx == 0, 0.0, hb)
        hf_ref[rows, :] = hf
        hb_ref[rows, :] = hb
        return (sf + jnp.sum(jnp.abs(hf), axis=0, keepdims=True),
                sb + jnp.sum(jnp.abs(hb), axis=0, keepdims=True))

    zero = jnp.zeros((1, LANES), F32)
    sf, sb = lax.fori_loop(0, SEQ // FILT_ROWS, taps, (zero, zero))
    inv = 1.0 / (sf + sb)

    def scale(i, carry):
        rows = pl.ds(pl.multiple_of(i * FILT_ROWS, FILT_ROWS), FILT_ROWS)
        hf_ref[rows, :] = hf_ref[rows, :] * inv
        hb_ref[rows, :] = hb_ref[rows, :] * inv
        return carry

    lax.fori_loop(0, SEQ // FILT_ROWS, scale, 0)


def _filter_constants():
    t = np.linspace(0.0, 1.0, SEQ, dtype=np.float32).astype(np.float64)[:, None]
    bands = (EMB - 1) // 2
    w = 2.0 * math.pi * np.arange(SEQ, dtype=np.float64)[:, None] / SEQ
    f = np.linspace(1e-4, bands - 1, bands, dtype=np.float32).astype(np.float64)[None, :]
    z = np.concatenate([t, np.cos(f * w), -np.sin(f * w)], axis=-1)
    z_pad = np.zeros((SEQ, LANES), np.float32)
    z_pad[:, :EMB] = z
    t_lanes = np.broadcast_to(t.astype(np.float32), (SEQ, LANES)).copy()
    deltas = np.abs(np.linspace(math.log(DECAY_TARGET) / FAST_DECAY,
                                math.log(DECAY_TARGET) / SLOW_DECAY, D_B, dtype=np.float32))
    return z_pad, t_lanes, deltas.reshape(1, D_B)


def hyena_filter(f_w1, f_b1, f_w2, f_b2, f_w3, f_b3, f_wout, f_freq):
    z_pad, t_lanes, deltas = _filter_constants()
    w1_pad = jnp.zeros((LANES, FO), F32).at[:EMB].set(f_w1)
    nblk = D_B // LANES
    const = lambda shape: pl.BlockSpec(shape, lambda c: (0,) * len(shape))
    return pl.pallas_call(
        _filter_kernel,
        out_shape=[jax.ShapeDtypeStruct((SEQ, D_B), F32)] * 2,
        grid=(nblk,),
        in_specs=[
            const((SEQ, LANES)), const((SEQ, LANES)),
            const((LANES, FO)), const((1, FO)),
            const((FO, FO)), const((1, FO)),
            const((FO, FO)), const((1, FO)),
            const((1, FO)),
            pl.BlockSpec((FO, LANES), lambda c: (0, c)),
            pl.BlockSpec((FO, LANES), lambda c: (0, nblk + c)),
            pl.BlockSpec((1, LANES), lambda c: (0, c)),
        ],
        out_specs=[pl.BlockSpec((SEQ, LANES), lambda c: (0, c))] * 2,
        scratch_shapes=[pltpu.VMEM((SEQ, FO), F32)],
        compiler_params=_cparams(("arbitrary",)),
        name="hyena_filter",
    )(jnp.asarray(z_pad), jnp.asarray(t_lanes), w1_pad, f_b1.reshape(1, FO), f_w2,
      f_b2.reshape(1, FO), f_w3, f_b3.reshape(1, FO), f_freq.reshape(1, FO), f_wout, f_wout,
      jnp.asarray(deltas))


ROW_TILE = 8
N_K1_PAD = N_K1 + 1


@functools.lru_cache(maxsize=None)
def _dft_tables():
    k1 = np.arange(N_K1_PAD)
    n2 = np.arange(N_FAST)
    ang = 2.0 * math.pi * np.outer(k1, n2) / N_FFT
    tw = np.concatenate([np.cos(ang), -np.sin(ang)], axis=1)
    tw = np.broadcast_to(tw[:, :, None], (N_K1_PAD, 2 * N_FAST, LANES))
    ang2 = 2.0 * math.pi * np.outer(n2, n2) / N_FAST
    fr, fi = np.cos(ang2), -np.sin(ang2)
    f_fwd = np.block([[fr, -fi], [fi, fr]])
    f_inv = np.block([[fr, fi], [-fi, fr]])
    f32 = lambda a: np.ascontiguousarray(a, dtype=np.float32)
    return f32(tw), f32(f_fwd), f32(f_inv)


def _cmul_const(x, w):
    xr, xi = x
    c, s = w.real, w.imag
    eps = 1e-12
    if abs(s) < eps and abs(c - 1.0) < eps:
        return xr, xi
    if abs(c) < eps and abs(s - 1.0) < eps:
        return ("neg", xi), xr
    if abs(c) < eps and abs(s + 1.0) < eps:
        return xi, ("neg", xr)
    return xr * c - xi * s, xr * s + xi * c


def _is_neg(v):
    return isinstance(v, tuple) and v[0] == "neg"


def _plain(v):
    return -v[1] if _is_neg(v) else v


def _add(a, b):
    return a - b[1] if _is_neg(b) else a + b


def _sub(a, b):
    return a + b[1] if _is_neg(b) else a - b


def _fft_dit(xs, sign):
    n = len(xs)
    if n == 1:
        return xs
    ev = _fft_dit(xs[0::2], sign)
    od = _fft_dit(xs[1::2], sign)
    out = [None] * n
    for k in range(n // 2):
        w = complex(math.cos(2 * math.pi * k / n), sign * math.sin(2 * math.pi * k / n))
        tr, ti = _cmul_const(od[k], w)
        er, ei = ev[k]
        out[k] = (_add(er, tr), _add(ei, ti))
        out[k + n // 2] = (_sub(er, tr), _sub(ei, ti))
    return out


def _slow_forward(x):
    half, quarter = N_SLOW // 2, N_SLOW // 4
    z = [(x[2 * m], x[2 * m + 1]) for m in range(quarter)]
    zt = [z[0]]
    for m in range(1, quarter):
        tr, ti = _cmul_const(z[m], complex(math.cos(2 * math.pi * m / half), -math.sin(2 * math.pi * m / half)))
        zt.append((_plain(tr), _plain(ti)))
    ze = _fft_dit(z, -1)
    zo = _fft_dit(zt, -1)
    zc = [None] * half
    for j in range(quarter):
        zc[2 * j] = ze[j]
        zc[2 * j + 1] = zo[j]
    out = [None] * (half + 1)
    ar, ai = zc[0]
    out[0] = ((ar + ai) * 2.0, None)
    out[half] = ((ar - ai) * 2.0, None)
    ar, ai = zc[quarter]
    out[quarter] = (ar * 2.0, ai * -2.0)
    for k in range(1, quarter):
        ar, ai = zc[k]
        br, bi = zc[half - k]
        c, s = math.cos(2 * math.pi * k / N_SLOW), math.sin(2 * math.pi * k / N_SLOW)
        p, q, r, t = ar + br, ai - bi, ai + bi, ar - br
        u = r * c - t * s
        vn = t * (-c) - r * s
        out[k] = (p + u, q + vn)
        out[half - k] = (p - u, vn - q)
    return out


def _slow_inverse(xs):
    half, quarter = N_SLOW // 2, N_SLOW // 4
    zc = [None] * half
    ar, br = xs[0][0], xs[half][0]
    zc[0] = (ar + br, ar - br)
    ar, ai = xs[quarter]
    zc[quarter] = (ar * 2.0, ai * -2.0)
    for k in range(1, quarter):
        ar, ai = xs[k]
        br, bi = xs[half - k]
        c, s = math.cos(2 * math.pi * k / N_SLOW), math.sin(2 * math.pi * k / N_SLOW)
        p, q, t, r = ar + br, ai - bi, ar - br, ai + bi
        o_r = t * c - r * s
        o_i = t * s + r * c
        zc[k] = (p - o_i, q + o_r)
        zc[half - k] = (p + o_i, o_r - q)
    ze = _fft_dit(zc[0::2], +1)
    zo = _fft_dit(zc[1::2], +1)
    out = [None] * N_IN
    for m in range(quarter):
        tr, ti = _cmul_const(zo[m], complex(math.cos(2 * math.pi * m / half), math.sin(2 * math.pi * m / half)))
        out[2 * m] = _add(ze[m][0], tr)
        out[2 * m + 1] = _add(ze[m][1], ti)
    return out


def _slow_forward_pass(src_ref, a_ref):
    lanes = src_ref.shape[-1]
    zeros = jnp.zeros((ROW_TILE, lanes), F32)

    def body(j, carry):
        rows = pl.ds(pl.multiple_of(j * ROW_TILE, ROW_TILE), ROW_TILE)
        rows_im = pl.ds(pl.multiple_of(N_FAST + j * ROW_TILE, ROW_TILE), ROW_TILE)
        out = _slow_forward([src_ref[n1, rows, :] for n1 in range(N_IN)])
        for k1, (re, im) in enumerate(out):
            a_ref[k1, rows, :] = re
            a_ref[k1, rows_im, :] = zeros if im is None else im
        a_ref[N_K1, rows, :] = zeros
        a_ref[N_K1, rows_im, :] = zeros
        return carry

    lax.fori_loop(0, N_FAST // ROW_TILE, body, 0)


def _cmul(a, w):
    ar, ai, wr, wi = a[:N_FAST], a[N_FAST:], w[:N_FAST], w[N_FAST:]
    return ar * wr - ai * wi, ar * wi + ai * wr


def _cmul_conj(a, w):
    ar, ai, wr, wi = a[:N_FAST], a[N_FAST:], w[:N_FAST], w[N_FAST:]
    return ar * wr + ai * wi, ai * wr - ar * wi


def _spectrum_kernel(hf_ref, hb_ref, tw_ref, f_ref, o_ref, af_ref, ab_ref):
    _slow_forward_pass(hf_ref, af_ref)
    _slow_forward_pass(hb_ref, ab_ref)

    def body(k1, carry):
        tw = tw_ref[k1]
        f = f_ref[...]
        xf = _dot3(f, jnp.concatenate(_cmul(af_ref[k1], tw), axis=0))
        xb = _dot3(f, jnp.concatenate(_cmul(ab_ref[k1], tw), axis=0))
        re = xf[:N_FAST] + xb[:N_FAST]
        im = xf[N_FAST:] - xb[N_FAST:]
        o_ref[k1] = jnp.concatenate([re, im], axis=0) * (1.0 / (4 * N_FFT))
        return carry

    lax.fori_loop(0, N_K1_PAD, body, 0)


def filter_spectrum(hf, hb, tw, f_fwd):
    nblk = D_B // LANES
    slabs = lambda a: a.reshape(N_IN, N_FAST, D_B)
    whole = pl.BlockSpec(memory_space=pltpu.VMEM)
    spec_blk = (N_K1_PAD, 2 * N_FAST, LANES)
    return pl.pallas_call(
        _spectrum_kernel,
        out_shape=jax.ShapeDtypeStruct((N_K1_PAD, 2 * N_FAST, D_B), F32),
        grid=(nblk,),
        in_specs=[
            pl.BlockSpec((N_IN, N_FAST, LANES), lambda c: (0, 0, c)),
            pl.BlockSpec((N_IN, N_FAST, LANES), lambda c: (0, 0, c)),
            whole, whole,
        ],
        out_specs=pl.BlockSpec(spec_blk, lambda c: (0, 0, c)),
        scratch_shapes=[pltpu.VMEM(spec_blk, F32)] * 2,
        compiler_params=_cparams(("arbitrary",)),
        name="filter_spectrum",
    )(slabs(hf), slabs(hb), tw, f_fwd)


def _short_conv_slab(x_ref, w_ref, b_ref, n1):
    r0 = pl.multiple_of(n1 * N_FAST, N_FAST)
    cur = x_ref[0, pl.ds(r0, N_FAST), :]
    before = x_ref[0, pl.ds(pl.multiple_of(jnp.maximum(r0 - 8, 0), 8), 8), :][7:8]
    after = x_ref[0, pl.ds(pl.multiple_of(jnp.minimum(r0 + N_FAST, SEQ - 8), 8), 8), :][0:1]
    before = before * (n1 > 0).astype(F32)
    after = after * (n1 < N_IN - 1).astype(F32)
    row = lax.broadcasted_iota(jnp.int32, cur.shape, 0)
    prev = jnp.where(row == 0, before, pltpu.roll(cur, 1, axis=0))
    nxt = jnp.where(row == N_FAST - 1, after, pltpu.roll(cur, N_FAST - 1, axis=0))
    return prev * w_ref[0:1, :] + cur * w_ref[1:2, :] + nxt * w_ref[2:3, :] + b_ref[...]


def _hyena_kernel(x0_ref, x1_ref, v_ref, w0_ref, w1_ref, wv_ref, b0_ref, b1_ref, bv_ref,
                  hs_ref, hbias_ref, tw_ref, ff_ref, fi_ref, o_ref, u_ref, g0_ref, a_ref, z_ref):
    def conv(n1, carry):
        u_ref[n1] = _short_conv_slab(v_ref, wv_ref, bv_ref, n1) * _short_conv_slab(x1_ref, w1_ref, b1_ref, n1)
        g0_ref[n1] = _short_conv_slab(x0_ref, w0_ref, b0_ref, n1)
        return carry

    lax.fori_loop(0, N_IN, conv, 0)
    _slow_forward_pass(u_ref, a_ref)

    def freq(i, carry):
        ks = (2 * i, 2 * i + 1)
        tws = [tw_ref[k] for k in ks]
        a = [jnp.concatenate(_cmul(a_ref[k], tw), axis=0).astype(BF16) for k, tw in zip(ks, tws)]
        x = jnp.dot(ff_ref[...], jnp.concatenate(a, axis=1), preferred_element_type=F32)
        y = [jnp.concatenate(_cmul(x[:, j * LANES:(j + 1) * LANES], hs_ref[k]), axis=0).astype(BF16)
             for j, k in enumerate(ks)]
        z = jnp.dot(fi_ref[...], jnp.concatenate(y, axis=1), preferred_element_type=F32)
        for j, (k, tw) in enumerate(zip(ks, tws)):
            z_ref[k] = jnp.concatenate(_cmul_conj(z[:, j * LANES:(j + 1) * LANES], tw), axis=0)
        return carry

    lax.fori_loop(0, N_K1_PAD // 2, freq, 0, unroll=True)

    def back(j, carry):
        rows = pl.ds(pl.multiple_of(j * ROW_TILE, ROW_TILE), ROW_TILE)
        rows_im = pl.ds(pl.multiple_of(N_FAST + j * ROW_TILE, ROW_TILE), ROW_TILE)
        y = _slow_inverse([(z_ref[k1, rows, :], z_ref[k1, rows_im, :]) for k1 in range(N_K1)])
        for n1 in range(N_IN):
            u_ref[n1, rows, :] = y[n1] + u_ref[n1, rows, :] * hbias_ref[...]
        return carry

    lax.fori_loop(0, N_FAST // ROW_TILE, back, 0)

    def gate(n1, carry):
        rows = pl.ds(pl.multiple_of(n1 * N_FAST, N_FAST), N_FAST)
        o_ref[0, rows, :] = (u_ref[n1] * g0_ref[n1]).astype(BF16)
        return carry

    lax.fori_loop(0, N_IN, gate, 0)


def hyena(p3, sc_w, sc_b, h_spec, h_bias, tables):
    tw, f_fwd, f_inv = tables
    b = p3.shape[0]
    nblk = D_B // LANES
    xblk = lambda off: pl.BlockSpec((1, SEQ, LANES), lambda c, i: (i, 0, off * nblk + c))
    wblk = lambda off: pl.BlockSpec((3, LANES), lambda c, i: (0, off * nblk + c))
    bblk = lambda off: pl.BlockSpec((1, LANES), lambda c, i: (0, off * nblk + c))
    whole = pl.BlockSpec(memory_space=pltpu.VMEM)
    sc_b2 = sc_b.reshape(1, 3 * D_B)
    spec_blk = (N_K1_PAD, 2 * N_FAST, LANES)
    return pl.pallas_call(
        _hyena_kernel,
        out_shape=jax.ShapeDtypeStruct((b, SEQ, D_B), BF16),
        grid=(nblk, b),
        in_specs=[
            xblk(0), xblk(1), xblk(2),
            wblk(0), wblk(1), wblk(2),
            bblk(0), bblk(1), bblk(2),
            pl.BlockSpec(spec_blk, lambda c, i: (0, 0, c)),
            pl.BlockSpec((1, LANES), lambda c, i: (0, c)),
            whole, whole, whole,
        ],
        out_specs=pl.BlockSpec((1, SEQ, LANES), lambda c, i: (i, 0, c)),
        scratch_shapes=[
            pltpu.VMEM((N_IN, N_FAST, LANES), F32),
            pltpu.VMEM((N_IN, N_FAST, LANES), F32),
            pltpu.VMEM(spec_blk, F32),
            pltpu.VMEM(spec_blk, F32),
        ],
        compiler_params=_cparams(("parallel", "arbitrary")),
        name="hyena",
    )(p3, p3, p3, sc_w, sc_w, sc_w, sc_b2, sc_b2, sc_b2,
      h_spec, h_bias.reshape(1, D_B), tw, f_fwd.astype(BF16), f_inv.astype(BF16))


def kernel(x_prompt, x_sample, norm_mix, norm_mlp, w_in_ab, sc_w, sc_b, q_gain, k_gain, rpb,
           f_w1, f_b1, f_w2, f_b2, f_w3, f_b3, f_wout, f_freq, h_bias, w_out_ab, w_in_c,
           v_gain, w_s, b_s, w_out_c, w_mlp1, w_mlp2):
    nb_p = x_prompt.shape[0]
    x = jnp.concatenate([x_prompt, x_sample], axis=0)
    nb = x.shape[0]
    x = x.reshape(nb * SEQ, D_MODEL)
    tables = tuple(jnp.asarray(a) for a in _dft_tables())

    for i in range(DEPTH):
        j = i // 2
        if i % 2 == 0:
            w_in = w_in_ab[j].astype(BF16)
            qkv = qkv_proj(x, norm_mix[i], w_in, q_gain[j], k_gain[j])
            a = natten(qkv.reshape(nb, SEQ, 3 * D_A), natten_bias_table(rpb[j]))
            pz = norm_matmul(x, norm_mix[i], w_in[:, 3 * D_A:])
            hf, hb = hyena_filter(f_w1[j], f_b1[j], f_w2[j], f_b2[j], f_w3[j], f_b3[j],
                                  f_wout[j], f_freq[j])
            h_spec = filter_spectrum(hf, hb, tables[0], tables[1])
            bo = hyena(pz.reshape(nb, SEQ, 3 * D_B), sc_w[j], sc_b[j], h_spec, h_bias[j], tables)
            wo = w_out_ab[j].astype(BF16)
            x = proj_residual(x, a.reshape(nb * SEQ, D_A), bo.reshape(nb * SEQ, D_B),
                              wo[:D_A], wo[D_A:])
        else:
            zz = norm_matmul(x, norm_mix[i], w_in_c[j].astype(BF16), act="gelu")
            b_lanes = jnp.broadcast_to(b_s[j][:, :, None], (G_C, CHUNK, LANES))
            m = gmlp_gate(zz, v_gain[j], w_s[j].astype(BF16), b_lanes)
            wo = w_out_c[j].astype(BF16)
            half = D_C // 2
            x = proj_residual(x, m, m, wo[:half], wo[half:], a_col=0, b_col=1)
        x = mlp_residual(x, norm_mlp[i], w_mlp1[i].astype(BF16), w_mlp2[i].astype(BF16))

    x = x.reshape(nb, SEQ, D_MODEL)
    return x[:nb_p], x[nb_p:]
```

```python
import functools
import math

import numpy as np
import jax
import jax.numpy as jnp
from jax import lax
from jax.experimental import pallas as pl
from jax.experimental.pallas import tpu as pltpu

F32 = jnp.float32
BF16 = jnp.bfloat16

D_MODEL = 2048
DEPTH = 4
SEQ = 4096
GRID_W = 64
WIN_H = 8
WIN_W = 16
DH = 128
H_A = 8
D_A = H_A * DH
D_B = 1024
EMB = 33
FO = 64
FAST_DECAY = 0.3
SLOW_DECAY = 1.5
DECAY_TARGET = 1e-2
D_C = D_MODEL
G_C = 8
DG_C = D_C // G_C
CHUNK = 128
D_FF = 4 * D_MODEL
EPS = 1e-6

LANES = 128
VMEM_LIMIT_BYTES = 52 * 1024 * 1024

N_FFT = 2 * SEQ
N_FAST = 128
N_SLOW = N_FFT // N_FAST
N_IN = SEQ // N_FAST
N_K1 = N_SLOW // 2 + 1


def _cparams(sem):
    return pltpu.CompilerParams(dimension_semantics=sem, vmem_limit_bytes=VMEM_LIMIT_BYTES)


def _rms_scale(x, g):
    return x * lax.rsqrt(jnp.mean(x * x, axis=-1, keepdims=True) + EPS) * g


def _gelu_tanh(x):
    return 0.5 * x * (1.0 + jnp.tanh(math.sqrt(2.0 / math.pi) * (x + 0.044715 * (x * x * x))))


def _split_bf16(x):
    hi = x.astype(BF16)
    lo = (x - hi.astype(F32)).astype(BF16)
    return hi, lo


def _dot3(a, b):
    ah, al = _split_bf16(a)
    bh, bl = _split_bf16(b)
    d = functools.partial(jnp.dot, preferred_element_type=F32)
    return d(ah, bh) + (d(ah, bl) + d(al, bh))


def _norm_matmul_kernel(x_ref, g_ref, w_ref, o_ref, h_ref, *, act):
    @pl.when(pl.program_id(1) == 0)
    def _():
        h_ref[...] = _rms_scale(x_ref[...], g_ref[...]).astype(BF16)

    acc = jnp.dot(h_ref[...], w_ref[...], preferred_element_type=F32)
    if act == "gelu":
        acc = _gelu_tanh(acc)
    o_ref[...] = acc.astype(o_ref.dtype)


def norm_matmul(x, g, w, layer, *, n, col_off=0, act=None, tm=1024, tn=1024, out_dtype=F32):
    t, d = x.shape
    return pl.pallas_call(
        functools.partial(_norm_matmul_kernel, act=act),
        out_shape=jax.ShapeDtypeStruct((t, n), out_dtype),
        grid=(t // tm, n // tn),
        in_specs=[
            pl.BlockSpec((tm, d), lambda i, j: (i, 0)),
            pl.BlockSpec((1, d), lambda i, j: (0, 0)),
            pl.BlockSpec((None, d, tn), lambda i, j: (layer, 0, col_off + j)),
        ],
        out_specs=pl.BlockSpec((tm, tn), lambda i, j: (i, j)),
        scratch_shapes=[pltpu.VMEM((tm, d), BF16)],
        compiler_params=_cparams(("parallel", "arbitrary")),
        name="norm_matmul",
    )(x, g.reshape(1, d), w)


def _mlp_kernel(x_ref, g_ref, w1_ref, w2_ref, o_ref, h_ref):
    @pl.when(pl.program_id(1) == 0)
    def _():
        x = x_ref[...]
        h_ref[...] = _rms_scale(x, g_ref[...]).astype(BF16)
        o_ref[...] = x

    a = jnp.dot(h_ref[...], w1_ref[...], preferred_element_type=F32)
    a = jnp.square(jnp.maximum(a, 0.0)).astype(BF16)
    o_ref[...] += jnp.dot(a, w2_ref[...], preferred_element_type=F32)


def mlp_residual(x, g, w1, w2, layer, *, tm=512, tf=1024):
    t, d = x.shape
    f = w1.shape[2]
    return pl.pallas_call(
        _mlp_kernel,
        out_shape=jax.ShapeDtypeStruct((t, d), F32),
        grid=(t // tm, f // tf),
        in_specs=[
            pl.BlockSpec((tm, d), lambda i, k: (i, 0)),
            pl.BlockSpec((1, d), lambda i, k: (0, 0)),
            pl.BlockSpec((None, d, tf), lambda i, k: (layer, 0, k)),
            pl.BlockSpec((None, tf, d), lambda i, k: (layer, k, 0)),
        ],
        out_specs=pl.BlockSpec((tm, d), lambda i, k: (i, 0)),
        scratch_shapes=[pltpu.VMEM((tm, d), BF16)],
        compiler_params=_cparams(("parallel", "arbitrary")),
        name="mlp_residual",
    )(x, g.reshape(1, d), w1, w2)


def _proj_residual_kernel(x_ref, a_ref, b_ref, wa_ref, wb_ref, o_ref):
    acc = jnp.dot(a_ref[...], wa_ref[...], preferred_element_type=F32)
    acc += jnp.dot(b_ref[...], wb_ref[...], preferred_element_type=F32)
    o_ref[...] = x_ref[...] + acc


def proj_residual(x, a, b, w, layer, *, a_col=0, b_col=0, tm=1024, tn=1024):
    t, n = x.shape
    k = w.shape[1] // 2
    return pl.pallas_call(
        _proj_residual_kernel,
        out_shape=jax.ShapeDtypeStruct((t, n), F32),
        grid=(t // tm, n // tn),
        in_specs=[
            pl.BlockSpec((tm, tn), lambda i, j: (i, j)),
            pl.BlockSpec((tm, k), lambda i, j: (i, a_col)),
            pl.BlockSpec((tm, k), lambda i, j: (i, b_col)),
            pl.BlockSpec((None, k, tn), lambda i, j: (layer, 0, j)),
            pl.BlockSpec((None, k, tn), lambda i, j: (layer, 1, j)),
        ],
        out_specs=pl.BlockSpec((tm, tn), lambda i, j: (i, j)),
        compiler_params=_cparams(("parallel", "arbitrary")),
        name="proj_residual",
    )(x, a, b, w, w)


def _gate_kernel(zz_ref, vg_ref, ws_ref, bs_ref, o_ref, vn_ref, *, tm):
    for c in range(tm // CHUNK):
        rows = slice(c * CHUNK, (c + 1) * CHUNK)
        vn_ref[rows, :] = _rms_scale(zz_ref[rows, D_C:].astype(F32), vg_ref[...]).astype(BF16)
        for g in range(G_C):
            cols = slice(g * DG_C, (g + 1) * DG_C)
            s = jnp.dot(ws_ref[g], vn_ref[rows, cols], preferred_element_type=F32)
            bias = bs_ref[g]
            s = s + jnp.concatenate([bias] * (DG_C // LANES), axis=1)
            o_ref[rows, cols] = (zz_ref[rows, cols].astype(F32) * s).astype(BF16)


def gmlp_gate(zz, v_gain, w_s, b_s_lanes, *, tm=512):
    t = zz.shape[0]
    return pl.pallas_call(
        functools.partial(_gate_kernel, tm=tm),
        out_shape=jax.ShapeDtypeStruct((t, D_C), BF16),
        grid=(t // tm,),
        in_specs=[
            pl.BlockSpec((tm, 2 * D_C), lambda i: (i, 0)),
            pl.BlockSpec((1, D_C), lambda i: (0, 0)),
            pl.BlockSpec((G_C, CHUNK, CHUNK), lambda i: (0, 0, 0)),
            pl.BlockSpec((G_C, CHUNK, LANES), lambda i: (0, 0, 0)),
        ],
        out_specs=pl.BlockSpec((tm, D_C), lambda i: (i, 0)),
        scratch_shapes=[pltpu.VMEM((tm, D_C), BF16)],
        compiler_params=_cparams(("parallel",)),
        name="gmlp_gate",
    )(zz, v_gain.reshape(1, D_C), w_s, b_s_lanes)


def _head_norm_store(acc_ref, gain, o_ref):
    for h in range(H_A):
        cols = slice(h * DH, (h + 1) * DH)
        o_ref[:, cols] = _rms_scale(acc_ref[:, cols], gain).astype(BF16)


def _qkv_kernel(x_ref, g_ref, w_ref, qg_ref, kg_ref, o_ref, h_ref, acc_ref):
    j = pl.program_id(1)

    @pl.when(j == 0)
    def _():
        h_ref[...] = _rms_scale(x_ref[...], g_ref[...]).astype(BF16)

    acc_ref[...] = jnp.dot(h_ref[...], w_ref[...], preferred_element_type=F32)

    @pl.when(j == 0)
    def _():
        _head_norm_store(acc_ref, qg_ref[...] * (DH ** -0.5), o_ref)

    @pl.when(j == 1)
    def _():
        _head_norm_store(acc_ref, kg_ref[...], o_ref)

    @pl.when(j == 2)
    def _():
        o_ref[...] = acc_ref[...].astype(BF16)


def qkv_proj(x, g, w, layer, q_gain, k_gain, *, tm=1024):
    t, d = x.shape
    return pl.pallas_call(
        _qkv_kernel,
        out_shape=jax.ShapeDtypeStruct((t, 3 * D_A), BF16),
        grid=(t // tm, 3),
        in_specs=[
            pl.BlockSpec((tm, d), lambda i, j: (i, 0)),
            pl.BlockSpec((1, d), lambda i, j: (0, 0)),
            pl.BlockSpec((None, d, D_A), lambda i, j: (layer, 0, j)),
            pl.BlockSpec((1, DH), lambda i, j: (0, 0)),
            pl.BlockSpec((1, DH), lambda i, j: (0, 0)),
        ],
        out_specs=pl.BlockSpec((tm, D_A), lambda i, j: (i, j)),
        scratch_shapes=[pltpu.VMEM((tm, d), BF16), pltpu.VMEM((tm, D_A), F32)],
        compiler_params=_cparams(("parallel", "arbitrary")),
        name="qkv_proj",
    )(x, g.reshape(1, d), w, q_gain.reshape(1, DH), k_gain.reshape(1, DH))


N_ROWS = SEQ // GRID_W
GROUP_ROWS = 4
N_GROUPS = N_ROWS // GROUP_ROWS
KEY_ROWS = 12
NQ = GROUP_ROWS * GRID_W
NK = KEY_ROWS * GRID_W
SOFTMAX_ROWS = 64


def _natten_group(g, slot, q_ref, k_ref, v_ref, bias_ref, o_ref, s_ref, p_ref):
    ws = jnp.clip(GROUP_ROWS * g - WIN_H // 2, 0, N_ROWS - KEY_ROWS)
    kind = jnp.where(g == 0, 0, jnp.where(g == N_GROUPS - 1, 2, 1))
    qrows = pl.ds(pl.multiple_of(g * NQ, NQ), NQ)
    krows = pl.ds(pl.multiple_of(ws * GRID_W, GRID_W), NK)
    s_ref[slot] = lax.dot_general(q_ref[0, qrows, :], k_ref[0, krows, :], (((1,), (1,)), ((), ())),
                                  preferred_element_type=F32)
    for c in range(NQ // SOFTMAX_ROWS):
        rows = slice(c * SOFTMAX_ROWS, (c + 1) * SOFTMAX_ROWS)
        s = s_ref[slot, rows, :] + bias_ref[0, kind, rows, :]
        p = jnp.exp(s - jnp.max(s, axis=-1, keepdims=True))
        p = p * (1.0 / jnp.sum(p, axis=-1, keepdims=True))
        p_ref[slot, rows, :] = p.astype(BF16)
    o = jnp.dot(p_ref[slot], v_ref[0, krows, :], preferred_element_type=F32)
    o_ref[0, qrows, :] = o.astype(BF16)


def _natten_kernel(q_ref, k_ref, v_ref, bias_ref, o_ref, s_ref, p_ref):
    def body(i, carry):
        for slot in range(2):
            _natten_group(2 * i + slot, slot, q_ref, k_ref, v_ref, bias_ref, o_ref, s_ref, p_ref)
        return carry

    lax.fori_loop(0, N_GROUPS // 2, body, 0)


def natten(qkv3, bias_tab):
    b = qkv3.shape[0]
    blk = (1, SEQ, DH)
    return pl.pallas_call(
        _natten_kernel,
        out_shape=jax.ShapeDtypeStruct((b, SEQ, D_A), BF16),
        grid=(H_A, b),
        in_specs=[
            pl.BlockSpec(blk, lambda h, i: (i, 0, h)),
            pl.BlockSpec(blk, lambda h, i: (i, 0, H_A + h)),
            pl.BlockSpec(blk, lambda h, i: (i, 0, 2 * H_A + h)),
            pl.BlockSpec((1, 3, NQ, NK), lambda h, i: (h, 0, 0, 0)),
        ],
        out_specs=pl.BlockSpec(blk, lambda h, i: (i, 0, h)),
        scratch_shapes=[pltpu.VMEM((2, NQ, NK), F32), pltpu.VMEM((2, NQ, NK), BF16)],
        compiler_params=_cparams(("parallel", "arbitrary")),
        name="natten",
    )(qkv3, qkv3, qkv3, bias_tab)


def natten_bias_table(rpb):
    col = np.arange(GRID_W)
    cs = np.clip(col - WIN_W // 2, 0, GRID_W - WIN_W)
    qc, kc = col[:, None], col[None, :]
    col_ok = (kc >= cs[:, None]) & (kc < cs[:, None] + WIN_W)
    dc = np.clip(kc - qc, -(WIN_W - 1), WIN_W - 1) + WIN_W - 1
    bias_c = rpb.astype(F32)[:, :, dc]
    a = np.arange(GROUP_ROWS)
    q_off = np.stack([a, a + WIN_H // 2, a + KEY_ROWS - GROUP_ROWS])
    w_off = np.stack([0 * a, a, 0 * a + KEY_ROWS - WIN_H])
    i = np.arange(KEY_ROWS)
    row_ok = (i >= w_off[..., None]) & (i < w_off[..., None] + WIN_H)
    dr = np.clip(i - q_off[..., None] + WIN_H - 1, 0, 2 * WIN_H - 2)
    tab = bias_c[:, dr]
    ok = row_ok[None, :, :, :, None, None] & col_ok[None, None, None, None]
    tab = jnp.where(ok, tab, -jnp.inf)
    return tab.transpose(0, 1, 2, 4, 3, 5).reshape(H_A, 3, NQ, NK)


FILT_ROWS = 512


def _filter_kernel(z_ref, t_ref, w1_ref, b1_ref, w2_ref, b2_ref, w3_ref, b3_ref, fr_ref,
                   wf_ref, wb_ref, dl_ref, hf_ref, hb_ref, h3_ref):
    @pl.when(pl.program_id(0) == 0)
    def _():
        def mlp(i, carry):
            rows = pl.ds(pl.multiple_of(i * FILT_ROWS, FILT_ROWS), FILT_ROWS)
            fr = fr_ref[...]
            h = jnp.sin(fr * (_dot3(z_ref[rows, :], w1_ref[...]) + b1_ref[...]))
            h = jnp.sin(fr * (_dot3(h, w2_ref[...]) + b2_ref[...]))
            h3_ref[rows, :] = jnp.sin(fr * (_dot3(h, w3_ref[...]) + b3_ref[...]))
            return carry

        lax.fori_loop(0, SEQ // FILT_ROWS, mlp, 0)

    def taps(i, carry):
        sf, sb = carry
        rows = pl.ds(pl.multiple_of(i * FILT_ROWS, FILT_ROWS), FILT_ROWS)
        h3 = h3_ref[rows, :]
        decay = jnp.exp(-t_ref[rows, :] * dl_ref[...])
        hf = _dot3(h3, wf_ref[...]) * decay
        hb = _dot3(h3, wb_ref[...]) * decay
        t_idx = lax.broadcasted_iota(jnp.int32, hb.shape, 0) + i * FILT_ROWS
        hb = jnp.where(t_idx == 0, 0.0, hb)
        hf_ref[rows, :] = hf
        hb_ref[rows, :] = hb
        return (sf + jnp.sum(jnp.abs(hf), axis=0, keepdims=True),
                sb + jnp.sum(jnp.abs(hb), axis=0, keepdims=True))

    zero = jnp.zeros((1, LANES), F32)
    sf, sb = lax.fori_loop(0, SEQ // FILT_ROWS, taps, (zero, zero))
    inv = 1.0 / (sf + sb)

    def scale(i, carry):
        rows = pl.ds(pl.multiple_of(i * FILT_ROWS, FILT_ROWS), FILT_ROWS)
        hf_ref[rows, :] = hf_ref[rows, :] * inv
        hb_ref[rows, :] = hb_ref[rows, :] * inv
        return carry

    lax.fori_loop(0, SEQ // FILT_ROWS, scale, 0)


def _filter_constants():
    t = np.linspace(0.0, 1.0, SEQ, dtype=np.float32).astype(np.float64)[:, None]
    bands = (EMB - 1) // 2
    w = 2.0 * math.pi * np.arange(SEQ, dtype=np.float64)[:, None] / SEQ
    f = np.linspace(1e-4, bands - 1, bands, dtype=np.float32).astype(np.float64)[None, :]
    z = np.concatenate([t, np.cos(f * w), -np.sin(f * w)], axis=-1)
    z_pad = np.zeros((SEQ, LANES), np.float32)
    z_pad[:, :EMB] = z
    t_lanes = np.broadcast_to(t.astype(np.float32), (SEQ, LANES)).copy()
    deltas = np.abs(np.linspace(math.log(DECAY_TARGET) / FAST_DECAY,
                                math.log(DECAY_TARGET) / SLOW_DECAY, D_B, dtype=np.float32))
    return z_pad, t_lanes, deltas.reshape(1, D_B)


def hyena_filter(f_w1, f_b1, f_w2, f_b2, f_w3, f_b3, f_wout, f_freq):
    z_pad, t_lanes, deltas = _filter_constants()
    w1_pad = jnp.zeros((LANES, FO), F32).at[:EMB].set(f_w1)
    nblk = D_B // LANES
    const = lambda shape: pl.BlockSpec(shape, lambda c: (0,) * len(shape))
    return pl.pallas_call(
        _filter_kernel,
        out_shape=[jax.ShapeDtypeStruct((SEQ, D_B), F32)] * 2,
        grid=(nblk,),
        in_specs=[
            const((SEQ, LANES)), const((SEQ, LANES)),
            const((LANES, FO)), const((1, FO)),
            const((FO, FO)), const((1, FO)),
            const((FO, FO)), const((1, FO)),
            const((1, FO)),
            pl.BlockSpec((FO, LANES), lambda c: (0, c)),
            pl.BlockSpec((FO, LANES), lambda c: (0, nblk + c)),
            pl.BlockSpec((1, LANES), lambda c: (0, c)),
        ],
        out_specs=[pl.BlockSpec((SEQ, LANES), lambda c: (0, c))] * 2,
        scratch_shapes=[pltpu.VMEM((SEQ, FO), F32)],
        compiler_params=_cparams(("arbitrary",)),
        name="hyena_filter",
    )(jnp.asarray(z_pad), jnp.asarray(t_lanes), w1_pad, f_b1.reshape(1, FO), f_w2,
      f_b2.reshape(1, FO), f_w3, f_b3.reshape(1, FO), f_freq.reshape(1, FO), f_wout, f_wout,
      jnp.asarray(deltas))


ROW_TILE = 8
N_K1_PAD = N_K1 + 1


@functools.lru_cache(maxsize=None)
def _dft_tables():
    k1 = np.arange(N_K1_PAD)
    n2 = np.arange(N_FAST)
    ang = 2.0 * math.pi * np.outer(k1, n2) / N_FFT
    tw = np.concatenate([np.cos(ang), -np.sin(ang)], axis=1)
    tw = np.broadcast_to(tw[:, :, None], (N_K1_PAD, 2 * N_FAST, LANES))
    ang2 = 2.0 * math.pi * np.outer(n2, n2) / N_FAST
    fr, fi = np.cos(ang2), -np.sin(ang2)
    f_fwd = np.block([[fr, -fi], [fi, fr]])
    f_inv = np.block([[fr, fi], [-fi, fr]])
    f32 = lambda a: np.ascontiguousarray(a, dtype=np.float32)
    return f32(tw), f32(f_fwd), f32(f_inv)


def _cmul_const(x, w):
    xr, xi = x
    c, s = w.real, w.imag
    eps = 1e-12
    if abs(s) < eps and abs(c - 1.0) < eps:
        return xr, xi
    if abs(c) < eps and abs(s - 1.0) < eps:
        return ("neg", xi), xr
    if abs(c) < eps and abs(s + 1.0) < eps:
        return xi, ("neg", xr)
    return xr * c - xi * s, xr * s + xi * c


def _is_neg(v):
    return isinstance(v, tuple) and v[0] == "neg"


def _plain(v):
    return -v[1] if _is_neg(v) else v


def _add(a, b):
    return a - b[1] if _is_neg(b) else a + b


def _sub(a, b):
    return a + b[1] if _is_neg(b) else a - b


def _fft_dit(xs, sign):
    n = len(xs)
    if n == 1:
        return xs
    ev = _fft_dit(xs[0::2], sign)
    od = _fft_dit(xs[1::2], sign)
    out = [None] * n
    for k in range(n // 2):
        w = complex(math.cos(2 * math.pi * k / n), sign * math.sin(2 * math.pi * k / n))
        tr, ti = _cmul_const(od[k], w)
        er, ei = ev[k]
        out[k] = (_add(er, tr), _add(ei, ti))
        out[k + n // 2] = (_sub(er, tr), _sub(ei, ti))
    return out


def _slow_forward(x):
    half, quarter = N_SLOW // 2, N_SLOW // 4
    z = [(x[2 * m], x[2 * m + 1]) for m in range(quarter)]
    zt = [z[0]]
    for m in range(1, quarter):
        tr, ti = _cmul_const(z[m], complex(math.cos(2 * math.pi * m / half), -math.sin(2 * math.pi * m / half)))
        zt.append((_plain(tr), _plain(ti)))
    ze = _fft_dit(z, -1)
    zo = _fft_dit(zt, -1)
    zc = [None] * half
    for j in range(quarter):
        zc[2 * j] = ze[j]
        zc[2 * j + 1] = zo[j]
    out = [None] * (half + 1)
    ar, ai = zc[0]
    out[0] = ((ar + ai) * 2.0, None)
    out[half] = ((ar - ai) * 2.0, None)
    ar, ai = zc[quarter]
    out[quarter] = (ar * 2.0, ai * -2.0)
    for k in range(1, quarter):
        ar, ai = zc[k]
        br, bi = zc[half - k]
        c, s = math.cos(2 * math.pi * k / N_SLOW), math.sin(2 * math.pi * k / N_SLOW)
        p, q, r, t = ar + br, ai - bi, ai + bi, ar - br
        u = r * c - t * s
        vn = t * (-c) - r * s
        out[k] = (p + u, q + vn)
        out[half - k] = (p - u, vn - q)
    return out


def _slow_inverse(xs):
    half, quarter = N_SLOW // 2, N_SLOW // 4
    zc = [None] * half
    ar, br = xs[0][0], xs[half][0]
    zc[0] = (ar + br, ar - br)
    ar, ai = xs[quarter]
    zc[quarter] = (ar * 2.0, ai * -2.0)
    for k in range(1, quarter):
        ar, ai = xs[k]
        br, bi = xs[half - k]
        c, s = math.cos(2 * math.pi * k / N_SLOW), math.sin(2 * math.pi * k / N_SLOW)
        p, q, t, r = ar + br, ai - bi, ar - br, ai + bi
        o_r = t * c - r * s
        o_i = t * s + r * c
        zc[k] = (p - o_i, q + o_r)
        zc[half - k] = (p + o_i, o_r - q)
    ze = _fft_dit(zc[0::2], +1)
    zo = _fft_dit(zc[1::2], +1)
    out = [None] * N_IN
    for m in range(quarter):
        tr, ti = _cmul_const(zo[m], complex(math.cos(2 * math.pi * m / half), math.sin(2 * math.pi * m / half)))
        out[2 * m] = _add(ze[m][0], tr)
        out[2 * m + 1] = _add(ze[m][1], ti)
    return out


def _slow_forward_pass(src_ref, a_ref):
    lanes = src_ref.shape[-1]
    zeros = jnp.zeros((ROW_TILE, lanes), F32)

    def body(j, carry):
        rows = pl.ds(pl.multiple_of(j * ROW_TILE, ROW_TILE), ROW_TILE)
        rows_im = pl.ds(pl.multiple_of(N_FAST + j * ROW_TILE, ROW_TILE), ROW_TILE)
        out = _slow_forward([src_ref[n1, rows, :] for n1 in range(N_IN)])
        for k1, (re, im) in enumerate(out):
            a_ref[k1, rows, :] = re
            a_ref[k1, rows_im, :] = zeros if im is None else im
        a_ref[N_K1, rows, :] = zeros
        a_ref[N_K1, rows_im, :] = zeros
        return carry

    lax.fori_loop(0, N_FAST // ROW_TILE, body, 0)


def _cmul(a, w):
    ar, ai, wr, wi = a[:N_FAST], a[N_FAST:], w[:N_FAST], w[N_FAST:]
    return ar * wr - ai * wi, ar * wi + ai * wr


def _cmul_conj(a, w):
    ar, ai, wr, wi = a[:N_FAST], a[N_FAST:], w[:N_FAST], w[N_FAST:]
    return ar * wr + ai * wi, ai * wr - ar * wi


def _spectrum_kernel(hf_ref, hb_ref, tw_ref, f_ref, o_ref, af_ref, ab_ref):
    _slow_forward_pass(hf_ref, af_ref)
    _slow_forward_pass(hb_ref, ab_ref)

    def body(k1, carry):
        tw = tw_ref[k1]
        f = f_ref[...]
        xf = _dot3(f, jnp.concatenate(_cmul(af_ref[k1], tw), axis=0))
        xb = _dot3(f, jnp.concatenate(_cmul(ab_ref[k1], tw), axis=0))
        re = xf[:N_FAST] + xb[:N_FAST]
        im = xf[N_FAST:] - xb[N_FAST:]
        o_ref[k1] = jnp.concatenate([re, im], axis=0) * (1.0 / (4 * N_FFT))
        return carry

    lax.fori_loop(0, N_K1_PAD, body, 0)


def filter_spectrum(hf, hb, tw, f_fwd):
    nblk = D_B // LANES
    slabs = lambda a: a.reshape(N_IN, N_FAST, D_B)
    whole = pl.BlockSpec(memory_space=pltpu.VMEM)
    spec_blk = (N_K1_PAD, 2 * N_FAST, LANES)
    return pl.pallas_call(
        _spectrum_kernel,
        out_shape=jax.ShapeDtypeStruct((N_K1_PAD, 2 * N_FAST, D_B), F32),
        grid=(nblk,),
        in_specs=[
            pl.BlockSpec((N_IN, N_FAST, LANES), lambda c: (0, 0, c)),
            pl.BlockSpec((N_IN, N_FAST, LANES), lambda c: (0, 0, c)),
            whole, whole,
        ],
        out_specs=pl.BlockSpec(spec_blk, lambda c: (0, 0, c)),
        scratch_shapes=[pltpu.VMEM(spec_blk, F32)] * 2,
        compiler_params=_cparams(("arbitrary",)),
        name="filter_spectrum",
    )(slabs(hf), slabs(hb), tw, f_fwd)


def _short_conv_slab(x_ref, w_ref, b_ref, n1):
    r0 = pl.multiple_of(n1 * N_FAST, N_FAST)
    edge = 16
    cur = x_ref[0, pl.ds(r0, N_FAST), :].astype(F32)
    before = x_ref[0, pl.ds(pl.multiple_of(jnp.maximum(r0 - edge, 0), edge), edge), :]
    after = x_ref[0, pl.ds(pl.multiple_of(jnp.minimum(r0 + N_FAST, SEQ - edge), edge), edge), :]
    before = jnp.where(n1 > 0, before.astype(F32)[edge - 1:edge], 0.0)
    after = jnp.where(n1 < N_IN - 1, after.astype(F32)[0:1], 0.0)
    row = lax.broadcasted_iota(jnp.int32, cur.shape, 0)
    prev = jnp.where(row == 0, before, pltpu.roll(cur, 1, axis=0))
    nxt = jnp.where(row == N_FAST - 1, after, pltpu.roll(cur, N_FAST - 1, axis=0))
    return prev * w_ref[0:1, :] + cur * w_ref[1:2, :] + nxt * w_ref[2:3, :] + b_ref[...]


def _hyena_kernel(x0_ref, x1_ref, v_ref, w0_ref, w1_ref, wv_ref, b0_ref, b1_ref, bv_ref,
                  hs_ref, hbias_ref, tw_ref, ff_ref, fi_ref, o_ref, u_ref, g0_ref, a_ref, z_ref):
    def conv(n1, carry):
        u_ref[n1] = _short_conv_slab(v_ref, wv_ref, bv_ref, n1) * _short_conv_slab(x1_ref, w1_ref, b1_ref, n1)
        g0_ref[n1] = _short_conv_slab(x0_ref, w0_ref, b0_ref, n1)
        return carry

    lax.fori_loop(0, N_IN, conv, 0)
    _slow_forward_pass(u_ref, a_ref)

    def freq(i, carry):
        ks = (2 * i, 2 * i + 1)
        tws = [tw_ref[k] for k in ks]
        a = [jnp.concatenate(_cmul(a_ref[k], tw), axis=0).astype(BF16) for k, tw in zip(ks, tws)]
        x = jnp.dot(ff_ref[...], jnp.concatenate(a, axis=1), preferred_element_type=F32)
        y = [jnp.concatenate(_cmul(x[:, j * LANES:(j + 1) * LANES], hs_ref[k]), axis=0).astype(BF16)
             for j, k in enumerate(ks)]
        z = jnp.dot(fi_ref[...], jnp.concatenate(y, axis=1), preferred_element_type=F32)
        for j, (k, tw) in enumerate(zip(ks, tws)):
            z_ref[k] = jnp.concatenate(_cmul_conj(z[:, j * LANES:(j + 1) * LANES], tw), axis=0)
        return carry

    lax.fori_loop(0, N_K1_PAD // 2, freq, 0, unroll=True)

    def back(j, carry):
        rows = pl.ds(pl.multiple_of(j * ROW_TILE, ROW_TILE), ROW_TILE)
        rows_im = pl.ds(pl.multiple_of(N_FAST + j * ROW_TILE, ROW_TILE), ROW_TILE)
        y = _slow_inverse([(z_ref[k1, rows, :], z_ref[k1, rows_im, :]) for k1 in range(N_K1)])
        for n1 in range(N_IN):
            u_ref[n1, rows, :] = y[n1] + u_ref[n1, rows, :] * hbias_ref[...]
        return carry

    lax.fori_loop(0, N_FAST // ROW_TILE, back, 0)

    def gate(n1, carry):
        rows = pl.ds(pl.multiple_of(n1 * N_FAST, N_FAST), N_FAST)
        o_ref[0, rows, :] = (u_ref[n1] * g0_ref[n1]).astype(BF16)
        return carry

    lax.fori_loop(0, N_IN, gate, 0)


def hyena(p3, sc_w, sc_b, h_spec, h_bias, tables):
    tw, f_fwd, f_inv = tables
    b = p3.shape[0]
    nblk = D_B // LANES
    xblk = lambda off: pl.BlockSpec((1, SEQ, LANES), lambda c, i: (i, 0, off * nblk + c))
    wblk = lambda off: pl.BlockSpec((3, LANES), lambda c, i: (0, off * nblk + c))
    bblk = lambda off: pl.BlockSpec((1, LANES), lambda c, i: (0, off * nblk + c))
    whole = pl.BlockSpec(memory_space=pltpu.VMEM)
    sc_b2 = sc_b.reshape(1, 3 * D_B)
    spec_blk = (N_K1_PAD, 2 * N_FAST, LANES)
    return pl.pallas_call(
        _hyena_kernel,
        out_shape=jax.ShapeDtypeStruct((b, SEQ, D_B), BF16),
        grid=(nblk, b),
        in_specs=[
            xblk(0), xblk(1), xblk(2),
            wblk(0), wblk(1), wblk(2),
            bblk(0), bblk(1), bblk(2),
            pl.BlockSpec(spec_blk, lambda c, i: (0, 0, c)),
            pl.BlockSpec((1, LANES), lambda c, i: (0, c)),
            whole, whole, whole,
        ],
        out_specs=pl.BlockSpec((1, SEQ, LANES), lambda c, i: (i, 0, c)),
        scratch_shapes=[
            pltpu.VMEM((N_IN, N_FAST, LANES), F32),
            pltpu.VMEM((N_IN, N_FAST, LANES), F32),
            pltpu.VMEM(spec_blk, F32),
            pltpu.VMEM(spec_blk, F32),
        ],
        compiler_params=_cparams(("parallel", "arbitrary")),
        name="hyena",
    )(p3, p3, p3, sc_w, sc_w, sc_w, sc_b2, sc_b2, sc_b2,
      h_spec, h_bias.reshape(1, D_B), tw, f_fwd.astype(BF16), f_inv.astype(BF16))


def kernel(x_prompt, x_sample, norm_mix, norm_mlp, w_in_ab, sc_w, sc_b, q_gain, k_gain, rpb,
           f_w1, f_b1, f_w2, f_b2, f_w3, f_b3, f_wout, f_freq, h_bias, w_out_ab, w_in_c,
           v_gain, w_s, b_s, w_out_c, w_mlp1, w_mlp2):
    nb_p = x_prompt.shape[0]
    x = jnp.concatenate([x_prompt, x_sample], axis=0)
    nb = x.shape[0]
    x = x.reshape(nb * SEQ, D_MODEL)
    tables = tuple(jnp.asarray(a) for a in _dft_tables())

    w_in_ab, w_out_ab, w_in_c, w_out_c, w_mlp1, w_mlp2, w_s = (
        w.astype(BF16) for w in (w_in_ab, w_out_ab, w_in_c, w_out_c, w_mlp1, w_mlp2, w_s))

    for i in range(DEPTH):
        j = i // 2
        if i % 2 == 0:
            qkv = qkv_proj(x, norm_mix[i], w_in_ab, j, q_gain[j], k_gain[j])
            a = natten(qkv.reshape(nb, SEQ, 3 * D_A), natten_bias_table(rpb[j]))
            pz = norm_matmul(x, norm_mix[i], w_in_ab, j, n=3 * D_B, col_off=3, out_dtype=BF16)
            hf, hb = hyena_filter(f_w1[j], f_b1[j], f_w2[j], f_b2[j], f_w3[j], f_b3[j],
                                  f_wout[j], f_freq[j])
            h_spec = filter_spectrum(hf, hb, tables[0], tables[1])
            bo = hyena(pz.reshape(nb, SEQ, 3 * D_B), sc_w[j], sc_b[j], h_spec, h_bias[j], tables)
            x = proj_residual(x, a.reshape(nb * SEQ, D_A), bo.reshape(nb * SEQ, D_B), w_out_ab, j)
        else:
            zz = norm_matmul(x, norm_mix[i], w_in_c, j, n=2 * D_C, act="gelu", out_dtype=BF16)
            b_lanes = jnp.broadcast_to(b_s[j][:, :, None], (G_C, CHUNK, LANES))
            m = gmlp_gate(zz, v_gain[j], w_s[j], b_lanes)
            x = proj_residual(x, m, m, w_out_c, j, a_col=0, b_col=1)
        x = mlp_residual(x, norm_mlp[i], w_mlp1, w_mlp2, i)

    x = x.reshape(nb, SEQ, D_MODEL)
    return x[:nb_p], x[nb_p:]
```

```python
import functools
import math

import numpy as np
import jax
import jax.numpy as jnp
from jax import lax
from jax.experimental import pallas as pl
from jax.experimental.pallas import tpu as pltpu

F32 = jnp.float32
BF16 = jnp.bfloat16

D_MODEL = 2048
DEPTH = 4
SEQ = 4096
GRID_W = 64
WIN_H = 8
WIN_W = 16
DH = 128
H_A = 8
D_A = H_A * DH
D_B = 1024
EMB = 33
FO = 64
FAST_DECAY = 0.3
SLOW_DECAY = 1.5
DECAY_TARGET = 1e-2
D_C = D_MODEL
G_C = 8
DG_C = D_C // G_C
CHUNK = 128
D_FF = 4 * D_MODEL
EPS = 1e-6

LANES = 128
VMEM_LIMIT_BYTES = 52 * 1024 * 1024

N_FFT = 2 * SEQ
N_FAST = 128
N_SLOW = N_FFT // N_FAST
N_IN = SEQ // N_FAST
N_K1 = N_SLOW // 2 + 1


def _cparams(sem):
    return pltpu.CompilerParams(dimension_semantics=sem, vmem_limit_bytes=VMEM_LIMIT_BYTES)


def _rms_scale(x, g):
    return x * lax.rsqrt(jnp.mean(x * x, axis=-1, keepdims=True) + EPS) * g


def _gelu_tanh(x):
    return 0.5 * x * (1.0 + jnp.tanh(math.sqrt(2.0 / math.pi) * (x + 0.044715 * (x * x * x))))


def _split_bf16(x):
    hi = x.astype(BF16)
    lo = (x - hi.astype(F32)).astype(BF16)
    return hi, lo


def _dot3(a, b):
    ah, al = _split_bf16(a)
    bh, bl = _split_bf16(b)
    d = functools.partial(jnp.dot, preferred_element_type=F32)
    return d(ah, bh) + (d(ah, bl) + d(al, bh))


def _in_proj_kernel(*refs, mode):
    if mode == "gelu":
        x_ref, g_ref, w_ref, o_ref, h_ref = refs
    else:
        x_ref, g_ref, w_ref, qg_ref, kg_ref, o_ref, h_ref = refs
    j = pl.program_id(1)

    @pl.when(j == 0)
    def _():
        h_ref[...] = _rms_scale(x_ref[...], g_ref[...]).astype(BF16)

    acc = jnp.dot(h_ref[...], w_ref[...], preferred_element_type=F32)
    if mode == "gelu":
        o_ref[...] = _gelu_tanh(acc).astype(BF16)
    else:
        gain = jnp.where(j == 0, qg_ref[...] * (DH ** -0.5), kg_ref[...])
        for c in range(0, o_ref.shape[1], DH):
            seg = acc[:, c:c + DH]
            o_ref[:, c:c + DH] = jnp.where(j < 2, _rms_scale(seg, gain), seg).astype(BF16)


def in_proj(x, g, w, layer, *, n, mode, qk_gains=(), tm=1024, tn=1024):
    t, d = x.shape
    return pl.pallas_call(
        functools.partial(_in_proj_kernel, mode=mode),
        out_shape=jax.ShapeDtypeStruct((t, n), BF16),
        grid=(t // tm, n // tn),
        in_specs=[
            pl.BlockSpec((tm, d), lambda i, j: (i, 0)),
            pl.BlockSpec((1, d), lambda i, j: (0, 0)),
            pl.BlockSpec((None, d, tn), lambda i, j: (layer, 0, j)),
        ] + [pl.BlockSpec((1, DH), lambda i, j: (0, 0))] * len(qk_gains),
        out_specs=pl.BlockSpec((tm, tn), lambda i, j: (i, j)),
        scratch_shapes=[pltpu.VMEM((tm, d), BF16)],
        compiler_params=_cparams(("parallel", "arbitrary")),
        name="in_proj",
    )(x, g.reshape(1, d), w, *[gain.reshape(1, DH) for gain in qk_gains])


def _mlp_kernel(x_ref, g_ref, w1_ref, w2_ref, o_ref, h_ref):
    @pl.when(pl.program_id(1) == 0)
    def _():
        x = x_ref[...]
        h_ref[...] = _rms_scale(x, g_ref[...]).astype(BF16)
        o_ref[...] = x

    a = jnp.dot(h_ref[...], w1_ref[...], preferred_element_type=F32)
    a = jnp.square(jnp.maximum(a, 0.0)).astype(BF16)
    o_ref[...] += jnp.dot(a, w2_ref[...], preferred_element_type=F32)


def mlp_residual(x, g, w1, w2, layer, *, tm=512, tf=1024):
    t, d = x.shape
    f = w1.shape[2]
    return pl.pallas_call(
        _mlp_kernel,
        out_shape=jax.ShapeDtypeStruct((t, d), F32),
        grid=(t // tm, f // tf),
        in_specs=[
            pl.BlockSpec((tm, d), lambda i, k: (i, 0)),
            pl.BlockSpec((1, d), lambda i, k: (0, 0)),
            pl.BlockSpec((None, d, tf), lambda i, k: (layer, 0, k)),
            pl.BlockSpec((None, tf, d), lambda i, k: (layer, k, 0)),
        ],
        out_specs=pl.BlockSpec((tm, d), lambda i, k: (i, 0)),
        scratch_shapes=[pltpu.VMEM((tm, d), BF16)],
        compiler_params=_cparams(("parallel", "arbitrary")),
        name="mlp_residual",
    )(x, g.reshape(1, d), w1, w2)


def _proj_residual_kernel(x_ref, a_ref, b_ref, wa_ref, wb_ref, o_ref):
    acc = jnp.dot(a_ref[...], wa_ref[...], preferred_element_type=F32)
    acc += jnp.dot(b_ref[...], wb_ref[...], preferred_element_type=F32)
    o_ref[...] = x_ref[...] + acc


def proj_residual(x, a, b, w, layer, *, a_col=0, b_col=0, tm=1024, tn=1024):
    t, n = x.shape
    k = w.shape[1] // 2
    return pl.pallas_call(
        _proj_residual_kernel,
        out_shape=jax.ShapeDtypeStruct((t, n), F32),
        grid=(t // tm, n // tn),
        in_specs=[
            pl.BlockSpec((tm, tn), lambda i, j: (i, j)),
            pl.BlockSpec((tm, k), lambda i, j: (i, a_col)),
            pl.BlockSpec((tm, k), lambda i, j: (i, b_col)),
            pl.BlockSpec((None, k, tn), lambda i, j: (layer, 0, j)),
            pl.BlockSpec((None, k, tn), lambda i, j: (layer, 1, j)),
        ],
        out_specs=pl.BlockSpec((tm, tn), lambda i, j: (i, j)),
        compiler_params=_cparams(("parallel", "arbitrary")),
        name="proj_residual",
    )(x, a, b, w, w)


def _gate_kernel(zz_ref, vg_ref, ws_ref, bs_ref, o_ref, vn_ref, *, tm):
    for c in range(tm // CHUNK):
        rows = slice(c * CHUNK, (c + 1) * CHUNK)
        vn_ref[rows, :] = _rms_scale(zz_ref[rows, D_C:].astype(F32), vg_ref[...]).astype(BF16)
        for g in range(G_C):
            cols = slice(g * DG_C, (g + 1) * DG_C)
            s = jnp.dot(ws_ref[g], vn_ref[rows, cols], preferred_element_type=F32)
            bias = bs_ref[g]
            s = s + jnp.concatenate([bias] * (DG_C // LANES), axis=1)
            o_ref[rows, cols] = (zz_ref[rows, cols].astype(F32) * s).astype(BF16)


def gmlp_gate(zz, v_gain, w_s, b_s_lanes, *, tm=512):
    t = zz.shape[0]
    return pl.pallas_call(
        functools.partial(_gate_kernel, tm=tm),
        out_shape=jax.ShapeDtypeStruct((t, D_C), BF16),
        grid=(t // tm,),
        in_specs=[
            pl.BlockSpec((tm, 2 * D_C), lambda i: (i, 0)),
            pl.BlockSpec((1, D_C), lambda i: (0, 0)),
            pl.BlockSpec((G_C, CHUNK, CHUNK), lambda i: (0, 0, 0)),
            pl.BlockSpec((G_C, CHUNK, LANES), lambda i: (0, 0, 0)),
        ],
        out_specs=pl.BlockSpec((tm, D_C), lambda i: (i, 0)),
        scratch_shapes=[pltpu.VMEM((tm, D_C), BF16)],
        compiler_params=_cparams(("parallel",)),
        name="gmlp_gate",
    )(zz, v_gain.reshape(1, D_C), w_s, b_s_lanes)


N_ROWS = SEQ // GRID_W
GROUP_ROWS = 4
N_GROUPS = N_ROWS // GROUP_ROWS
KEY_ROWS = 12
NQ = GROUP_ROWS * GRID_W
NK = KEY_ROWS * GRID_W
SOFTMAX_ROWS = 64


def _natten_kernel(q_ref, k_ref, v_ref, bias_ref, o_ref, s_ref, p_ref):
    def q_rows(g):
        return pl.ds(pl.multiple_of(g * NQ, NQ), NQ)

    def k_rows(g):
        ws = jnp.clip(GROUP_ROWS * g - WIN_H // 2, 0, N_ROWS - KEY_ROWS)
        return pl.ds(pl.multiple_of(ws * GRID_W, GRID_W), NK)

    def scores(g, slot):
        s_ref[slot] = lax.dot_general(q_ref[0, q_rows(g), :], k_ref[0, k_rows(g), :],
                                      (((1,), (1,)), ((), ())), preferred_element_type=F32)

    def softmax(g, slot):
        kind = jnp.where(g == 0, 0, jnp.where(g == N_GROUPS - 1, 2, 1))
        for c in range(NQ // SOFTMAX_ROWS):
            rows = slice(c * SOFTMAX_ROWS, (c + 1) * SOFTMAX_ROWS)
            s = s_ref[slot, rows, :] + bias_ref[0, kind, rows, :]
            p = jnp.exp(s - jnp.max(s, axis=-1, keepdims=True))
            p = p * (1.0 / jnp.sum(p, axis=-1, keepdims=True))
            p_ref[slot, rows, :] = p.astype(BF16)

    def values(g, slot):
        o = jnp.dot(p_ref[slot], v_ref[0, k_rows(g), :], preferred_element_type=F32)
        o_ref[0, q_rows(g), :] = o.astype(BF16)

    scores(0, 0)
    scores(1, 1)
    softmax(0, 0)

    def body(i, carry):
        g = 2 * i + 1
        scores(g + 1, 0)
        softmax(g, 1)
        values(g - 1, 0)
        scores(g + 2, 1)
        softmax(g + 1, 0)
        values(g, 1)
        return carry

    lax.fori_loop(0, N_GROUPS // 2 - 1, body, 0)
    last = N_GROUPS - 1
    softmax(last, 1)
    values(last - 1, 0)
    values(last, 1)


def natten(qkv3, bias_tab):
    b = qkv3.shape[0]
    blk = (1, SEQ, DH)
    return pl.pallas_call(
        _natten_kernel,
        out_shape=jax.ShapeDtypeStruct((b, SEQ, D_A), BF16),
        grid=(H_A, b),
        in_specs=[
            pl.BlockSpec(blk, lambda h, i: (i, 0, h)),
            pl.BlockSpec(blk, lambda h, i: (i, 0, H_A + h)),
            pl.BlockSpec(blk, lambda h, i: (i, 0, 2 * H_A + h)),
            pl.BlockSpec((1, 3, NQ, NK), lambda h, i: (h, 0, 0, 0)),
        ],
        out_specs=pl.BlockSpec(blk, lambda h, i: (i, 0, h)),
        scratch_shapes=[pltpu.VMEM((2, NQ, NK), F32), pltpu.VMEM((2, NQ, NK), BF16)],
        compiler_params=_cparams(("parallel", "arbitrary")),
        name="natten",
    )(qkv3, qkv3, qkv3, bias_tab)


def natten_bias_table(rpb):
    col = np.arange(GRID_W)
    cs = np.clip(col - WIN_W // 2, 0, GRID_W - WIN_W)
    qc, kc = col[:, None], col[None, :]
    col_ok = (kc >= cs[:, None]) & (kc < cs[:, None] + WIN_W)
    dc = np.clip(kc - qc, -(WIN_W - 1), WIN_W - 1) + WIN_W - 1
    bias_c = rpb.astype(F32)[:, :, dc]
    a = np.arange(GROUP_ROWS)
    q_off = np.stack([a, a + WIN_H // 2, a + KEY_ROWS - GROUP_ROWS])
    w_off = np.stack([0 * a, a, 0 * a + KEY_ROWS - WIN_H])
    i = np.arange(KEY_ROWS)
    row_ok = (i >= w_off[..., None]) & (i < w_off[..., None] + WIN_H)
    dr = np.clip(i - q_off[..., None] + WIN_H - 1, 0, 2 * WIN_H - 2)
    tab = bias_c[:, dr]
    ok = row_ok[None, :, :, :, None, None] & col_ok[None, None, None, None]
    tab = jnp.where(ok, tab, -jnp.inf)
    return tab.transpose(0, 1, 2, 4, 3, 5).reshape(H_A, 3, NQ, NK)


FILT_ROWS = 512


def _filter_kernel(z_ref, t_ref, w1_ref, b1_ref, w2_ref, b2_ref, w3_ref, b3_ref, fr_ref,
                   wf_ref, wb_ref, dl_ref, hf_ref, hb_ref, h3_ref):
    @pl.when(pl.program_id(0) == 0)
    def _():
        def mlp(i, carry):
            rows = pl.ds(pl.multiple_of(i * FILT_ROWS, FILT_ROWS), FILT_ROWS)
            fr = fr_ref[...]
            h = jnp.sin(fr * (_dot3(z_ref[rows, :], w1_ref[...]) + b1_ref[...]))
            h = jnp.sin(fr * (_dot3(h, w2_ref[...]) + b2_ref[...]))
            h3_ref[rows, :] = jnp.sin(fr * (_dot3(h, w3_ref[...]) + b3_ref[...]))
            return carry

        lax.fori_loop(0, SEQ // FILT_ROWS, mlp, 0)

    def taps(i, carry):
        sf, sb = carry
        rows = pl.ds(pl.multiple_of(i * FILT_ROWS, FILT_ROWS), FILT_ROWS)
        h3 = h3_ref[rows, :]
        decay = jnp.exp(-t_ref[rows, :] * dl_ref[...])
        hf = _dot3(h3, wf_ref[...]) * decay
        hb = _dot3(h3, wb_ref[...]) * decay
        t_idx = lax.broadcasted_iota(jnp.int32, hb.shape, 0) + i * FILT_ROWS
        hb = jnp.where(t_idx == 0, 0.0, hb)
        hf_ref[rows, :] = hf
        hb_ref[rows, :] = hb
        return (sf + jnp.sum(jnp.abs(hf), axis=0, keepdims=True),
                sb + jnp.sum(jnp.abs(hb), axis=0, keepdims=True))

    zero = jnp.zeros((1, LANES), F32)
    sf, sb = lax.fori_loop(0, SEQ // FILT_ROWS, taps, (zero, zero))
    inv = 1.0 / (sf + sb)

    def scale(i, carry):
        rows = pl.ds(pl.multiple_of(i * FILT_ROWS, FILT_ROWS), FILT_ROWS)
        hf_ref[rows, :] = hf_ref[rows, :] * inv
        hb_ref[rows, :] = hb_ref[rows, :] * inv
        return carry

    lax.fori_loop(0, SEQ // FILT_ROWS, scale, 0)


def _filter_constants():
    t = np.linspace(0.0, 1.0, SEQ, dtype=np.float32).astype(np.float64)[:, None]
    bands = (EMB - 1) // 2
    w = 2.0 * math.pi * np.arange(SEQ, dtype=np.float64)[:, None] / SEQ
    f = np.linspace(1e-4, bands - 1, bands, dtype=np.float32).astype(np.float64)[None, :]
    z = np.concatenate([t, np.cos(f * w), -np.sin(f * w)], axis=-1)
    z_pad = np.zeros((SEQ, LANES), np.float32)
    z_pad[:, :EMB] = z
    t_lanes = np.broadcast_to(t.astype(np.float32), (SEQ, LANES)).copy()
    deltas = np.abs(np.linspace(math.log(DECAY_TARGET) / FAST_DECAY,
                                math.log(DECAY_TARGET) / SLOW_DECAY, D_B, dtype=np.float32))
    return z_pad, t_lanes, deltas.reshape(1, D_B)


def hyena_filter(f_w1, f_b1, f_w2, f_b2, f_w3, f_b3, f_wout, f_freq):
    z_pad, t_lanes, deltas = _filter_constants()
    w1_pad = jnp.zeros((LANES, FO), F32).at[:EMB].set(f_w1)
    nblk = D_B // LANES
    const = lambda shape: pl.BlockSpec(shape, lambda c: (0,) * len(shape))
    return pl.pallas_call(
        _filter_kernel,
        out_shape=[jax.ShapeDtypeStruct((SEQ, D_B), F32)] * 2,
        grid=(nblk,),
        in_specs=[
            const((SEQ, LANES)), const((SEQ, LANES)),
            const((LANES, FO)), const((1, FO)),
            const((FO, FO)), const((1, FO)),
            const((FO, FO)), const((1, FO)),
            const((1, FO)),
            pl.BlockSpec((FO, LANES), lambda c: (0, c)),
            pl.BlockSpec((FO, LANES), lambda c: (0, nblk + c)),
            pl.BlockSpec((1, LANES), lambda c: (0, c)),
        ],
        out_specs=[pl.BlockSpec((SEQ, LANES), lambda c: (0, c))] * 2,
        scratch_shapes=[pltpu.VMEM((SEQ, FO), F32)],
        compiler_params=_cparams(("arbitrary",)),
        name="hyena_filter",
    )(jnp.asarray(z_pad), jnp.asarray(t_lanes), w1_pad, f_b1.reshape(1, FO), f_w2,
      f_b2.reshape(1, FO), f_w3, f_b3.reshape(1, FO), f_freq.reshape(1, FO), f_wout, f_wout,
      jnp.asarray(deltas))


ROW_TILE = 8
N_K1_PAD = N_K1 + 1


@functools.lru_cache(maxsize=None)
def _dft_tables():
    k1 = np.arange(N_K1_PAD)
    n2 = np.arange(N_FAST)
    ang = 2.0 * math.pi * np.outer(k1, n2) / N_FFT
    tw = np.concatenate([np.cos(ang), -np.sin(ang)], axis=1)
    tw = np.broadcast_to(tw[:, :, None], (N_K1_PAD, 2 * N_FAST, LANES))
    ang2 = 2.0 * math.pi * np.outer(n2, n2) / N_FAST
    fr, fi = np.cos(ang2), -np.sin(ang2)
    f_fwd = np.block([[fr, -fi], [fi, fr]])
    f_inv = np.block([[fr, fi], [-fi, fr]])
    f32 = lambda a: np.ascontiguousarray(a, dtype=np.float32)
    return f32(tw), f32(f_fwd), f32(f_inv)


def _cmul_const(x, w):
    xr, xi = x
    c, s = w.real, w.imag
    eps = 1e-12
    if abs(s) < eps and abs(c - 1.0) < eps:
        return xr, xi
    if abs(c) < eps and abs(s - 1.0) < eps:
        return ("neg", xi), xr
    if abs(c) < eps and abs(s + 1.0) < eps:
        return xi, ("neg", xr)
    return xr * c - xi * s, xr * s + xi * c


def _is_neg(v):
    return isinstance(v, tuple) and v[0] == "neg"


def _plain(v):
    return -v[1] if _is_neg(v) else v


def _add(a, b):
    return a - b[1] if _is_neg(b) else a + b


def _sub(a, b):
    return a + b[1] if _is_neg(b) else a - b


def _fft_dit(xs, sign):
    n = len(xs)
    if n == 1:
        return xs
    ev = _fft_dit(xs[0::2], sign)
    od = _fft_dit(xs[1::2], sign)
    out = [None] * n
    for k in range(n // 2):
        w = complex(math.cos(2 * math.pi * k / n), sign * math.sin(2 * math.pi * k / n))
        tr, ti = _cmul_const(od[k], w)
        er, ei = ev[k]
        out[k] = (_add(er, tr), _add(ei, ti))
        out[k + n // 2] = (_sub(er, tr), _sub(ei, ti))
    return out


def _slow_forward(x):
    half, quarter = N_SLOW // 2, N_SLOW // 4
    z = [(x[2 * m], x[2 * m + 1]) for m in range(quarter)]
    zt = [z[0]]
    for m in range(1, quarter):
        tr, ti = _cmul_const(z[m], complex(math.cos(2 * math.pi * m / half), -math.sin(2 * math.pi * m / half)))
        zt.append((_plain(tr), _plain(ti)))
    ze = _fft_dit(z, -1)
    zo = _fft_dit(zt, -1)
    zc = [None] * half
    for j in range(quarter):
        zc[2 * j] = ze[j]
        zc[2 * j + 1] = zo[j]
    out = [None] * (half + 1)
    ar, ai = zc[0]
    out[0] = ((ar + ai) * 2.0, None)
    out[half] = ((ar - ai) * 2.0, None)
    ar, ai = zc[quarter]
    out[quarter] = (ar * 2.0, ai * -2.0)
    for k in range(1, quarter):
        ar, ai = zc[k]
        br, bi = zc[half - k]
        c, s = math.cos(2 * math.pi * k / N_SLOW), math.sin(2 * math.pi * k / N_SLOW)
        p, q, r, t = ar + br, ai - bi, ai + bi, ar - br
        u = r * c - t * s
        vn = t * (-c) - r * s
        out[k] = (p + u, q + vn)
        out[half - k] = (p - u, vn - q)
    return out


def _slow_inverse(xs):
    half, quarter = N_SLOW // 2, N_SLOW // 4
    zc = [None] * half
    ar, br = xs[0][0], xs[half][0]
    zc[0] = (ar + br, ar - br)
    ar, ai = xs[quarter]
    zc[quarter] = (ar * 2.0, ai * -2.0)
    for k in range(1, quarter):
        ar, ai = xs[k]
        br, bi = xs[half - k]
        c, s = math.cos(2 * math.pi * k / N_SLOW), math.sin(2 * math.pi * k / N_SLOW)
        p, q, t, r = ar + br, ai - bi, ar - br, ai + bi
        o_r = t * c - r * s
        o_i = t * s + r * c
        zc[k] = (p - o_i, q + o_r)
        zc[half - k] = (p + o_i, o_r - q)
    ze = _fft_dit(zc[0::2], +1)
    zo = _fft_dit(zc[1::2], +1)
    out = [None] * N_IN
    for m in range(quarter):
        tr, ti = _cmul_const(zo[m], complex(math.cos(2 * math.pi * m / half), math.sin(2 * math.pi * m / half)))
        out[2 * m] = _add(ze[m][0], tr)
        out[2 * m + 1] = _add(ze[m][1], ti)
    return out


def _slow_forward_pass(src_ref, a_ref):
    lanes = src_ref.shape[-1]
    zeros = jnp.zeros((ROW_TILE, lanes), F32)

    def body(j, carry):
        rows = pl.ds(pl.multiple_of(j * ROW_TILE, ROW_TILE), ROW_TILE)
        rows_im = pl.ds(pl.multiple_of(N_FAST + j * ROW_TILE, ROW_TILE), ROW_TILE)
        out = _slow_forward([src_ref[n1, rows, :] for n1 in range(N_IN)])
        for k1, (re, im) in enumerate(out):
            a_ref[k1, rows, :] = re
            a_ref[k1, rows_im, :] = zeros if im is None else im
        a_ref[N_K1, rows, :] = zeros
        a_ref[N_K1, rows_im, :] = zeros
        return carry

    lax.fori_loop(0, N_FAST // ROW_TILE, body, 0)


def _cmul(a, w):
    ar, ai, wr, wi = a[:N_FAST], a[N_FAST:], w[:N_FAST], w[N_FAST:]
    return ar * wr - ai * wi, ar * wi + ai * wr


def _cmul_conj(a, w):
    ar, ai, wr, wi = a[:N_FAST], a[N_FAST:], w[:N_FAST], w[N_FAST:]
    return ar * wr + ai * wi, ai * wr - ar * wi


def _spectrum_kernel(hf_ref, hb_ref, tw_ref, f_ref, o_ref, af_ref, ab_ref):
    _slow_forward_pass(hf_ref, af_ref)
    _slow_forward_pass(hb_ref, ab_ref)

    def body(k1, carry):
        tw = tw_ref[k1]
        f = f_ref[...]
        xf = _dot3(f, jnp.concatenate(_cmul(af_ref[k1], tw), axis=0))
        xb = _dot3(f, jnp.concatenate(_cmul(ab_ref[k1], tw), axis=0))
        re = xf[:N_FAST] + xb[:N_FAST]
        im = xf[N_FAST:] - xb[N_FAST:]
        o_ref[k1] = jnp.concatenate([re, im], axis=0) * (1.0 / (4 * N_FFT))
        return carry

    lax.fori_loop(0, N_K1_PAD, body, 0)


def filter_spectrum(hf, hb, tw, f_fwd):
    nblk = D_B // LANES
    slabs = lambda a: a.reshape(N_IN, N_FAST, D_B)
    whole = pl.BlockSpec(memory_space=pltpu.VMEM)
    spec_blk = (N_K1_PAD, 2 * N_FAST, LANES)
    return pl.pallas_call(
        _spectrum_kernel,
        out_shape=jax.ShapeDtypeStruct((N_K1_PAD, 2 * N_FAST, D_B), F32),
        grid=(nblk,),
        in_specs=[
            pl.BlockSpec((N_IN, N_FAST, LANES), lambda c: (0, 0, c)),
            pl.BlockSpec((N_IN, N_FAST, LANES), lambda c: (0, 0, c)),
            whole, whole,
        ],
        out_specs=pl.BlockSpec(spec_blk, lambda c: (0, 0, c)),
        scratch_shapes=[pltpu.VMEM(spec_blk, F32)] * 2,
        compiler_params=_cparams(("arbitrary",)),
        name="filter_spectrum",
    )(slabs(hf), slabs(hb), tw, f_fwd)


def _short_conv_slab(x_ref, w_ref, b_ref, n1):
    r0 = pl.multiple_of(n1 * N_FAST, N_FAST)
    edge = 16
    cur = x_ref[0, pl.ds(r0, N_FAST), :].astype(F32)
    before = x_ref[0, pl.ds(pl.multiple_of(jnp.maximum(r0 - edge, 0), edge), edge), :]
    after = x_ref[0, pl.ds(pl.multiple_of(jnp.minimum(r0 + N_FAST, SEQ - edge), edge), edge), :]
    before = jnp.where(n1 > 0, before.astype(F32)[edge - 1:edge], 0.0)
    after = jnp.where(n1 < N_IN - 1, after.astype(F32)[0:1], 0.0)
    row = lax.broadcasted_iota(jnp.int32, cur.shape, 0)
    prev = jnp.where(row == 0, before, pltpu.roll(cur, 1, axis=0))
    nxt = jnp.where(row == N_FAST - 1, after, pltpu.roll(cur, N_FAST - 1, axis=0))
    return prev * w_ref[0:1, :] + cur * w_ref[1:2, :] + nxt * w_ref[2:3, :] + b_ref[...]


def _hyena_kernel(x0_ref, x1_ref, v_ref, w0_ref, w1_ref, wv_ref, b0_ref, b1_ref, bv_ref,
                  hs_ref, hbias_ref, tw_ref, ff_ref, fi_ref, o_ref, u_ref, g0_ref, a_ref, z_ref):
    def conv(n1, carry):
        u_ref[n1] = _short_conv_slab(v_ref, wv_ref, bv_ref, n1) * _short_conv_slab(x1_ref, w1_ref, b1_ref, n1)
        g0_ref[n1] = _short_conv_slab(x0_ref, w0_ref, b0_ref, n1)
        return carry

    lax.fori_loop(0, N_IN, conv, 0)
    _slow_forward_pass(u_ref, a_ref)

    def freq(i, carry):
        ks = (2 * i, 2 * i + 1)
        tws = [tw_ref[k] for k in ks]
        a = [jnp.concatenate(_cmul(a_ref[k], tw), axis=0).astype(BF16) for k, tw in zip(ks, tws)]
        x = jnp.dot(ff_ref[...], jnp.concatenate(a, axis=1), preferred_element_type=F32)
        y = [jnp.concatenate(_cmul(x[:, j * LANES:(j + 1) * LANES], hs_ref[k]), axis=0).astype(BF16)
             for j, k in enumerate(ks)]
        z = jnp.dot(fi_ref[...], jnp.concatenate(y, axis=1), preferred_element_type=F32)
        for j, (k, tw) in enumerate(zip(ks, tws)):
            z_ref[k] = jnp.concatenate(_cmul_conj(z[:, j * LANES:(j + 1) * LANES], tw), axis=0)
        return carry

    lax.fori_loop(0, N_K1_PAD // 2, freq, 0, unroll=True)

    def back(j, carry):
        rows = pl.ds(pl.multiple_of(j * ROW_TILE, ROW_TILE), ROW_TILE)
        rows_im = pl.ds(pl.multiple_of(N_FAST + j * ROW_TILE, ROW_TILE), ROW_TILE)
        y = _slow_inverse([(z_ref[k1, rows, :], z_ref[k1, rows_im, :]) for k1 in range(N_K1)])
        for n1 in range(N_IN):
            u_ref[n1, rows, :] = y[n1] + u_ref[n1, rows, :] * hbias_ref[...]
        return carry

    lax.fori_loop(0, N_FAST // ROW_TILE, back, 0)

    def gate(n1, carry):
        rows = pl.ds(pl.multiple_of(n1 * N_FAST, N_FAST), N_FAST)
        o_ref[0, rows, :] = (u_ref[n1] * g0_ref[n1]).astype(BF16)
        return carry

    lax.fori_loop(0, N_IN, gate, 0)


def hyena(p3, sc_w, sc_b, h_spec, h_bias, tables):
    tw, f_fwd, f_inv = tables
    b = p3.shape[0]
    nblk = D_B // LANES
    base = 3 * D_A // LANES
    xblk = lambda off: pl.BlockSpec((1, SEQ, LANES), lambda c, i: (i, 0, base + off * nblk + c))
    wblk = lambda off: pl.BlockSpec((3, LANES), lambda c, i: (0, off * nblk + c))
    bblk = lambda off: pl.BlockSpec((1, LANES), lambda c, i: (0, off * nblk + c))
    whole = pl.BlockSpec(memory_space=pltpu.VMEM)
    sc_b2 = sc_b.reshape(1, 3 * D_B)
    spec_blk = (N_K1_PAD, 2 * N_FAST, LANES)
    return pl.pallas_call(
        _hyena_kernel,
        out_shape=jax.ShapeDtypeStruct((b, SEQ, D_B), BF16),
        grid=(nblk, b),
        in_specs=[
            xblk(0), xblk(1), xblk(2),
            wblk(0), wblk(1), wblk(2),
            bblk(0), bblk(1), bblk(2),
            pl.BlockSpec(spec_blk, lambda c, i: (0, 0, c)),
            pl.BlockSpec((1, LANES), lambda c, i: (0, c)),
            whole, whole, whole,
        ],
        out_specs=pl.BlockSpec((1, SEQ, LANES), lambda c, i: (i, 0, c)),
        scratch_shapes=[
            pltpu.VMEM((N_IN, N_FAST, LANES), F32),
            pltpu.VMEM((N_IN, N_FAST, LANES), F32),
            pltpu.VMEM(spec_blk, F32),
            pltpu.VMEM(spec_blk, F32),
        ],
        compiler_params=_cparams(("parallel", "arbitrary")),
        name="hyena",
    )(p3, p3, p3, sc_w, sc_w, sc_w, sc_b2, sc_b2, sc_b2,
      h_spec, h_bias.reshape(1, D_B), tw, f_fwd.astype(BF16), f_inv.astype(BF16))


def kernel(x_prompt, x_sample, norm_mix, norm_mlp, w_in_ab, sc_w, sc_b, q_gain, k_gain, rpb,
           f_w1, f_b1, f_w2, f_b2, f_w3, f_b3, f_wout, f_freq, h_bias, w_out_ab, w_in_c,
           v_gain, w_s, b_s, w_out_c, w_mlp1, w_mlp2):
    nb_p = x_prompt.shape[0]
    x = jnp.concatenate([x_prompt, x_sample], axis=0)
    nb = x.shape[0]
    x = x.reshape(nb * SEQ, D_MODEL)
    tables = tuple(jnp.asarray(a) for a in _dft_tables())

    w_in_ab, w_out_ab, w_in_c, w_out_c, w_mlp1, w_mlp2, w_s = (
        w.astype(BF16) for w in (w_in_ab, w_out_ab, w_in_c, w_out_c, w_mlp1, w_mlp2, w_s))

    for i in range(DEPTH):
        j = i // 2
        if i % 2 == 0:
            p3 = in_proj(x, norm_mix[i], w_in_ab, j, n=3 * (D_A + D_B), mode="qk_norm",
                         qk_gains=(q_gain[j], k_gain[j])).reshape(nb, SEQ, 3 * (D_A + D_B))
            a = natten(p3, natten_bias_table(rpb[j]))
            hf, hb = hyena_filter(f_w1[j], f_b1[j], f_w2[j], f_b2[j], f_w3[j], f_b3[j],
                                  f_wout[j], f_freq[j])
            h_spec = filter_spectrum(hf, hb, tables[0], tables[1])
            bo = hyena(p3, sc_w[j], sc_b[j], h_spec, h_bias[j], tables)
            x = proj_residual(x, a.reshape(nb * SEQ, D_A), bo.reshape(nb * SEQ, D_B), w_out_ab, j)
        else:
            zz = in_proj(x, norm_mix[i], w_in_c, j, n=2 * D_C, mode="gelu")
            b_lanes = jnp.broadcast_to(b_s[j][:, :, None], (G_C, CHUNK, LANES))
            m = gmlp_gate(zz, v_gain[j], w_s[j], b_lanes)
            x = proj_residual(x, m, m, w_out_c, j, a_col=0, b_col=1)
        x = mlp_residual(x, norm_mlp[i], w_mlp1, w_mlp2, i)

    x = x.reshape(nb, SEQ, D_MODEL)
    return x[:nb_p], x[nb_p:]
```

```python
import functools
import math

import numpy as np
import jax
import jax.numpy as jnp
from jax import lax
from jax.experimental import pallas as pl
from jax.experimental.pallas import tpu as pltpu

F32 = jnp.float32
BF16 = jnp.bfloat16

D_MODEL = 2048
DEPTH = 4
SEQ = 4096
GRID_W = 64
WIN_H = 8
WIN_W = 16
DH = 128
H_A = 8
D_A = H_A * DH
D_B = 1024
EMB = 33
FO = 64
FAST_DECAY = 0.3
SLOW_DECAY = 1.5
DECAY_TARGET = 1e-2
D_C = D_MODEL
G_C = 8
DG_C = D_C // G_C
CHUNK = 128
D_FF = 4 * D_MODEL
EPS = 1e-6

LANES = 128
VMEM_LIMIT_BYTES = 52 * 1024 * 1024

N_FFT = 2 * SEQ
N_FAST = 128
N_SLOW = N_FFT // N_FAST
N_IN = SEQ // N_FAST
N_K1 = N_SLOW // 2 + 1


def _cparams(sem):
    return pltpu.CompilerParams(dimension_semantics=sem, vmem_limit_bytes=VMEM_LIMIT_BYTES)


def _tile_ranges(parts, tm):
    ranges, lo = [], 0
    for p in parts:
        ranges.append((lo, lo + p.shape[0] // tm))
        lo = ranges[-1][1]
    return ranges


def _part_spec(block, rng, col_fn):
    lo, hi = rng
    return pl.BlockSpec(block, lambda i, j: (jnp.clip(i - lo, 0, hi - lo - 1), col_fn(j)))


def _in_range(i, rng):
    return (i >= rng[0]) & (i < rng[1])


def _rms_scale(x, g):
    return x * lax.rsqrt(jnp.mean(x * x, axis=-1, keepdims=True) + EPS) * g


def _gelu_tanh(x):
    return 0.5 * x * (1.0 + jnp.tanh(math.sqrt(2.0 / math.pi) * (x + 0.044715 * (x * x * x))))


def _split_bf16(x):
    hi = x.astype(BF16)
    lo = (x - hi.astype(F32)).astype(BF16)
    return hi, lo


def _dot3(a, b):
    ah, al = _split_bf16(a)
    bh, bl = _split_bf16(b)
    d = functools.partial(jnp.dot, preferred_element_type=F32)
    return d(ah, bh) + (d(ah, bl) + d(al, bh))


def _in_proj_kernel(*refs, mode, ranges):
    x_refs, refs = refs[:len(ranges)], refs[len(ranges):]
    if mode == "gelu":
        g_ref, w_ref, o_ref, h_ref = refs
    else:
        g_ref, w_ref, qg_ref, kg_ref, o_ref, h_ref = refs
    i, j = pl.program_id(0), pl.program_id(1)

    for x_ref, rng in zip(x_refs, ranges):
        @pl.when((j == 0) & _in_range(i, rng))
        def _():
            h_ref[...] = _rms_scale(x_ref[...], g_ref[...]).astype(BF16)

    acc = jnp.dot(h_ref[...], w_ref[...], preferred_element_type=F32)
    if mode == "gelu":
        o_ref[...] = _gelu_tanh(acc).astype(BF16)
    else:
        gain = jnp.where(j == 0, qg_ref[...] * (DH ** -0.5), kg_ref[...])
        for c in range(0, o_ref.shape[1], DH):
            seg = acc[:, c:c + DH]
            o_ref[:, c:c + DH] = jnp.where(j < 2, _rms_scale(seg, gain), seg).astype(BF16)


def in_proj(x_parts, g, w, layer, *, n, mode, qk_gains=(), tn=1024):
    d = x_parts[0].shape[1]
    tm = 1024 // len(x_parts)
    t = sum(p.shape[0] for p in x_parts)
    ranges = _tile_ranges(x_parts, tm)
    return pl.pallas_call(
        functools.partial(_in_proj_kernel, mode=mode, ranges=ranges),
        out_shape=jax.ShapeDtypeStruct((t, n), BF16),
        grid=(t // tm, n // tn),
        in_specs=[_part_spec((tm, d), rng, lambda j: 0) for rng in ranges] + [
            pl.BlockSpec((1, d), lambda i, j: (0, 0)),
            pl.BlockSpec((None, d, tn), lambda i, j: (layer, 0, j)),
        ] + [pl.BlockSpec((1, DH), lambda i, j: (0, 0))] * len(qk_gains),
        out_specs=pl.BlockSpec((tm, tn), lambda i, j: (i, j)),
        scratch_shapes=[pltpu.VMEM((tm, d), BF16)],
        compiler_params=_cparams(("parallel", "arbitrary")),
        name="in_proj",
    )(*x_parts, g.reshape(1, d), w, *[gain.reshape(1, DH) for gain in qk_gains])


def _mlp_kernel(x_ref, g_ref, w1_ref, w2_ref, *refs, ranges):
    o_refs, h_ref = refs[:-1], refs[-1]

    def run(o_ref):
        @pl.when(pl.program_id(1) == 0)
        def _():
            x = x_ref[...]
            h_ref[...] = _rms_scale(x, g_ref[...]).astype(BF16)
            o_ref[...] = x

        a = jnp.dot(h_ref[...], w1_ref[...], preferred_element_type=F32)
        a = jnp.square(jnp.maximum(a, 0.0)).astype(BF16)
        o_ref[...] += jnp.dot(a, w2_ref[...], preferred_element_type=F32)

    if len(o_refs) == 1:
        run(o_refs[0])
    else:
        for o_ref, rng in zip(o_refs, ranges):
            pl.when(_in_range(pl.program_id(0), rng))(functools.partial(run, o_ref))


def mlp_residual(x, g, w1, w2, layer, *, out_rows=None, tm=512, tf=1024):
    t, d = x.shape
    f = w1.shape[2]
    out_rows = (t,) if out_rows is None else out_rows
    outs = [jax.ShapeDtypeStruct((r, d), F32) for r in out_rows]
    ranges = _tile_ranges(outs, tm)
    res = pl.pallas_call(
        functools.partial(_mlp_kernel, ranges=ranges),
        out_shape=outs,
        grid=(t // tm, f // tf),
        in_specs=[
            pl.BlockSpec((tm, d), lambda i, k: (i, 0)),
            pl.BlockSpec((1, d), lambda i, k: (0, 0)),
            pl.BlockSpec((None, d, tf), lambda i, k: (layer, 0, k)),
            pl.BlockSpec((None, tf, d), lambda i, k: (layer, k, 0)),
        ],
        out_specs=[_part_spec((tm, d), rng, lambda k: 0) for rng in ranges],
        scratch_shapes=[pltpu.VMEM((tm, d), BF16)],
        compiler_params=_cparams(("arbitrary", "arbitrary")),
        name="mlp_residual",
    )(x, g.reshape(1, d), w1, w2)
    return res[0] if len(res) == 1 else res


def _proj_residual_kernel(*refs, ranges):
    x_refs, (a_ref, b_ref, wa_ref, wb_ref, o_ref) = refs[:len(ranges)], refs[len(ranges):]
    acc = jnp.dot(a_ref[...], wa_ref[...], preferred_element_type=F32)
    acc += jnp.dot(b_ref[...], wb_ref[...], preferred_element_type=F32)
    x = x_refs[0][...]
    for x_ref, rng in zip(x_refs[1:], ranges[1:]):
        x = jnp.where(pl.program_id(0) >= rng[0], x_ref[...], x)
    o_ref[...] = x + acc


def proj_residual(x_parts, a, b, w, layer, *, a_col=0, b_col=0, tm=1024, tn=1024):
    n = x_parts[0].shape[1]
    t = sum(p.shape[0] for p in x_parts)
    ranges = _tile_ranges(x_parts, tm)
    k = w.shape[1] // 2
    return pl.pallas_call(
        functools.partial(_proj_residual_kernel, ranges=ranges),
        out_shape=jax.ShapeDtypeStruct((t, n), F32),
        grid=(t // tm, n // tn),
        in_specs=[_part_spec((tm, tn), rng, lambda j: j) for rng in ranges] + [
            pl.BlockSpec((tm, k), lambda i, j: (i, a_col)),
            pl.BlockSpec((tm, k), lambda i, j: (i, b_col)),
            pl.BlockSpec((None, k, tn), lambda i, j: (layer, 0, j)),
            pl.BlockSpec((None, k, tn), lambda i, j: (layer, 1, j)),
        ],
        out_specs=pl.BlockSpec((tm, tn), lambda i, j: (i, j)),
        compiler_params=_cparams(("parallel", "arbitrary")),
        name="proj_residual",
    )(*x_parts, a, b, w, w)


def _gate_kernel(zz_ref, vg_ref, ws_ref, bs_ref, o_ref, vn_ref, *, tm):
    for c in range(tm // CHUNK):
        rows = slice(c * CHUNK, (c + 1) * CHUNK)
        vn_ref[rows, :] = _rms_scale(zz_ref[rows, D_C:].astype(F32), vg_ref[...]).astype(BF16)
        for g in range(G_C):
            cols = slice(g * DG_C, (g + 1) * DG_C)
            s = jnp.dot(ws_ref[g], vn_ref[rows, cols], preferred_element_type=F32)
            bias = bs_ref[g]
            s = s + jnp.concatenate([bias] * (DG_C // LANES), axis=1)
            o_ref[rows, cols] = (zz_ref[rows, cols].astype(F32) * s).astype(BF16)


def gmlp_gate(zz, v_gain, w_s, b_s_lanes, *, tm=512):
    t = zz.shape[0]
    return pl.pallas_call(
        functools.partial(_gate_kernel, tm=tm),
        out_shape=jax.ShapeDtypeStruct((t, D_C), BF16),
        grid=(t // tm,),
        in_specs=[
            pl.BlockSpec((tm, 2 * D_C), lambda i: (i, 0)),
            pl.BlockSpec((1, D_C), lambda i: (0, 0)),
            pl.BlockSpec((G_C, CHUNK, CHUNK), lambda i: (0, 0, 0)),
            pl.BlockSpec((G_C, CHUNK, LANES), lambda i: (0, 0, 0)),
        ],
        out_specs=pl.BlockSpec((tm, D_C), lambda i: (i, 0)),
        scratch_shapes=[pltpu.VMEM((tm, D_C), BF16)],
        compiler_params=_cparams(("parallel",)),
        name="gmlp_gate",
    )(zz, v_gain.reshape(1, D_C), w_s, b_s_lanes)


N_ROWS = SEQ // GRID_W
GROUP_ROWS = 4
N_GROUPS = N_ROWS // GROUP_ROWS
KEY_ROWS = 12
NQ = GROUP_ROWS * GRID_W
NK = KEY_ROWS * GRID_W
SOFTMAX_ROWS = 64


def _natten_kernel(q_ref, k_ref, v_ref, bias_ref, o_ref, s_ref, p_ref):
    def q_rows(g):
        return pl.ds(pl.multiple_of(g * NQ, NQ), NQ)

    def k_rows(g):
        ws = jnp.clip(GROUP_ROWS * g - WIN_H // 2, 0, N_ROWS - KEY_ROWS)
        return pl.ds(pl.multiple_of(ws * GRID_W, GRID_W), NK)

    def scores(g, slot):
        s_ref[slot] = lax.dot_general(q_ref[0, q_rows(g), :], k_ref[0, k_rows(g), :],
                                      (((1,), (1,)), ((), ())), preferred_element_type=F32)

    def softmax(g, slot):
        kind = jnp.where(g == 0, 0, jnp.where(g == N_GROUPS - 1, 2, 1))
        for c in range(NQ // SOFTMAX_ROWS):
            rows = slice(c * SOFTMAX_ROWS, (c + 1) * SOFTMAX_ROWS)
            s = s_ref[slot, rows, :] + bias_ref[0, kind, rows, :]
            p = jnp.exp(s - jnp.max(s, axis=-1, keepdims=True))
            p = p * (1.0 / jnp.sum(p, axis=-1, keepdims=True))
            p_ref[slot, rows, :] = p.astype(BF16)

    def values(g, slot):
        o = jnp.dot(p_ref[slot], v_ref[0, k_rows(g), :], preferred_element_type=F32)
        o_ref[0, q_rows(g), :] = o.astype(BF16)

    scores(0, 0)
    scores(1, 1)
    softmax(0, 0)

    def body(i, carry):
        g = 2 * i + 1
        scores(g + 1, 0)
        softmax(g, 1)
        values(g - 1, 0)
        scores(g + 2, 1)
        softmax(g + 1, 0)
        values(g, 1)
        return carry

    lax.fori_loop(0, N_GROUPS // 2 - 1, body, 0)
    last = N_GROUPS - 1
    softmax(last, 1)
    values(last - 1, 0)
    values(last, 1)


def natten(qkv3, bias_tab):
    b = qkv3.shape[0]
    blk = (1, SEQ, DH)
    return pl.pallas_call(
        _natten_kernel,
        out_shape=jax.ShapeDtypeStruct((b, SEQ, D_A), BF16),
        grid=(H_A, b),
        in_specs=[
            pl.BlockSpec(blk, lambda h, i: (i, 0, h)),
            pl.BlockSpec(blk, lambda h, i: (i, 0, H_A + h)),
            pl.BlockSpec(blk, lambda h, i: (i, 0, 2 * H_A + h)),
            pl.BlockSpec((1, 3, NQ, NK), lambda h, i: (h, 0, 0, 0)),
        ],
        out_specs=pl.BlockSpec(blk, lambda h, i: (i, 0, h)),
        scratch_shapes=[pltpu.VMEM((2, NQ, NK), F32), pltpu.VMEM((2, NQ, NK), BF16)],
        compiler_params=_cparams(("parallel", "arbitrary")),
        name="natten",
    )(qkv3, qkv3, qkv3, bias_tab)


def natten_bias_table(rpb):
    col = np.arange(GRID_W)
    cs = np.clip(col - WIN_W // 2, 0, GRID_W - WIN_W)
    qc, kc = col[:, None], col[None, :]
    col_ok = (kc >= cs[:, None]) & (kc < cs[:, None] + WIN_W)
    dc = np.clip(kc - qc, -(WIN_W - 1), WIN_W - 1) + WIN_W - 1
    bias_c = rpb.astype(F32)[:, :, dc]
    a = np.arange(GROUP_ROWS)
    q_off = np.stack([a, a + WIN_H // 2, a + KEY_ROWS - GROUP_ROWS])
    w_off = np.stack([0 * a, a, 0 * a + KEY_ROWS - WIN_H])
    i = np.arange(KEY_ROWS)
    row_ok = (i >= w_off[..., None]) & (i < w_off[..., None] + WIN_H)
    dr = np.clip(i - q_off[..., None] + WIN_H - 1, 0, 2 * WIN_H - 2)
    tab = bias_c[:, dr]
    ok = row_ok[None, :, :, :, None, None] & col_ok[None, None, None, None]
    tab = jnp.where(ok, tab, -jnp.inf)
    return tab.transpose(0, 1, 2, 4, 3, 5).reshape(H_A, 3, NQ, NK)


FILT_ROWS = 512


def _filter_kernel(z_ref, t_ref, w1_ref, b1_ref, w2_ref, b2_ref, w3_ref, b3_ref, fr_ref,
                   wf_ref, wb_ref, dl_ref, hf_ref, hb_ref, h3_ref):
    @pl.when(pl.program_id(0) == 0)
    def _():
        def mlp(i, carry):
            rows = pl.ds(pl.multiple_of(i * FILT_ROWS, FILT_ROWS), FILT_ROWS)
            fr = fr_ref[...]
            h = jnp.sin(fr * (_dot3(z_ref[rows, :], w1_ref[...]) + b1_ref[...]))
            h = jnp.sin(fr * (_dot3(h, w2_ref[...]) + b2_ref[...]))
            h3_ref[rows, :] = jnp.sin(fr * (_dot3(h, w3_ref[...]) + b3_ref[...]))
            return carry

        lax.fori_loop(0, SEQ // FILT_ROWS, mlp, 0)

    def taps(i, carry):
        sf, sb = carry
        rows = pl.ds(pl.multiple_of(i * FILT_ROWS, FILT_ROWS), FILT_ROWS)
        h3 = h3_ref[rows, :]
        decay = jnp.exp(-t_ref[rows, :] * dl_ref[...])
        hf = _dot3(h3, wf_ref[...]) * decay
        hb = _dot3(h3, wb_ref[...]) * decay
        t_idx = lax.broadcasted_iota(jnp.int32, hb.shape, 0) + i * FILT_ROWS
        hb = jnp.where(t_idx == 0, 0.0, hb)
        hf_ref[rows, :] = hf
        hb_ref[rows, :] = hb
        return (sf + jnp.sum(jnp.abs(hf), axis=0, keepdims=True),
                sb + jnp.sum(jnp.abs(hb), axis=0, keepdims=True))

    zero = jnp.zeros((1, LANES), F32)
    sf, sb = lax.fori_loop(0, SEQ // FILT_ROWS, taps, (zero, zero))
    inv = 1.0 / (sf + sb)

    def scale(i, carry):
        rows = pl.ds(pl.multiple_of(i * FILT_ROWS, FILT_ROWS), FILT_ROWS)
        hf_ref[rows, :] = hf_ref[rows, :] * inv
        hb_ref[rows, :] = hb_ref[rows, :] * inv
        return carry

    lax.fori_loop(0, SEQ // FILT_ROWS, scale, 0)


def _filter_constants():
    t = np.linspace(0.0, 1.0, SEQ, dtype=np.float32).astype(np.float64)[:, None]
    bands = (EMB - 1) // 2
    w = 2.0 * math.pi * np.arange(SEQ, dtype=np.float64)[:, None] / SEQ
    f = np.linspace(1e-4, bands - 1, bands, dtype=np.float32).astype(np.float64)[None, :]
    z = np.concatenate([t, np.cos(f * w), -np.sin(f * w)], axis=-1)
    z_pad = np.zeros((SEQ, LANES), np.float32)
    z_pad[:, :EMB] = z
    t_lanes = np.broadcast_to(t.astype(np.float32), (SEQ, LANES)).copy()
    deltas = np.abs(np.linspace(math.log(DECAY_TARGET) / FAST_DECAY,
                                math.log(DECAY_TARGET) / SLOW_DECAY, D_B, dtype=np.float32))
    return z_pad, t_lanes, deltas.reshape(1, D_B)


def hyena_filter(f_w1, f_b1, f_w2, f_b2, f_w3, f_b3, f_wout, f_freq):
    z_pad, t_lanes, deltas = _filter_constants()
    w1_pad = jnp.zeros((LANES, FO), F32).at[:EMB].set(f_w1)
    nblk = D_B // LANES
    const = lambda shape: pl.BlockSpec(shape, lambda c: (0,) * len(shape))
    return pl.pallas_call(
        _filter_kernel,
        out_shape=[jax.ShapeDtypeStruct((SEQ, D_B), F32)] * 2,
        grid=(nblk,),
        in_specs=[
            const((SEQ, LANES)), const((SEQ, LANES)),
            const((LANES, FO)), const((1, FO)),
            const((FO, FO)), const((1, FO)),
            const((FO, FO)), const((1, FO)),
            const((1, FO)),
            pl.BlockSpec((FO, LANES), lambda c: (0, c)),
            pl.BlockSpec((FO, LANES), lambda c: (0, nblk + c)),
            pl.BlockSpec((1, LANES), lambda c: (0, c)),
        ],
        out_specs=[pl.BlockSpec((SEQ, LANES), lambda c: (0, c))] * 2,
        scratch_shapes=[pltpu.VMEM((SEQ, FO), F32)],
        compiler_params=_cparams(("arbitrary",)),
        name="hyena_filter",
    )(jnp.asarray(z_pad), jnp.asarray(t_lanes), w1_pad, f_b1.reshape(1, FO), f_w2,
      f_b2.reshape(1, FO), f_w3, f_b3.reshape(1, FO), f_freq.reshape(1, FO), f_wout, f_wout,
      jnp.asarray(deltas))


ROW_TILE = 8
N_K1_PAD = N_K1 + 1


@functools.lru_cache(maxsize=None)
def _dft_tables():
    k1 = np.arange(N_K1_PAD)
    n2 = np.arange(N_FAST)
    ang = 2.0 * math.pi * np.outer(k1, n2) / N_FFT
    tw = np.concatenate([np.cos(ang), -np.sin(ang)], axis=1)
    tw = np.broadcast_to(tw[:, :, None], (N_K1_PAD, 2 * N_FAST, LANES))
    ang2 = 2.0 * math.pi * np.outer(n2, n2) / N_FAST
    fr, fi = np.cos(ang2), -np.sin(ang2)
    f_fwd = np.block([[fr, -fi], [fi, fr]])
    f_inv = np.block([[fr, fi], [-fi, fr]])
    f32 = lambda a: np.ascontiguousarray(a, dtype=np.float32)
    return f32(tw), f32(f_fwd), f32(f_inv)


def _cmul_const(x, w):
    xr, xi = x
    c, s = w.real, w.imag
    eps = 1e-12
    if abs(s) < eps and abs(c - 1.0) < eps:
        return xr, xi
    if abs(c) < eps and abs(s - 1.0) < eps:
        return ("neg", xi), xr
    if abs(c) < eps and abs(s + 1.0) < eps:
        return xi, ("neg", xr)
    return xr * c - xi * s, xr * s + xi * c


def _is_neg(v):
    return isinstance(v, tuple) and v[0] == "neg"


def _plain(v):
    return -v[1] if _is_neg(v) else v


def _add(a, b):
    return a - b[1] if _is_neg(b) else a + b


def _sub(a, b):
    return a + b[1] if _is_neg(b) else a - b


def _fft_dit(xs, sign):
    n = len(xs)
    if n == 1:
        return xs
    ev = _fft_dit(xs[0::2], sign)
    od = _fft_dit(xs[1::2], sign)
    out = [None] * n
    for k in range(n // 2):
        w = complex(math.cos(2 * math.pi * k / n), sign * math.sin(2 * math.pi * k / n))
        tr, ti = _cmul_const(od[k], w)
        er, ei = ev[k]
        out[k] = (_add(er, tr), _add(ei, ti))
        out[k + n // 2] = (_sub(er, tr), _sub(ei, ti))
    return out


def _slow_forward(x):
    half, quarter = N_SLOW // 2, N_SLOW // 4
    z = [(x[2 * m], x[2 * m + 1]) for m in range(quarter)]
    zt = [z[0]]
    for m in range(1, quarter):
        tr, ti = _cmul_const(z[m], complex(math.cos(2 * math.pi * m / half), -math.sin(2 * math.pi * m / half)))
        zt.append((_plain(tr), _plain(ti)))
    ze = _fft_dit(z, -1)
    zo = _fft_dit(zt, -1)
    zc = [None] * half
    for j in range(quarter):
        zc[2 * j] = ze[j]
        zc[2 * j + 1] = zo[j]
    out = [None] * (half + 1)
    ar, ai = zc[0]
    out[0] = ((ar + ai) * 2.0, None)
    out[half] = ((ar - ai) * 2.0, None)
    ar, ai = zc[quarter]
    out[quarter] = (ar * 2.0, ai * -2.0)
    for k in range(1, quarter):
        ar, ai = zc[k]
        br, bi = zc[half - k]
        c, s = math.cos(2 * math.pi * k / N_SLOW), math.sin(2 * math.pi * k / N_SLOW)
        p, q, r, t = ar + br, ai - bi, ai + bi, ar - br
        u = r * c - t * s
        vn = t * (-c) - r * s
        out[k] = (p + u, q + vn)
        out[half - k] = (p - u, vn - q)
    return out


def _slow_inverse(xs):
    half, quarter = N_SLOW // 2, N_SLOW // 4
    zc = [None] * half
    ar, br = xs[0][0], xs[half][0]
    zc[0] = (ar + br, ar - br)
    ar, ai = xs[quarter]
    zc[quarter] = (ar * 2.0, ai * -2.0)
    for k in range(1, quarter):
        ar, ai = xs[k]
        br, bi = xs[half - k]
        c, s = math.cos(2 * math.pi * k / N_SLOW), math.sin(2 * math.pi * k / N_SLOW)
        p, q, t, r = ar + br, ai - bi, ar - br, ai + bi
        o_r = t * c - r * s
        o_i = t * s + r * c
        zc[k] = (p - o_i, q + o_r)
        zc[half - k] = (p + o_i, o_r - q)
    ze = _fft_dit(zc[0::2], +1)
    zo = _fft_dit(zc[1::2], +1)
    out = [None] * N_IN
    for m in range(quarter):
        tr, ti = _cmul_const(zo[m], complex(math.cos(2 * math.pi * m / half), math.sin(2 * math.pi * m / half)))
        out[2 * m] = _add(ze[m][0], tr)
        out[2 * m + 1] = _add(ze[m][1], ti)
    return out


def _slow_forward_pass(src_ref, a_ref):
    lanes = src_ref.shape[-1]
    zeros = jnp.zeros((ROW_TILE, lanes), F32)

    def body(j, carry):
        rows = pl.ds(pl.multiple_of(j * ROW_TILE, ROW_TILE), ROW_TILE)
        rows_im = pl.ds(pl.multiple_of(N_FAST + j * ROW_TILE, ROW_TILE), ROW_TILE)
        out = _slow_forward([src_ref[n1, rows, :] for n1 in range(N_IN)])
        for k1, (re, im) in enumerate(out):
            a_ref[k1, rows, :] = re
            a_ref[k1, rows_im, :] = zeros if im is None else im
        a_ref[N_K1, rows, :] = zeros
        a_ref[N_K1, rows_im, :] = zeros
        return carry

    lax.fori_loop(0, N_FAST // ROW_TILE, body, 0)


def _cmul(a, w):
    ar, ai, wr, wi = a[:N_FAST], a[N_FAST:], w[:N_FAST], w[N_FAST:]
    return ar * wr - ai * wi, ar * wi + ai * wr


def _cmul_conj(a, w):
    ar, ai, wr, wi = a[:N_FAST], a[N_FAST:], w[:N_FAST], w[N_FAST:]
    return ar * wr + ai * wi, ai * wr - ar * wi


def _spectrum_kernel(hf_ref, hb_ref, tw_ref, f_ref, o_ref, af_ref, ab_ref):
    _slow_forward_pass(hf_ref, af_ref)
    _slow_forward_pass(hb_ref, ab_ref)

    def body(k1, carry):
        tw = tw_ref[k1]
        f = f_ref[...]
        xf = _dot3(f, jnp.concatenate(_cmul(af_ref[k1], tw), axis=0))
        xb = _dot3(f, jnp.concatenate(_cmul(ab_ref[k1], tw), axis=0))
        re = xf[:N_FAST] + xb[:N_FAST]
        im = xf[N_FAST:] - xb[N_FAST:]
        o_ref[k1] = jnp.concatenate([re, im], axis=0) * (1.0 / (4 * N_FFT))
        return carry

    lax.fori_loop(0, N_K1_PAD, body, 0)


def filter_spectrum(hf, hb, tw, f_fwd):
    nblk = D_B // LANES
    slabs = lambda a: a.reshape(N_IN, N_FAST, D_B)
    whole = pl.BlockSpec(memory_space=pltpu.VMEM)
    spec_blk = (N_K1_PAD, 2 * N_FAST, LANES)
    return pl.pallas_call(
        _spectrum_kernel,
        out_shape=jax.ShapeDtypeStruct((N_K1_PAD, 2 * N_FAST, D_B), F32),
        grid=(nblk,),
        in_specs=[
            pl.BlockSpec((N_IN, N_FAST, LANES), lambda c: (0, 0, c)),
            pl.BlockSpec((N_IN, N_FAST, LANES), lambda c: (0, 0, c)),
            whole, whole,
        ],
        out_specs=pl.BlockSpec(spec_blk, lambda c: (0, 0, c)),
        scratch_shapes=[pltpu.VMEM(spec_blk, F32)] * 2,
        compiler_params=_cparams(("arbitrary",)),
        name="filter_spectrum",
    )(slabs(hf), slabs(hb), tw, f_fwd)


def _short_conv_slab(x_ref, w_ref, b_ref, n1):
    r0 = pl.multiple_of(n1 * N_FAST, N_FAST)
    edge = 16
    cur = x_ref[0, pl.ds(r0, N_FAST), :].astype(F32)
    before = x_ref[0, pl.ds(pl.multiple_of(jnp.maximum(r0 - edge, 0), edge), edge), :]
    after = x_ref[0, pl.ds(pl.multiple_of(jnp.minimum(r0 + N_FAST, SEQ - edge), edge), edge), :]
    before = jnp.where(n1 > 0, before.astype(F32)[edge - 1:edge], 0.0)
    after = jnp.where(n1 < N_IN - 1, after.astype(F32)[0:1], 0.0)
    row = lax.broadcasted_iota(jnp.int32, cur.shape, 0)
    prev = jnp.where(row == 0, before, pltpu.roll(cur, 1, axis=0))
    nxt = jnp.where(row == N_FAST - 1, after, pltpu.roll(cur, N_FAST - 1, axis=0))
    return prev * w_ref[0:1, :] + cur * w_ref[1:2, :] + nxt * w_ref[2:3, :] + b_ref[...]


def _hyena_kernel(x0_ref, x1_ref, v_ref, w0_ref, w1_ref, wv_ref, b0_ref, b1_ref, bv_ref,
                  hs_ref, hbias_ref, tw_ref, ff_ref, fi_ref, o_ref, u_ref, g0_ref, a_ref, z_ref):
    def conv(n1, carry):
        u_ref[n1] = _short_conv_slab(v_ref, wv_ref, bv_ref, n1) * _short_conv_slab(x1_ref, w1_ref, b1_ref, n1)
        g0_ref[n1] = _short_conv_slab(x0_ref, w0_ref, b0_ref, n1)
        return carry

    lax.fori_loop(0, N_IN, conv, 0)
    _slow_forward_pass(u_ref, a_ref)

    def freq(i, carry):
        ks = (2 * i, 2 * i + 1)
        tws = [tw_ref[k] for k in ks]
        a = [jnp.concatenate(_cmul(a_ref[k], tw), axis=0).astype(BF16) for k, tw in zip(ks, tws)]
        x = jnp.dot(ff_ref[...], jnp.concatenate(a, axis=1), preferred_element_type=F32)
        y = [jnp.concatenate(_cmul(x[:, j * LANES:(j + 1) * LANES], hs_ref[k]), axis=0).astype(BF16)
             for j, k in enumerate(ks)]
        z = jnp.dot(fi_ref[...], jnp.concatenate(y, axis=1), preferred_element_type=F32)
        for j, (k, tw) in enumerate(zip(ks, tws)):
            z_ref[k] = jnp.concatenate(_cmul_conj(z[:, j * LANES:(j + 1) * LANES], tw), axis=0)
        return carry

    lax.fori_loop(0, N_K1_PAD // 2, freq, 0, unroll=True)

    def back(j, carry):
        rows = pl.ds(pl.multiple_of(j * ROW_TILE, ROW_TILE), ROW_TILE)
        rows_im = pl.ds(pl.multiple_of(N_FAST + j * ROW_TILE, ROW_TILE), ROW_TILE)
        y = _slow_inverse([(z_ref[k1, rows, :], z_ref[k1, rows_im, :]) for k1 in range(N_K1)])
        for n1 in range(N_IN):
            u_ref[n1, rows, :] = y[n1] + u_ref[n1, rows, :] * hbias_ref[...]
        return carry

    lax.fori_loop(0, N_FAST // ROW_TILE, back, 0)

    def gate(n1, carry):
        rows = pl.ds(pl.multiple_of(n1 * N_FAST, N_FAST), N_FAST)
        o_ref[0, rows, :] = (u_ref[n1] * g0_ref[n1]).astype(BF16)
        return carry

    lax.fori_loop(0, N_IN, gate, 0)


def hyena(p3, sc_w, sc_b, h_spec, h_bias, tables):
    tw, f_fwd, f_inv = tables
    b = p3.shape[0]
    nblk = D_B // LANES
    base = 3 * D_A // LANES
    xblk = lambda off: pl.BlockSpec((1, SEQ, LANES), lambda c, i: (i, 0, base + off * nblk + c))
    wblk = lambda off: pl.BlockSpec((3, LANES), lambda c, i: (0, off * nblk + c))
    bblk = lambda off: pl.BlockSpec((1, LANES), lambda c, i: (0, off * nblk + c))
    whole = pl.BlockSpec(memory_space=pltpu.VMEM)
    sc_b2 = sc_b.reshape(1, 3 * D_B)
    spec_blk = (N_K1_PAD, 2 * N_FAST, LANES)
    return pl.pallas_call(
        _hyena_kernel,
        out_shape=jax.ShapeDtypeStruct((b, SEQ, D_B), BF16),
        grid=(nblk, b),
        in_specs=[
            xblk(0), xblk(1), xblk(2),
            wblk(0), wblk(1), wblk(2),
            bblk(0), bblk(1), bblk(2),
            pl.BlockSpec(spec_blk, lambda c, i: (0, 0, c)),
            pl.BlockSpec((1, LANES), lambda c, i: (0, c)),
            whole, whole, whole,
        ],
        out_specs=pl.BlockSpec((1, SEQ, LANES), lambda c, i: (i, 0, c)),
        scratch_shapes=[
            pltpu.VMEM((N_IN, N_FAST, LANES), F32),
            pltpu.VMEM((N_IN, N_FAST, LANES), F32),
            pltpu.VMEM(spec_blk, F32),
            pltpu.VMEM(spec_blk, F32),
        ],
        compiler_params=_cparams(("parallel", "arbitrary")),
        name="hyena",
    )(p3, p3, p3, sc_w, sc_w, sc_w, sc_b2, sc_b2, sc_b2,
      h_spec, h_bias.reshape(1, D_B), tw, f_fwd.astype(BF16), f_inv.astype(BF16))


def kernel(x_prompt, x_sample, norm_mix, norm_mlp, w_in_ab, sc_w, sc_b, q_gain, k_gain, rpb,
           f_w1, f_b1, f_w2, f_b2, f_w3, f_b3, f_wout, f_freq, h_bias, w_out_ab, w_in_c,
           v_gain, w_s, b_s, w_out_c, w_mlp1, w_mlp2):
    nb_p, nb_s = x_prompt.shape[0], x_sample.shape[0]
    nb = nb_p + nb_s
    x = (x_prompt.reshape(nb_p * SEQ, D_MODEL), x_sample.reshape(nb_s * SEQ, D_MODEL))
    tables = tuple(jnp.asarray(a) for a in _dft_tables())

    w_in_ab, w_out_ab, w_in_c, w_out_c, w_mlp1, w_mlp2, w_s = (
        w.astype(BF16) for w in (w_in_ab, w_out_ab, w_in_c, w_out_c, w_mlp1, w_mlp2, w_s))

    for i in range(DEPTH):
        j = i // 2
        if i % 2 == 0:
            p3 = in_proj(x, norm_mix[i], w_in_ab, j, n=3 * (D_A + D_B), mode="qk_norm",
                         qk_gains=(q_gain[j], k_gain[j])).reshape(nb, SEQ, 3 * (D_A + D_B))
            a = natten(p3, natten_bias_table(rpb[j]))
            hf, hb = hyena_filter(f_w1[j], f_b1[j], f_w2[j], f_b2[j], f_w3[j], f_b3[j],
                                  f_wout[j], f_freq[j])
            h_spec = filter_spectrum(hf, hb, tables[0], tables[1])
            bo = hyena(p3, sc_w[j], sc_b[j], h_spec, h_bias[j], tables)
            x = (proj_residual(x, a.reshape(nb * SEQ, D_A), bo.reshape(nb * SEQ, D_B), w_out_ab, j),)
        else:
            zz = in_proj(x, norm_mix[i], w_in_c, j, n=2 * D_C, mode="gelu")
            b_lanes = jnp.broadcast_to(b_s[j][:, :, None], (G_C, CHUNK, LANES))
            m = gmlp_gate(zz, v_gain[j], w_s[j], b_lanes)
            x = (proj_residual(x, m, m, w_out_c, j, a_col=0, b_col=1),)
        if i < DEPTH - 1:
            x = (mlp_residual(x[0], norm_mlp[i], w_mlp1, w_mlp2, i),)

    y_p, y_s = mlp_residual(x[0], norm_mlp[DEPTH - 1], w_mlp1, w_mlp2, DEPTH - 1,
                            out_rows=(nb_p * SEQ, nb_s * SEQ))
    return y_p.reshape(nb_p, SEQ, D_MODEL), y_s.reshape(nb_s, SEQ, D_MODEL)
```

```python
import functools
import math

import numpy as np
import jax
import jax.numpy as jnp
from jax import lax
from jax.experimental import pallas as pl
from jax.experimental.pallas import tpu as pltpu

F32 = jnp.float32
BF16 = jnp.bfloat16

D_MODEL = 2048
DEPTH = 4
SEQ = 4096
GRID_W = 64
WIN_H = 8
WIN_W = 16
DH = 128
H_A = 8
D_A = H_A * DH
D_B = 1024
EMB = 33
FO = 64
FAST_DECAY = 0.3
SLOW_DECAY = 1.5
DECAY_TARGET = 1e-2
D_C = D_MODEL
G_C = 8
DG_C = D_C // G_C
CHUNK = 128
D_FF = 4 * D_MODEL
EPS = 1e-6

LANES = 128
VMEM_LIMIT_BYTES = 52 * 1024 * 1024

N_FFT = 2 * SEQ
N_FAST = 128
N_SLOW = N_FFT // N_FAST
N_IN = SEQ // N_FAST
N_K1 = N_SLOW // 2 + 1


def _cparams(sem, vmem_limit_bytes=VMEM_LIMIT_BYTES):
    return pltpu.CompilerParams(dimension_semantics=sem, vmem_limit_bytes=vmem_limit_bytes)


def _tile_ranges(parts, tm):
    ranges, lo = [], 0
    for p in parts:
        ranges.append((lo, lo + p.shape[0] // tm))
        lo = ranges[-1][1]
    return ranges


def _part_spec(block, rng, col_fn):
    lo, hi = rng
    return pl.BlockSpec(block, lambda i, j: (jnp.clip(i - lo, 0, hi - lo - 1),
                                             jnp.where(_in_range(i, rng), col_fn(j), 0)))


def _in_range(i, rng):
    return (i >= rng[0]) & (i < rng[1])


def _rms_scale(x, g):
    return x * lax.rsqrt(jnp.mean(x * x, axis=-1, keepdims=True) + EPS) * g


def _gelu_tanh(x):
    return 0.5 * x * (1.0 + jnp.tanh(math.sqrt(2.0 / math.pi) * (x + 0.044715 * (x * x * x))))


def _split_bf16(x):
    hi = x.astype(BF16)
    lo = (x - hi.astype(F32)).astype(BF16)
    return hi, lo


def _dot3(a, b):
    ah, al = _split_bf16(a)
    bh, bl = _split_bf16(b)
    d = functools.partial(jnp.dot, preferred_element_type=F32)
    return d(ah, bh) + (d(ah, bl) + d(al, bh))


def _in_proj_kernel(*refs, mode, ranges):
    x_refs, refs = refs[:len(ranges)], refs[len(ranges):]
    if mode == "gelu":
        g_ref, w_ref, o_ref, h_ref = refs
    else:
        g_ref, w_ref, qg_ref, kg_ref, o_ref, h_ref = refs
    i, j = pl.program_id(0), pl.program_id(1)

    for x_ref, rng in zip(x_refs, ranges):
        @pl.when((j == 0) & _in_range(i, rng))
        def _():
            h_ref[...] = _rms_scale(x_ref[...], g_ref[...]).astype(BF16)

    acc = jnp.dot(h_ref[...], w_ref[...], preferred_element_type=F32)
    if mode == "gelu":
        o_ref[...] = _gelu_tanh(acc).astype(BF16)
    else:
        gain = jnp.where(j == 0, qg_ref[...] * (DH ** -0.5), kg_ref[...])
        for c in range(0, o_ref.shape[1], DH):
            seg = acc[:, c:c + DH]
            o_ref[:, c:c + DH] = jnp.where(j < 2, _rms_scale(seg, gain), seg).astype(BF16)


def in_proj(x_parts, g, w, layer, *, n, mode, qk_gains=(), tn=1024):
    d = x_parts[0].shape[1]
    tm = 1024
    t = sum(p.shape[0] for p in x_parts)
    x_bytes = len(x_parts) * 2 * tm * d * 4
    other_bytes = 2 * d * tn * 2 + 2 * tm * tn * 2 + tm * d * 2 + tm * tn * 4
    vmem_limit = max(VMEM_LIMIT_BYTES, x_bytes + other_bytes + (4 << 20))
    ranges = _tile_ranges(x_parts, tm)
    return pl.pallas_call(
        functools.partial(_in_proj_kernel, mode=mode, ranges=ranges),
        out_shape=jax.ShapeDtypeStruct((t, n), BF16),
        grid=(t // tm, n // tn),
        in_specs=[_part_spec((tm, d), rng, lambda j: 0) for rng in ranges] + [
            pl.BlockSpec((1, d), lambda i, j: (0, 0)),
            pl.BlockSpec((None, d, tn), lambda i, j: (layer, 0, j)),
        ] + [pl.BlockSpec((1, DH), lambda i, j: (0, 0))] * len(qk_gains),
        out_specs=pl.BlockSpec((tm, tn), lambda i, j: (i, j)),
        scratch_shapes=[pltpu.VMEM((tm, d), BF16)],
        compiler_params=_cparams(("parallel", "arbitrary"), vmem_limit),
        name="in_proj",
    )(*x_parts, g.reshape(1, d), w, *[gain.reshape(1, DH) for gain in qk_gains])


def _mlp_kernel(x_ref, g_ref, w1_ref, w2_ref, *refs, ranges):
    o_refs, h_ref = refs[:-1], refs[-1]

    def run(o_ref):
        @pl.when(pl.program_id(1) == 0)
        def _():
            x = x_ref[...]
            h_ref[...] = _rms_scale(x, g_ref[...]).astype(BF16)
            o_ref[...] = x

        a = jnp.dot(h_ref[...], w1_ref[...], preferred_element_type=F32)
        a = jnp.square(jnp.maximum(a, 0.0)).astype(BF16)
        o_ref[...] += jnp.dot(a, w2_ref[...], preferred_element_type=F32)

    if len(o_refs) == 1:
        run(o_refs[0])
    else:
        for o_ref, rng in zip(o_refs, ranges):
            pl.when(_in_range(pl.program_id(0), rng))(functools.partial(run, o_ref))


def mlp_residual(x, g, w1, w2, layer, *, out_rows=None, tm=512, tf=1024):
    t, d = x.shape
    f = w1.shape[2]
    out_rows = (t,) if out_rows is None else out_rows
    outs = [jax.ShapeDtypeStruct((r, d), F32) for r in out_rows]
    ranges = _tile_ranges(outs, tm)
    res = pl.pallas_call(
        functools.partial(_mlp_kernel, ranges=ranges),
        out_shape=outs,
        grid=(t // tm, f // tf),
        in_specs=[
            pl.BlockSpec((tm, d), lambda i, k: (i, 0)),
            pl.BlockSpec((1, d), lambda i, k: (0, 0)),
            pl.BlockSpec((None, d, tf), lambda i, k: (layer, 0, k)),
            pl.BlockSpec((None, tf, d), lambda i, k: (layer, k, 0)),
        ],
        out_specs=[_part_spec((tm, d), rng, lambda k: 0) for rng in ranges],
        scratch_shapes=[pltpu.VMEM((tm, d), BF16)],
        compiler_params=_cparams(("arbitrary", "arbitrary")),
        name="mlp_residual",
    )(x, g.reshape(1, d), w1, w2)
    return res[0] if len(res) == 1 else res


def _proj_residual_kernel(*refs, ranges):
    x_refs, (a_ref, b_ref, wa_ref, wb_ref, o_ref) = refs[:len(ranges)], refs[len(ranges):]
    acc = jnp.dot(a_ref[...], wa_ref[...], preferred_element_type=F32)
    acc += jnp.dot(b_ref[...], wb_ref[...], preferred_element_type=F32)
    x = x_refs[0][...]
    for x_ref, rng in zip(x_refs[1:], ranges[1:]):
        x = jnp.where(pl.program_id(0) >= rng[0], x_ref[...], x)
    o_ref[...] = x + acc


def proj_residual(x_parts, a, b, w, layer, *, a_col=0, b_col=0, tm=1024, tn=1024):
    n = x_parts[0].shape[1]
    t = sum(p.shape[0] for p in x_parts)
    ranges = _tile_ranges(x_parts, tm)
    k = w.shape[1] // 2
    return pl.pallas_call(
        functools.partial(_proj_residual_kernel, ranges=ranges),
        out_shape=jax.ShapeDtypeStruct((t, n), F32),
        grid=(t // tm, n // tn),
        in_specs=[_part_spec((tm, tn), rng, lambda j: j) for rng in ranges] + [
            pl.BlockSpec((tm, k), lambda i, j: (i, a_col)),
            pl.BlockSpec((tm, k), lambda i, j: (i, b_col)),
            pl.BlockSpec((None, k, tn), lambda i, j: (layer, 0, j)),
            pl.BlockSpec((None, k, tn), lambda i, j: (layer, 1, j)),
        ],
        out_specs=pl.BlockSpec((tm, tn), lambda i, j: (i, j)),
        compiler_params=_cparams(("parallel", "arbitrary")),
        name="proj_residual",
    )(*x_parts, a, b, w, w)


def _gate_kernel(zz_ref, vg_ref, ws_ref, bs_ref, o_ref, vn_ref, *, tm):
    for c in range(tm // CHUNK):
        rows = slice(c * CHUNK, (c + 1) * CHUNK)
        vn_ref[rows, :] = _rms_scale(zz_ref[rows, D_C:].astype(F32), vg_ref[...]).astype(BF16)
        for g in range(G_C):
            cols = slice(g * DG_C, (g + 1) * DG_C)
            s = jnp.dot(ws_ref[g], vn_ref[rows, cols], preferred_element_type=F32)
            bias = bs_ref[g]
            s = s + jnp.concatenate([bias] * (DG_C // LANES), axis=1)
            o_ref[rows, cols] = (zz_ref[rows, cols].astype(F32) * s).astype(BF16)


def gmlp_gate(zz, v_gain, w_s, b_s_lanes, *, tm=512):
    t = zz.shape[0]
    return pl.pallas_call(
        functools.partial(_gate_kernel, tm=tm),
        out_shape=jax.ShapeDtypeStruct((t, D_C), BF16),
        grid=(t // tm,),
        in_specs=[
            pl.BlockSpec((tm, 2 * D_C), lambda i: (i, 0)),
            pl.BlockSpec((1, D_C), lambda i: (0, 0)),
            pl.BlockSpec((G_C, CHUNK, CHUNK), lambda i: (0, 0, 0)),
            pl.BlockSpec((G_C, CHUNK, LANES), lambda i: (0, 0, 0)),
        ],
        out_specs=pl.BlockSpec((tm, D_C), lambda i: (i, 0)),
        scratch_shapes=[pltpu.VMEM((tm, D_C), BF16)],
        compiler_params=_cparams(("parallel",)),
        name="gmlp_gate",
    )(zz, v_gain.reshape(1, D_C), w_s, b_s_lanes)


N_ROWS = SEQ // GRID_W
GROUP_ROWS = 4
N_GROUPS = N_ROWS // GROUP_ROWS
KEY_ROWS = 12
NQ = GROUP_ROWS * GRID_W
NK = KEY_ROWS * GRID_W
SOFTMAX_ROWS = 64


def _natten_kernel(q_ref, k_ref, v_ref, bias_ref, o_ref, s_ref, p_ref):
    def q_rows(g):
        return pl.ds(pl.multiple_of(g * NQ, NQ), NQ)

    def k_rows(g):
        ws = jnp.clip(GROUP_ROWS * g - WIN_H // 2, 0, N_ROWS - KEY_ROWS)
        return pl.ds(pl.multiple_of(ws * GRID_W, GRID_W), NK)

    def scores(g, slot):
        s_ref[slot] = lax.dot_general(q_ref[0, q_rows(g), :], k_ref[0, k_rows(g), :],
                                      (((1,), (1,)), ((), ())), preferred_element_type=F32)

    def softmax(g, slot):
        kind = jnp.where(g == 0, 0, jnp.where(g == N_GROUPS - 1, 2, 1))
        for c in range(NQ // SOFTMAX_ROWS):
            rows = slice(c * SOFTMAX_ROWS, (c + 1) * SOFTMAX_ROWS)
            s = s_ref[slot, rows, :] + bias_ref[0, kind, rows, :]
            p = jnp.exp(s - jnp.max(s, axis=-1, keepdims=True))
            p = p * (1.0 / jnp.sum(p, axis=-1, keepdims=True))
            p_ref[slot, rows, :] = p.astype(BF16)

    def values(g, slot):
        o = jnp.dot(p_ref[slot], v_ref[0, k_rows(g), :], preferred_element_type=F32)
        o_ref[0, q_rows(g), :] = o.astype(BF16)

    scores(0, 0)
    scores(1, 1)
    softmax(0, 0)

    def body(i, carry):
        g = 2 * i + 1
        scores(g + 1, 0)
        softmax(g, 1)
        values(g - 1, 0)
        scores(g + 2, 1)
        softmax(g + 1, 0)
        values(g, 1)
        return carry

    lax.fori_loop(0, N_GROUPS // 2 - 1, body, 0)
    last = N_GROUPS - 1
    softmax(last, 1)
    values(last - 1, 0)
    values(last, 1)


def natten(qkv3, bias_tab):
    b = qkv3.shape[0]
    blk = (1, SEQ, DH)
    return pl.pallas_call(
        _natten_kernel,
        out_shape=jax.ShapeDtypeStruct((b, SEQ, D_A), BF16),
        grid=(H_A, b),
        in_specs=[
            pl.BlockSpec(blk, lambda h, i: (i, 0, h)),
            pl.BlockSpec(blk, lambda h, i: (i, 0, H_A + h)),
            pl.BlockSpec(blk, lambda h, i: (i, 0, 2 * H_A + h)),
            pl.BlockSpec((1, 3, NQ, NK), lambda h, i: (h, 0, 0, 0)),
        ],
        out_specs=pl.BlockSpec(blk, lambda h, i: (i, 0, h)),
        scratch_shapes=[pltpu.VMEM((2, NQ, NK), F32), pltpu.VMEM((2, NQ, NK), BF16)],
        compiler_params=_cparams(("parallel", "arbitrary")),
        name="natten",
    )(qkv3, qkv3, qkv3, bias_tab)


def natten_bias_table(rpb):
    col = np.arange(GRID_W)
    cs = np.clip(col - WIN_W // 2, 0, GRID_W - WIN_W)
    qc, kc = col[:, None], col[None, :]
    col_ok = (kc >= cs[:, None]) & (kc < cs[:, None] + WIN_W)
    dc = np.clip(kc - qc, -(WIN_W - 1), WIN_W - 1) + WIN_W - 1
    bias_c = rpb.astype(F32)[:, :, dc]
    a = np.arange(GROUP_ROWS)
    q_off = np.stack([a, a + WIN_H // 2, a + KEY_ROWS - GROUP_ROWS])
    w_off = np.stack([0 * a, a, 0 * a + KEY_ROWS - WIN_H])
    i = np.arange(KEY_ROWS)
    row_ok = (i >= w_off[..., None]) & (i < w_off[..., None] + WIN_H)
    dr = np.clip(i - q_off[..., None] + WIN_H - 1, 0, 2 * WIN_H - 2)
    tab = bias_c[:, dr]
    ok = row_ok[None, :, :, :, None, None] & col_ok[None, None, None, None]
    tab = jnp.where(ok, tab, -jnp.inf)
    return tab.transpose(0, 1, 2, 4, 3, 5).reshape(H_A, 3, NQ, NK)


FILT_ROWS = 512


def _filter_kernel(z_ref, t_ref, w1_ref, b1_ref, w2_ref, b2_ref, w3_ref, b3_ref, fr_ref,
                   wf_ref, wb_ref, dl_ref, hf_ref, hb_ref, h3_ref):
    @pl.when(pl.program_id(0) == 0)
    def _():
        def mlp(i, carry):
            rows = pl.ds(pl.multiple_of(i * FILT_ROWS, FILT_ROWS), FILT_ROWS)
            fr = fr_ref[...]
            h = jnp.sin(fr * (_dot3(z_ref[rows, :], w1_ref[...]) + b1_ref[...]))
            h = jnp.sin(fr * (_dot3(h, w2_ref[...]) + b2_ref[...]))
            h3_ref[rows, :] = jnp.sin(fr * (_dot3(h, w3_ref[...]) + b3_ref[...]))
            return carry

        lax.fori_loop(0, SEQ // FILT_ROWS, mlp, 0)

    def taps(i, carry):
        sf, sb = carry
        rows = pl.ds(pl.multiple_of(i * FILT_ROWS, FILT_ROWS), FILT_ROWS)
        h3 = h3_ref[rows, :]
        decay = jnp.exp(-t_ref[rows, :] * dl_ref[...])
        hf = _dot3(h3, wf_ref[...]) * decay
        hb = _dot3(h3, wb_ref[...]) * decay
        t_idx = lax.broadcasted_iota(jnp.int32, hb.shape, 0) + i * FILT_ROWS
        hb = jnp.where(t_idx == 0, 0.0, hb)
        hf_ref[rows, :] = hf
        hb_ref[rows, :] = hb
        return (sf + jnp.sum(jnp.abs(hf), axis=0, keepdims=True),
                sb + jnp.sum(jnp.abs(hb), axis=0, keepdims=True))

    zero = jnp.zeros((1, LANES), F32)
    sf, sb = lax.fori_loop(0, SEQ // FILT_ROWS, taps, (zero, zero))
    inv = 1.0 / (sf + sb)

    def scale(i, carry):
        rows = pl.ds(pl.multiple_of(i * FILT_ROWS, FILT_ROWS), FILT_ROWS)
        hf_ref[rows, :] = hf_ref[rows, :] * inv
        hb_ref[rows, :] = hb_ref[rows, :] * inv
        return carry

    lax.fori_loop(0, SEQ // FILT_ROWS, scale, 0)


def _filter_constants():
    t = np.linspace(0.0, 1.0, SEQ, dtype=np.float32).astype(np.float64)[:, None]
    bands = (EMB - 1) // 2
    w = 2.0 * math.pi * np.arange(SEQ, dtype=np.float64)[:, None] / SEQ
    f = np.linspace(1e-4, bands - 1, bands, dtype=np.float32).astype(np.float64)[None, :]
    z = np.concatenate([t, np.cos(f * w), -np.sin(f * w)], axis=-1)
    z_pad = np.zeros((SEQ, LANES), np.float32)
    z_pad[:, :EMB] = z
    t_lanes = np.broadcast_to(t.astype(np.float32), (SEQ, LANES)).copy()
    deltas = np.abs(np.linspace(math.log(DECAY_TARGET) / FAST_DECAY,
                                math.log(DECAY_TARGET) / SLOW_DECAY, D_B, dtype=np.float32))
    return z_pad, t_lanes, deltas.reshape(1, D_B)


def hyena_filter(f_w1, f_b1, f_w2, f_b2, f_w3, f_b3, f_wout, f_freq):
    z_pad, t_lanes, deltas = _filter_constants()
    w1_pad = jnp.zeros((LANES, FO), F32).at[:EMB].set(f_w1)
    nblk = D_B // LANES
    const = lambda shape: pl.BlockSpec(shape, lambda c: (0,) * len(shape))
    return pl.pallas_call(
        _filter_kernel,
        out_shape=[jax.ShapeDtypeStruct((SEQ, D_B), F32)] * 2,
        grid=(nblk,),
        in_specs=[
            const((SEQ, LANES)), const((SEQ, LANES)),
            const((LANES, FO)), const((1, FO)),
            const((FO, FO)), const((1, FO)),
            const((FO, FO)), const((1, FO)),
            const((1, FO)),
            pl.BlockSpec((FO, LANES), lambda c: (0, c)),
            pl.BlockSpec((FO, LANES), lambda c: (0, nblk + c)),
            pl.BlockSpec((1, LANES), lambda c: (0, c)),
        ],
        out_specs=[pl.BlockSpec((SEQ, LANES), lambda c: (0, c))] * 2,
        scratch_shapes=[pltpu.VMEM((SEQ, FO), F32)],
        compiler_params=_cparams(("arbitrary",)),
        name="hyena_filter",
    )(jnp.asarray(z_pad), jnp.asarray(t_lanes), w1_pad, f_b1.reshape(1, FO), f_w2,
      f_b2.reshape(1, FO), f_w3, f_b3.reshape(1, FO), f_freq.reshape(1, FO), f_wout, f_wout,
      jnp.asarray(deltas))


ROW_TILE = 8
N_K1_PAD = N_K1 + 1


@functools.lru_cache(maxsize=None)
def _dft_tables():
    k1 = np.arange(N_K1_PAD)
    n2 = np.arange(N_FAST)
    ang = 2.0 * math.pi * np.outer(k1, n2) / N_FFT
    tw = np.concatenate([np.cos(ang), -np.sin(ang)], axis=1)
    tw = np.broadcast_to(tw[:, :, None], (N_K1_PAD, 2 * N_FAST, LANES))
    ang2 = 2.0 * math.pi * np.outer(n2, n2) / N_FAST
    fr, fi = np.cos(ang2), -np.sin(ang2)
    f_fwd = np.block([[fr, -fi], [fi, fr]])
    f_inv = np.block([[fr, fi], [-fi, fr]])
    f32 = lambda a: np.ascontiguousarray(a, dtype=np.float32)
    return f32(tw), f32(f_fwd), f32(f_inv)


def _cmul_const(x, w):
    xr, xi = x
    c, s = w.real, w.imag
    eps = 1e-12
    if abs(s) < eps and abs(c - 1.0) < eps:
        return xr, xi
    if abs(c) < eps and abs(s - 1.0) < eps:
        return ("neg", xi), xr
    if abs(c) < eps and abs(s + 1.0) < eps:
        return xi, ("neg", xr)
    return xr * c - xi * s, xr * s + xi * c


def _is_neg(v):
    return isinstance(v, tuple) and v[0] == "neg"


def _plain(v):
    return -v[1] if _is_neg(v) else v


def _add(a, b):
    return a - b[1] if _is_neg(b) else a + b


def _sub(a, b):
    return a + b[1] if _is_neg(b) else a - b


def _fft_dit(xs, sign):
    n = len(xs)
    if n == 1:
        return xs
    ev = _fft_dit(xs[0::2], sign)
    od = _fft_dit(xs[1::2], sign)
    out = [None] * n
    for k in range(n // 2):
        w = complex(math.cos(2 * math.pi * k / n), sign * math.sin(2 * math.pi * k / n))
        tr, ti = _cmul_const(od[k], w)
        er, ei = ev[k]
        out[k] = (_add(er, tr), _add(ei, ti))
        out[k + n // 2] = (_sub(er, tr), _sub(ei, ti))
    return out


def _slow_forward(x):
    half, quarter = N_SLOW // 2, N_SLOW // 4
    z = [(x[2 * m], x[2 * m + 1]) for m in range(quarter)]
    zt = [z[0]]
    for m in range(1, quarter):
        tr, ti = _cmul_const(z[m], complex(math.cos(2 * math.pi * m / half), -math.sin(2 * math.pi * m / half)))
        zt.append((_plain(tr), _plain(ti)))
    ze = _fft_dit(z, -1)
    zo = _fft_dit(zt, -1)
    zc = [None] * half
    for j in range(quarter):
        zc[2 * j] = ze[j]
        zc[2 * j + 1] = zo[j]
    out = [None] * (half + 1)
    ar, ai = zc[0]
    out[0] = ((ar + ai) * 2.0, None)
    out[half] = ((ar - ai) * 2.0, None)
    ar, ai = zc[quarter]
    out[quarter] = (ar * 2.0, ai * -2.0)
    for k in range(1, quarter):
        ar, ai = zc[k]
        br, bi = zc[half - k]
        c, s = math.cos(2 * math.pi * k / N_SLOW), math.sin(2 * math.pi * k / N_SLOW)
        p, q, r, t = ar + br, ai - bi, ai + bi, ar - br
        u = r * c - t * s
        vn = t * (-c) - r * s
        out[k] = (p + u, q + vn)
        out[half - k] = (p - u, vn - q)
    return out


def _slow_inverse(xs):
    half, quarter = N_SLOW // 2, N_SLOW // 4
    zc = [None] * half
    ar, br = xs[0][0], xs[half][0]
    zc[0] = (ar + br, ar - br)
    ar, ai = xs[quarter]
    zc[quarter] = (ar * 2.0, ai * -2.0)
    for k in range(1, quarter):
        ar, ai = xs[k]
        br, bi = xs[half - k]
        c, s = math.cos(2 * math.pi * k / N_SLOW), math.sin(2 * math.pi * k / N_SLOW)
        p, q, t, r = ar + br, ai - bi, ar - br, ai + bi
        o_r = t * c - r * s
        o_i = t * s + r * c
        zc[k] = (p - o_i, q + o_r)
        zc[half - k] = (p + o_i, o_r - q)
    ze = _fft_dit(zc[0::2], +1)
    zo = _fft_dit(zc[1::2], +1)
    out = [None] * N_IN
    for m in range(quarter):
        tr, ti = _cmul_const(zo[m], complex(math.cos(2 * math.pi * m / half), math.sin(2 * math.pi * m / half)))
        out[2 * m] = _add(ze[m][0], tr)
        out[2 * m + 1] = _add(ze[m][1], ti)
    return out


def _slow_forward_pass(src_ref, a_ref):
    lanes = src_ref.shape[-1]
    zeros = jnp.zeros((ROW_TILE, lanes), F32)

    def body(j, carry):
        rows = pl.ds(pl.multiple_of(j * ROW_TILE, ROW_TILE), ROW_TILE)
        rows_im = pl.ds(pl.multiple_of(N_FAST + j * ROW_TILE, ROW_TILE), ROW_TILE)
        out = _slow_forward([src_ref[n1, rows, :] for n1 in range(N_IN)])
        for k1, (re, im) in enumerate(out):
            a_ref[k1, rows, :] = re
            a_ref[k1, rows_im, :] = zeros if im is None else im
        a_ref[N_K1, rows, :] = zeros
        a_ref[N_K1, rows_im, :] = zeros
        return carry

    lax.fori_loop(0, N_FAST // ROW_TILE, body, 0)


def _cmul(a, w):
    ar, ai, wr, wi = a[:N_FAST], a[N_FAST:], w[:N_FAST], w[N_FAST:]
    return ar * wr - ai * wi, ar * wi + ai * wr


def _cmul_conj(a, w):
    ar, ai, wr, wi = a[:N_FAST], a[N_FAST:], w[:N_FAST], w[N_FAST:]
    return ar * wr + ai * wi, ai * wr - ar * wi


def _spectrum_kernel(hf_ref, hb_ref, tw_ref, f_ref, o_ref, af_ref, ab_ref):
    _slow_forward_pass(hf_ref, af_ref)
    _slow_forward_pass(hb_ref, ab_ref)

    def body(k1, carry):
        tw = tw_ref[k1]
        f = f_ref[...]
        xf = _dot3(f, jnp.concatenate(_cmul(af_ref[k1], tw), axis=0))
        xb = _dot3(f, jnp.concatenate(_cmul(ab_ref[k1], tw), axis=0))
        re = xf[:N_FAST] + xb[:N_FAST]
        im = xf[N_FAST:] - xb[N_FAST:]
        o_ref[k1] = jnp.concatenate([re, im], axis=0) * (1.0 / (4 * N_FFT))
        return carry

    lax.fori_loop(0, N_K1_PAD, body, 0)


def filter_spectrum(hf, hb, tw, f_fwd):
    nblk = D_B // LANES
    slabs = lambda a: a.reshape(N_IN, N_FAST, D_B)
    whole = pl.BlockSpec(memory_space=pltpu.VMEM)
    spec_blk = (N_K1_PAD, 2 * N_FAST, LANES)
    return pl.pallas_call(
        _spectrum_kernel,
        out_shape=jax.ShapeDtypeStruct((N_K1_PAD, 2 * N_FAST, D_B), F32),
        grid=(nblk,),
        in_specs=[
            pl.BlockSpec((N_IN, N_FAST, LANES), lambda c: (0, 0, c)),
            pl.BlockSpec((N_IN, N_FAST, LANES), lambda c: (0, 0, c)),
            whole, whole,
        ],
        out_specs=pl.BlockSpec(spec_blk, lambda c: (0, 0, c)),
        scratch_shapes=[pltpu.VMEM(spec_blk, F32)] * 2,
        compiler_params=_cparams(("arbitrary",)),
        name="filter_spectrum",
    )(slabs(hf), slabs(hb), tw, f_fwd)


def _short_conv_slab(x_ref, w_ref, b_ref, n1):
    r0 = pl.multiple_of(n1 * N_FAST, N_FAST)
    edge = 16
    cur = x_ref[0, pl.ds(r0, N_FAST), :].astype(F32)
    before = x_ref[0, pl.ds(pl.multiple_of(jnp.maximum(r0 - edge, 0), edge), edge), :]
    after = x_ref[0, pl.ds(pl.multiple_of(jnp.minimum(r0 + N_FAST, SEQ - edge), edge), edge), :]
    before = jnp.where(n1 > 0, before.astype(F32)[edge - 1:edge], 0.0)
    after = jnp.where(n1 < N_IN - 1, after.astype(F32)[0:1], 0.0)
    row = lax.broadcasted_iota(jnp.int32, (ROW_TILE, cur.shape[1]), 0)
    prev = pltpu.roll(cur, 1, axis=0)
    prev = jnp.concatenate([jnp.where(row == 0, before, prev[:ROW_TILE]), prev[ROW_TILE:]], axis=0)
    nxt = pltpu.roll(cur, N_FAST - 1, axis=0)
    nxt = jnp.concatenate([nxt[:-ROW_TILE], jnp.where(row == ROW_TILE - 1, after, nxt[-ROW_TILE:])], axis=0)
    return prev * w_ref[0:1, :] + cur * w_ref[1:2, :] + nxt * w_ref[2:3, :] + b_ref[...]


def _hyena_kernel(x0_ref, x1_ref, v_ref, w0_ref, w1_ref, wv_ref, b0_ref, b1_ref, bv_ref,
                  hs_ref, hbias_ref, tw_ref, ff_ref, fi_ref, o_ref, u_ref, g0_ref, a_ref, z_ref):
    def conv(n1, carry):
        u_ref[n1] = _short_conv_slab(v_ref, wv_ref, bv_ref, n1) * _short_conv_slab(x1_ref, w1_ref, b1_ref, n1)
        g0_ref[n1] = _short_conv_slab(x0_ref, w0_ref, b0_ref, n1)
        return carry

    lax.fori_loop(0, N_IN, conv, 0)
    _slow_forward_pass(u_ref, a_ref)

    def freq(i, carry):
        ks = (2 * i, 2 * i + 1)
        tws = [tw_ref[k] for k in ks]
        a = [jnp.concatenate(_cmul(a_ref[k], tw), axis=0).astype(BF16) for k, tw in zip(ks, tws)]
        x = jnp.dot(ff_ref[...], jnp.concatenate(a, axis=1), preferred_element_type=F32)
        y = [jnp.concatenate(_cmul(x[:, j * LANES:(j + 1) * LANES], hs_ref[k]), axis=0).astype(BF16)
             for j, k in enumerate(ks)]
        z = jnp.dot(fi_ref[...], jnp.concatenate(y, axis=1), preferred_element_type=F32)
        for j, (k, tw) in enumerate(zip(ks, tws)):
            z_ref[k] = jnp.concatenate(_cmul_conj(z[:, j * LANES:(j + 1) * LANES], tw), axis=0)
        return carry

    lax.fori_loop(0, N_K1_PAD // 2, freq, 0, unroll=True)

    def back(j, carry):
        rows = pl.ds(pl.multiple_of(j * ROW_TILE, ROW_TILE), ROW_TILE)
        rows_im = pl.ds(pl.multiple_of(N_FAST + j * ROW_TILE, ROW_TILE), ROW_TILE)
        y = _slow_inverse([(z_ref[k1, rows, :], z_ref[k1, rows_im, :]) for k1 in range(N_K1)])
        for n1 in range(N_IN):
            u_ref[n1, rows, :] = y[n1] + u_ref[n1, rows, :] * hbias_ref[...]
        return carry

    lax.fori_loop(0, N_FAST // ROW_TILE, back, 0)

    def gate(n1, carry):
        rows = pl.ds(pl.multiple_of(n1 * N_FAST, N_FAST), N_FAST)
        o_ref[0, rows, :] = (u_ref[n1] * g0_ref[n1]).astype(BF16)
        return carry

    lax.fori_loop(0, N_IN, gate, 0)


def hyena(p3, sc_w, sc_b, h_spec, h_bias, tables):
    tw, f_fwd, f_inv = tables
    b = p3.shape[0]
    nblk = D_B // LANES
    base = 3 * D_A // LANES
    xblk = lambda off: pl.BlockSpec((1, SEQ, LANES), lambda c, i: (i, 0, base + off * nblk + c))
    wblk = lambda off: pl.BlockSpec((3, LANES), lambda c, i: (0, off * nblk + c))
    bblk = lambda off: pl.BlockSpec((1, LANES), lambda c, i: (0, off * nblk + c))
    whole = pl.BlockSpec(memory_space=pltpu.VMEM)
    sc_b2 = sc_b.reshape(1, 3 * D_B)
    spec_blk = (N_K1_PAD, 2 * N_FAST, LANES)
    return pl.pallas_call(
        _hyena_kernel,
        out_shape=jax.ShapeDtypeStruct((b, SEQ, D_B), BF16),
        grid=(nblk, b),
        in_specs=[
            xblk(0), xblk(1), xblk(2),
            wblk(0), wblk(1), wblk(2),
            bblk(0), bblk(1), bblk(2),
            pl.BlockSpec(spec_blk, lambda c, i: (0, 0, c)),
            pl.BlockSpec((1, LANES), lambda c, i: (0, c)),
            whole, whole, whole,
        ],
        out_specs=pl.BlockSpec((1, SEQ, LANES), lambda c, i: (i, 0, c)),
        scratch_shapes=[
            pltpu.VMEM((N_IN, N_FAST, LANES), F32),
            pltpu.VMEM((N_IN, N_FAST, LANES), F32),
            pltpu.VMEM(spec_blk, F32),
            pltpu.VMEM(spec_blk, F32),
        ],
        compiler_params=_cparams(("parallel", "arbitrary")),
        name="hyena",
    )(p3, p3, p3, sc_w, sc_w, sc_w, sc_b2, sc_b2, sc_b2,
      h_spec, h_bias.reshape(1, D_B), tw, f_fwd.astype(BF16), f_inv.astype(BF16))


def kernel(x_prompt, x_sample, norm_mix, norm_mlp, w_in_ab, sc_w, sc_b, q_gain, k_gain, rpb,
           f_w1, f_b1, f_w2, f_b2, f_w3, f_b3, f_wout, f_freq, h_bias, w_out_ab, w_in_c,
           v_gain, w_s, b_s, w_out_c, w_mlp1, w_mlp2):
    nb_p, nb_s = x_prompt.shape[0], x_sample.shape[0]
    nb = nb_p + nb_s
    x = (x_prompt.reshape(nb_p * SEQ, D_MODEL), x_sample.reshape(nb_s * SEQ, D_MODEL))
    tables = tuple(jnp.asarray(a) for a in _dft_tables())

    w_in_ab, w_out_ab, w_in_c, w_out_c, w_mlp1, w_mlp2, w_s = (
        w.astype(BF16) for w in (w_in_ab, w_out_ab, w_in_c, w_out_c, w_mlp1, w_mlp2, w_s))

    for i in range(DEPTH):
        j = i // 2
        if i % 2 == 0:
            p3 = in_proj(x, norm_mix[i], w_in_ab, j, n=3 * (D_A + D_B), mode="qk_norm",
                         qk_gains=(q_gain[j], k_gain[j])).reshape(nb, SEQ, 3 * (D_A + D_B))
            a = natten(p3, natten_bias_table(rpb[j]))
            hf, hb = hyena_filter(f_w1[j], f_b1[j], f_w2[j], f_b2[j], f_w3[j], f_b3[j],
                                  f_wout[j], f_freq[j])
            h_spec = filter_spectrum(hf, hb, tables[0], tables[1])
            bo = hyena(p3, sc_w[j], sc_b[j], h_spec, h_bias[j], tables)
            x = (proj_residual(x, a.reshape(nb * SEQ, D_A), bo.reshape(nb * SEQ, D_B), w_out_ab, j),)
        else:
            zz = in_proj(x, norm_mix[i], w_in_c, j, n=2 * D_C, mode="gelu")
            b_lanes = jnp.broadcast_to(b_s[j][:, :, None], (G_C, CHUNK, LANES))
            m = gmlp_gate(zz, v_gain[j], w_s[j], b_lanes)
            x = (proj_residual(x, m, m, w_out_c, j, a_col=0, b_col=1),)
        if i < DEPTH - 1:
            x = (mlp_residual(x[0], norm_mlp[i], w_mlp1, w_mlp2, i),)

    y_p, y_s = mlp_residual(x[0], norm_mlp[DEPTH - 1], w_mlp1, w_mlp2, DEPTH - 1,
                            out_rows=(nb_p * SEQ, nb_s * SEQ))
    return y_p.reshape(nb_p, SEQ, D_MODEL), y_s.reshape(nb_s, SEQ, D_MODEL)
```

```python
import functools
import math

import numpy as np
import jax
import jax.numpy as jnp
from jax import lax
from jax.experimental import pallas as pl
from jax.experimental.pallas import tpu as pltpu

F32 = jnp.float32
BF16 = jnp.bfloat16

D_MODEL = 2048
DEPTH = 4
SEQ = 4096
GRID_W = 64
WIN_H = 8
WIN_W = 16
DH = 128
H_A = 8
D_A = H_A * DH
D_B = 1024
EMB = 33
FO = 64
FAST_DECAY = 0.3
SLOW_DECAY = 1.5
DECAY_TARGET = 1e-2
D_C = D_MODEL
G_C = 8
DG_C = D_C // G_C
CHUNK = 128
D_FF = 4 * D_MODEL
EPS = 1e-6

LANES = 128
VMEM_LIMIT_BYTES = 52 * 1024 * 1024

N_FFT = 2 * SEQ
N_FAST = 128
N_SLOW = N_FFT // N_FAST
N_IN = SEQ // N_FAST
N_K1 = N_SLOW // 2 + 1


def _cparams(sem, vmem_limit_bytes=VMEM_LIMIT_BYTES):
    return pltpu.CompilerParams(dimension_semantics=sem, vmem_limit_bytes=vmem_limit_bytes)


def _tile_ranges(parts, tm):
    ranges, lo = [], 0
    for p in parts:
        ranges.append((lo, lo + p.shape[0] // tm))
        lo = ranges[-1][1]
    return ranges


def _part_spec(block, rng, col_fn):
    lo, hi = rng
    return pl.BlockSpec(block, lambda i, j: (jnp.clip(i - lo, 0, hi - lo - 1),
                                             jnp.where(_in_range(i, rng), col_fn(j), 0)))


def _in_range(i, rng):
    return (i >= rng[0]) & (i < rng[1])


def _rms_scale(x, g):
    return x * lax.rsqrt(jnp.mean(x * x, axis=-1, keepdims=True) + EPS) * g


def _gelu_tanh(x):
    return 0.5 * x * (1.0 + jnp.tanh(math.sqrt(2.0 / math.pi) * (x + 0.044715 * (x * x * x))))


def _split_bf16(x):
    hi = x.astype(BF16)
    lo = (x - hi.astype(F32)).astype(BF16)
    return hi, lo


def _dot3(a, b):
    ah, al = _split_bf16(a)
    bh, bl = _split_bf16(b)
    d = functools.partial(jnp.dot, preferred_element_type=F32)
    return d(ah, bh) + (d(ah, bl) + d(al, bh))


def _in_proj_kernel(*refs, mode, ranges):
    x_refs, refs = refs[:len(ranges)], refs[len(ranges):]
    if mode == "gelu":
        g_ref, w_ref, o_ref, h_ref = refs
    else:
        g_ref, w_ref, qg_ref, kg_ref, o_ref, h_ref = refs
    i, j = pl.program_id(0), pl.program_id(1)

    for x_ref, rng in zip(x_refs, ranges):
        @pl.when((j == 0) & _in_range(i, rng))
        def _():
            h_ref[...] = _rms_scale(x_ref[...], g_ref[...]).astype(BF16)

    acc = jnp.dot(h_ref[...], w_ref[...], preferred_element_type=F32)
    if mode == "gelu":
        o_ref[...] = _gelu_tanh(acc).astype(BF16)
    else:
        gain = jnp.where(j == 0, qg_ref[...] * (DH ** -0.5), kg_ref[...])
        for c in range(0, o_ref.shape[1], DH):
            seg = acc[:, c:c + DH]
            o_ref[:, c:c + DH] = jnp.where(j < 2, _rms_scale(seg, gain), seg).astype(BF16)


def in_proj(x_parts, g, w, layer, *, n, mode, qk_gains=(), tn=1024):
    d = x_parts[0].shape[1]
    tm = 1024
    t = sum(p.shape[0] for p in x_parts)
    x_bytes = len(x_parts) * 2 * tm * d * 4
    other_bytes = 2 * d * tn * 2 + 2 * tm * tn * 2 + tm * d * 2 + tm * tn * 4
    vmem_limit = max(VMEM_LIMIT_BYTES, x_bytes + other_bytes + (4 << 20))
    ranges = _tile_ranges(x_parts, tm)
    return pl.pallas_call(
        functools.partial(_in_proj_kernel, mode=mode, ranges=ranges),
        out_shape=jax.ShapeDtypeStruct((t, n), BF16),
        grid=(t // tm, n // tn),
        in_specs=[_part_spec((tm, d), rng, lambda j: 0) for rng in ranges] + [
            pl.BlockSpec((1, d), lambda i, j: (0, 0)),
            pl.BlockSpec((None, d, tn), lambda i, j: (layer, 0, j)),
        ] + [pl.BlockSpec((1, DH), lambda i, j: (0, 0))] * len(qk_gains),
        out_specs=pl.BlockSpec((tm, tn), lambda i, j: (i, j)),
        scratch_shapes=[pltpu.VMEM((tm, d), BF16)],
        compiler_params=_cparams(("parallel", "arbitrary"), vmem_limit),
        name="in_proj",
    )(*x_parts, g.reshape(1, d), w, *[gain.reshape(1, DH) for gain in qk_gains])


def _mlp_kernel(x_ref, g_ref, w1_ref, w2_ref, *refs, ranges):
    o_refs, h_ref = refs[:-1], refs[-1]

    def run(o_ref):
        @pl.when(pl.program_id(1) == 0)
        def _():
            x = x_ref[...]
            h_ref[...] = _rms_scale(x, g_ref[...]).astype(BF16)
            o_ref[...] = x

        a = jnp.dot(h_ref[...], w1_ref[...], preferred_element_type=F32)
        a = jnp.square(jnp.maximum(a, 0.0)).astype(BF16)
        o_ref[...] += jnp.dot(a, w2_ref[...], preferred_element_type=F32)

    if len(o_refs) == 1:
        run(o_refs[0])
    else:
        for o_ref, rng in zip(o_refs, ranges):
            pl.when(_in_range(pl.program_id(0), rng))(functools.partial(run, o_ref))


def mlp_residual(x, g, w1, w2, layer, *, out_rows=None):
    t, d = x.shape
    f = w1.shape[2]
    out_rows = (t,) if out_rows is None else out_rows
    tm, tf = (1024, 512) if len(out_rows) == 1 else (512, 1024)
    outs = [jax.ShapeDtypeStruct((r, d), F32) for r in out_rows]
    ranges = _tile_ranges(outs, tm)
    res = pl.pallas_call(
        functools.partial(_mlp_kernel, ranges=ranges),
        out_shape=outs,
        grid=(t // tm, f // tf),
        in_specs=[
            pl.BlockSpec((tm, d), lambda i, k: (i, 0)),
            pl.BlockSpec((1, d), lambda i, k: (0, 0)),
            pl.BlockSpec((None, d, tf), lambda i, k: (layer, 0, k)),
            pl.BlockSpec((None, tf, d), lambda i, k: (layer, k, 0)),
        ],
        out_specs=[_part_spec((tm, d), rng, lambda k: 0) for rng in ranges],
        scratch_shapes=[pltpu.VMEM((tm, d), BF16)],
        compiler_params=_cparams(("arbitrary", "arbitrary")),
        name="mlp_residual",
    )(x, g.reshape(1, d), w1, w2)
    return res[0] if len(res) == 1 else res


def _proj_residual_kernel(*refs, ranges):
    x_refs, (a_ref, b_ref, wa_ref, wb_ref, o_ref) = refs[:len(ranges)], refs[len(ranges):]
    acc = jnp.dot(a_ref[...], wa_ref[...], preferred_element_type=F32)
    acc += jnp.dot(b_ref[...], wb_ref[...], preferred_element_type=F32)
    x = x_refs[0][...]
    for x_ref, rng in zip(x_refs[1:], ranges[1:]):
        x = jnp.where(pl.program_id(0) >= rng[0], x_ref[...], x)
    o_ref[...] = x + acc


def proj_residual(x_parts, a, b, w, layer, *, a_col=0, b_col=0, tm=1024, tn=1024):
    n = x_parts[0].shape[1]
    t = sum(p.shape[0] for p in x_parts)
    ranges = _tile_ranges(x_parts, tm)
    k = w.shape[1] // 2
    return pl.pallas_call(
        functools.partial(_proj_residual_kernel, ranges=ranges),
        out_shape=jax.ShapeDtypeStruct((t, n), F32),
        grid=(t // tm, n // tn),
        in_specs=[_part_spec((tm, tn), rng, lambda j: j) for rng in ranges] + [
            pl.BlockSpec((tm, k), lambda i, j: (i, a_col)),
            pl.BlockSpec((tm, k), lambda i, j: (i, b_col)),
            pl.BlockSpec((None, k, tn), lambda i, j: (layer, 0, j)),
            pl.BlockSpec((None, k, tn), lambda i, j: (layer, 1, j)),
        ],
        out_specs=pl.BlockSpec((tm, tn), lambda i, j: (i, j)),
        compiler_params=_cparams(("parallel", "arbitrary")),
        name="proj_residual",
    )(*x_parts, a, b, w, w)


def _gate_kernel(zz_ref, vg_ref, ws_ref, bs_ref, o_ref, vn_ref, *, tm):
    for c in range(tm // CHUNK):
        rows = slice(c * CHUNK, (c + 1) * CHUNK)
        vn_ref[rows, :] = _rms_scale(zz_ref[rows, D_C:].astype(F32), vg_ref[...]).astype(BF16)
        for g in range(G_C):
            cols = slice(g * DG_C, (g + 1) * DG_C)
            s = jnp.dot(ws_ref[g], vn_ref[rows, cols], preferred_element_type=F32)
            bias = bs_ref[g]
            s = s + jnp.concatenate([bias] * (DG_C // LANES), axis=1)
            o_ref[rows, cols] = (zz_ref[rows, cols].astype(F32) * s).astype(BF16)


def gmlp_gate(zz, v_gain, w_s, b_s_lanes, *, tm=512):
    t = zz.shape[0]
    return pl.pallas_call(
        functools.partial(_gate_kernel, tm=tm),
        out_shape=jax.ShapeDtypeStruct((t, D_C), BF16),
        grid=(t // tm,),
        in_specs=[
            pl.BlockSpec((tm, 2 * D_C), lambda i: (i, 0)),
            pl.BlockSpec((1, D_C), lambda i: (0, 0)),
            pl.BlockSpec((G_C, CHUNK, CHUNK), lambda i: (0, 0, 0)),
            pl.BlockSpec((G_C, CHUNK, LANES), lambda i: (0, 0, 0)),
        ],
        out_specs=pl.BlockSpec((tm, D_C), lambda i: (i, 0)),
        scratch_shapes=[pltpu.VMEM((tm, D_C), BF16)],
        compiler_params=_cparams(("parallel",)),
        name="gmlp_gate",
    )(zz, v_gain.reshape(1, D_C), w_s, b_s_lanes)


N_ROWS = SEQ // GRID_W
GROUP_ROWS = 4
N_GROUPS = N_ROWS // GROUP_ROWS
KEY_ROWS = 12
NQ = GROUP_ROWS * GRID_W
NK = KEY_ROWS * GRID_W
SOFTMAX_ROWS = 64


def _natten_kernel(q_ref, k_ref, v_ref, bias_ref, o_ref, s_ref, p_ref):
    def q_rows(g):
        return pl.ds(g * NQ, NQ)

    def k_rows(g):
        ws = min(max(GROUP_ROWS * g - WIN_H // 2, 0), N_ROWS - KEY_ROWS)
        return pl.ds(ws * GRID_W, NK)

    def scores(g, slot):
        s_ref[slot] = lax.dot_general(q_ref[0, q_rows(g), :], k_ref[0, k_rows(g), :],
                                      (((1,), (1,)), ((), ())), preferred_element_type=F32)

    def softmax(g, slot):
        kind = 0 if g == 0 else (2 if g == N_GROUPS - 1 else 1)
        for c in range(NQ // SOFTMAX_ROWS):
            rows = slice(c * SOFTMAX_ROWS, (c + 1) * SOFTMAX_ROWS)
            s = s_ref[slot, rows, :] + bias_ref[0, kind, rows, :]
            p = jnp.exp(s - jnp.max(s, axis=-1, keepdims=True))
            p = p * (1.0 / jnp.sum(p, axis=-1, keepdims=True))
            p_ref[slot, rows, :] = p.astype(BF16)

    def values(g, slot):
        o = jnp.dot(p_ref[slot], v_ref[0, k_rows(g), :], preferred_element_type=F32)
        o_ref[0, q_rows(g), :] = o.astype(BF16)

    for t in range(N_GROUPS + 2):
        if t < N_GROUPS:
            scores(t, t % 2)
        if 1 <= t <= N_GROUPS:
            softmax(t - 1, (t - 1) % 2)
        if t >= 2:
            values(t - 2, t % 2)


def natten(qkv3, bias_tab):
    b = qkv3.shape[0]
    blk = (1, SEQ, DH)
    return pl.pallas_call(
        _natten_kernel,
        out_shape=jax.ShapeDtypeStruct((b, SEQ, D_A), BF16),
        grid=(H_A, b),
        in_specs=[
            pl.BlockSpec(blk, lambda h, i: (i, 0, h)),
            pl.BlockSpec(blk, lambda h, i: (i, 0, H_A + h)),
            pl.BlockSpec(blk, lambda h, i: (i, 0, 2 * H_A + h)),
            pl.BlockSpec((1, 3, NQ, NK), lambda h, i: (h, 0, 0, 0)),
        ],
        out_specs=pl.BlockSpec(blk, lambda h, i: (i, 0, h)),
        scratch_shapes=[pltpu.VMEM((2, NQ, NK), F32), pltpu.VMEM((2, NQ, NK), BF16)],
        compiler_params=_cparams(("parallel", "arbitrary")),
        name="natten",
    )(qkv3, qkv3, qkv3, bias_tab)


def natten_bias_table(rpb):
    col = np.arange(GRID_W)
    cs = np.clip(col - WIN_W // 2, 0, GRID_W - WIN_W)
    qc, kc = col[:, None], col[None, :]
    col_ok = (kc >= cs[:, None]) & (kc < cs[:, None] + WIN_W)
    dc = np.clip(kc - qc, -(WIN_W - 1), WIN_W - 1) + WIN_W - 1
    bias_c = rpb.astype(F32)[:, :, dc]
    a = np.arange(GROUP_ROWS)
    q_off = np.stack([a, a + WIN_H // 2, a + KEY_ROWS - GROUP_ROWS])
    w_off = np.stack([0 * a, a, 0 * a + KEY_ROWS - WIN_H])
    i = np.arange(KEY_ROWS)
    row_ok = (i >= w_off[..., None]) & (i < w_off[..., None] + WIN_H)
    dr = np.clip(i - q_off[..., None] + WIN_H - 1, 0, 2 * WIN_H - 2)
    tab = bias_c[:, dr]
    ok = row_ok[None, :, :, :, None, None] & col_ok[None, None, None, None]
    tab = jnp.where(ok, tab, -jnp.inf)
    return tab.transpose(0, 1, 2, 4, 3, 5).reshape(H_A, 3, NQ, NK)


FILT_ROWS = 512


def _filter_kernel(z_ref, t_ref, w1_ref, b1_ref, w2_ref, b2_ref, w3_ref, b3_ref, fr_ref,
                   wf_ref, wb_ref, dl_ref, hf_ref, hb_ref, h3_ref):
    @pl.when(pl.program_id(0) == 0)
    def _():
        def mlp(i, carry):
            rows = pl.ds(pl.multiple_of(i * FILT_ROWS, FILT_ROWS), FILT_ROWS)
            fr = fr_ref[...]
            h = jnp.sin(fr * (_dot3(z_ref[rows, :], w1_ref[...]) + b1_ref[...]))
            h = jnp.sin(fr * (_dot3(h, w2_ref[...]) + b2_ref[...]))
            h3_ref[rows, :] = jnp.sin(fr * (_dot3(h, w3_ref[...]) + b3_ref[...]))
            return carry

        lax.fori_loop(0, SEQ // FILT_ROWS, mlp, 0)

    def taps(i, carry):
        sf, sb = carry
        rows = pl.ds(pl.multiple_of(i * FILT_ROWS, FILT_ROWS), FILT_ROWS)
        h3 = h3_ref[rows, :]
        decay = jnp.exp(-t_ref[rows, :] * dl_ref[...])
        hf = _dot3(h3, wf_ref[...]) * decay
        hb = _dot3(h3, wb_ref[...]) * decay
        t_idx = lax.broadcasted_iota(jnp.int32, hb.shape, 0) + i * FILT_ROWS
        hb = jnp.where(t_idx == 0, 0.0, hb)
        hf_ref[rows, :] = hf
        hb_ref[rows, :] = hb
        return (sf + jnp.sum(jnp.abs(hf), axis=0, keepdims=True),
                sb + jnp.sum(jnp.abs(hb), axis=0, keepdims=True))

    zero = jnp.zeros((1, LANES), F32)
    sf, sb = lax.fori_loop(0, SEQ // FILT_ROWS, taps, (zero, zero))
    inv = 1.0 / (sf + sb)

    def scale(i, carry):
        rows = pl.ds(pl.multiple_of(i * FILT_ROWS, FILT_ROWS), FILT_ROWS)
        hf_ref[rows, :] = hf_ref[rows, :] * inv
        hb_ref[rows, :] = hb_ref[rows, :] * inv
        return carry

    lax.fori_loop(0, SEQ // FILT_ROWS, scale, 0)


def _filter_constants():
    t = np.linspace(0.0, 1.0, SEQ, dtype=np.float32).astype(np.float64)[:, None]
    bands = (EMB - 1) // 2
    w = 2.0 * math.pi * np.arange(SEQ, dtype=np.float64)[:, None] / SEQ
    f = np.linspace(1e-4, bands - 1, bands, dtype=np.float32).astype(np.float64)[None, :]
    z = np.concatenate([t, np.cos(f * w), -np.sin(f * w)], axis=-1)
    z_pad = np.zeros((SEQ, LANES), np.float32)
    z_pad[:, :EMB] = z
    t_lanes = np.broadcast_to(t.astype(np.float32), (SEQ, LANES)).copy()
    deltas = np.abs(np.linspace(math.log(DECAY_TARGET) / FAST_DECAY,
                                math.log(DECAY_TARGET) / SLOW_DECAY, D_B, dtype=np.float32))
    return z_pad, t_lanes, deltas.reshape(1, D_B)


def hyena_filter(f_w1, f_b1, f_w2, f_b2, f_w3, f_b3, f_wout, f_freq):
    z_pad, t_lanes, deltas = _filter_constants()
    w1_pad = jnp.zeros((LANES, FO), F32).at[:EMB].set(f_w1)
    nblk = D_B // LANES
    const = lambda shape: pl.BlockSpec(shape, lambda c: (0,) * len(shape))
    return pl.pallas_call(
        _filter_kernel,
        out_shape=[jax.ShapeDtypeStruct((SEQ, D_B), F32)] * 2,
        grid=(nblk,),
        in_specs=[
            const((SEQ, LANES)), const((SEQ, LANES)),
            const((LANES, FO)), const((1, FO)),
            const((FO, FO)), const((1, FO)),
            const((FO, FO)), const((1, FO)),
            const((1, FO)),
            pl.BlockSpec((FO, LANES), lambda c: (0, c)),
            pl.BlockSpec((FO, LANES), lambda c: (0, nblk + c)),
            pl.BlockSpec((1, LANES), lambda c: (0, c)),
        ],
        out_specs=[pl.BlockSpec((SEQ, LANES), lambda c: (0, c))] * 2,
        scratch_shapes=[pltpu.VMEM((SEQ, FO), F32)],
        compiler_params=_cparams(("arbitrary",)),
        name="hyena_filter",
    )(jnp.asarray(z_pad), jnp.asarray(t_lanes), w1_pad, f_b1.reshape(1, FO), f_w2,
      f_b2.reshape(1, FO), f_w3, f_b3.reshape(1, FO), f_freq.reshape(1, FO), f_wout, f_wout,
      jnp.asarray(deltas))


ROW_TILE = 8
N_K1_PAD = N_K1 + 1


@functools.lru_cache(maxsize=None)
def _dft_tables():
    k1 = np.arange(N_K1_PAD)
    n2 = np.arange(N_FAST)
    ang = 2.0 * math.pi * np.outer(k1, n2) / N_FFT
    tw = np.concatenate([np.cos(ang), -np.sin(ang)], axis=1)
    tw = np.broadcast_to(tw[:, :, None], (N_K1_PAD, 2 * N_FAST, LANES))
    ang2 = 2.0 * math.pi * np.outer(n2, n2) / N_FAST
    fr, fi = np.cos(ang2), -np.sin(ang2)
    f_fwd = np.block([[fr, -fi], [fi, fr]])
    f_inv = np.block([[fr, fi], [-fi, fr]])
    f32 = lambda a: np.ascontiguousarray(a, dtype=np.float32)
    return f32(tw), f32(f_fwd), f32(f_inv)


def _cmul_const(x, w):
    xr, xi = x
    c, s = w.real, w.imag
    eps = 1e-12
    if abs(s) < eps and abs(c - 1.0) < eps:
        return xr, xi
    if abs(c) < eps and abs(s - 1.0) < eps:
        return ("neg", xi), xr
    if abs(c) < eps and abs(s + 1.0) < eps:
        return xi, ("neg", xr)
    return xr * c - xi * s, xr * s + xi * c


def _is_neg(v):
    return isinstance(v, tuple) and v[0] == "neg"


def _plain(v):
    return -v[1] if _is_neg(v) else v


def _add(a, b):
    return a - b[1] if _is_neg(b) else a + b


def _sub(a, b):
    return a + b[1] if _is_neg(b) else a - b


def _fft_dit(xs, sign):
    n = len(xs)
    if n == 1:
        return xs
    ev = _fft_dit(xs[0::2], sign)
    od = _fft_dit(xs[1::2], sign)
    out = [None] * n
    for k in range(n // 2):
        w = complex(math.cos(2 * math.pi * k / n), sign * math.sin(2 * math.pi * k / n))
        tr, ti = _cmul_const(od[k], w)
        er, ei = ev[k]
        out[k] = (_add(er, tr), _add(ei, ti))
        out[k + n // 2] = (_sub(er, tr), _sub(ei, ti))
    return out


def _slow_forward(x):
    half, quarter = N_SLOW // 2, N_SLOW // 4
    z = [(x[2 * m], x[2 * m + 1]) for m in range(quarter)]
    zt = [z[0]]
    for m in range(1, quarter):
        tr, ti = _cmul_const(z[m], complex(math.cos(2 * math.pi * m / half), -math.sin(2 * math.pi * m / half)))
        zt.append((_plain(tr), _plain(ti)))
    ze = _fft_dit(z, -1)
    zo = _fft_dit(zt, -1)
    zc = [None] * half
    for j in range(quarter):
        zc[2 * j] = ze[j]
        zc[2 * j + 1] = zo[j]
    out = [None] * (half + 1)
    ar, ai = zc[0]
    out[0] = ((ar + ai) * 2.0, None)
    out[half] = ((ar - ai) * 2.0, None)
    ar, ai = zc[quarter]
    out[quarter] = (ar * 2.0, ai * -2.0)
    for k in range(1, quarter):
        ar, ai = zc[k]
        br, bi = zc[half - k]
        c, s = math.cos(2 * math.pi * k / N_SLOW), math.sin(2 * math.pi * k / N_SLOW)
        p, q, r, t = ar + br, ai - bi, ai + bi, ar - br
        u = r * c - t * s
        vn = t * (-c) - r * s
        out[k] = (p + u, q + vn)
        out[half - k] = (p - u, vn - q)
    return out


def _slow_inverse(xs):
    half, quarter = N_SLOW // 2, N_SLOW // 4
    zc = [None] * half
    ar, br = xs[0][0], xs[half][0]
    zc[0] = (ar + br, ar - br)
    ar, ai = xs[quarter]
    zc[quarter] = (ar * 2.0, ai * -2.0)
    for k in range(1, quarter):
        ar, ai = xs[k]
        br, bi = xs[half - k]
        c, s = math.cos(2 * math.pi * k / N_SLOW), math.sin(2 * math.pi * k / N_SLOW)
        p, q, t, r = ar + br, ai - bi, ar - br, ai + bi
        o_r = t * c - r * s
        o_i = t * s + r * c
        zc[k] = (p - o_i, q + o_r)
        zc[half - k] = (p + o_i, o_r - q)
    ze = _fft_dit(zc[0::2], +1)
    zo = _fft_dit(zc[1::2], +1)
    out = [None] * N_IN
    for m in range(quarter):
        tr, ti = _cmul_const(zo[m], complex(math.cos(2 * math.pi * m / half), math.sin(2 * math.pi * m / half)))
        out[2 * m] = _add(ze[m][0], tr)
        out[2 * m + 1] = _add(ze[m][1], ti)
    return out


def _slow_forward_pass(src_ref, a_ref):
    lanes = src_ref.shape[-1]
    zeros = jnp.zeros((ROW_TILE, lanes), F32)

    def body(j, carry):
        rows = pl.ds(pl.multiple_of(j * ROW_TILE, ROW_TILE), ROW_TILE)
        rows_im = pl.ds(pl.multiple_of(N_FAST + j * ROW_TILE, ROW_TILE), ROW_TILE)
        out = _slow_forward([src_ref[n1, rows, :] for n1 in range(N_IN)])
        for k1, (re, im) in enumerate(out):
            a_ref[k1, rows, :] = re
            a_ref[k1, rows_im, :] = zeros if im is None else im
        a_ref[N_K1, rows, :] = zeros
        a_ref[N_K1, rows_im, :] = zeros
        return carry

    lax.fori_loop(0, N_FAST // ROW_TILE, body, 0)


def _cmul(a, w):
    ar, ai, wr, wi = a[:N_FAST], a[N_FAST:], w[:N_FAST], w[N_FAST:]
    return ar * wr - ai * wi, ar * wi + ai * wr


def _cmul_conj(a, w):
    ar, ai, wr, wi = a[:N_FAST], a[N_FAST:], w[:N_FAST], w[N_FAST:]
    return ar * wr + ai * wi, ai * wr - ar * wi


def _spectrum_kernel(hf_ref, hb_ref, tw_ref, f_ref, o_ref, af_ref, ab_ref):
    _slow_forward_pass(hf_ref, af_ref)
    _slow_forward_pass(hb_ref, ab_ref)

    def body(k1, carry):
        tw = tw_ref[k1]
        f = f_ref[...]
        xf = _dot3(f, jnp.concatenate(_cmul(af_ref[k1], tw), axis=0))
        xb = _dot3(f, jnp.concatenate(_cmul(ab_ref[k1], tw), axis=0))
        re = xf[:N_FAST] + xb[:N_FAST]
        im = xf[N_FAST:] - xb[N_FAST:]
        o_ref[k1] = jnp.concatenate([re, im], axis=0) * (1.0 / (4 * N_FFT))
        return carry

    lax.fori_loop(0, N_K1_PAD, body, 0)


def filter_spectrum(hf, hb, tw, f_fwd):
    nblk = D_B // LANES
    slabs = lambda a: a.reshape(N_IN, N_FAST, D_B)
    whole = pl.BlockSpec(memory_space=pltpu.VMEM)
    spec_blk = (N_K1_PAD, 2 * N_FAST, LANES)
    return pl.pallas_call(
        _spectrum_kernel,
        out_shape=jax.ShapeDtypeStruct((N_K1_PAD, 2 * N_FAST, D_B), F32),
        grid=(nblk,),
        in_specs=[
            pl.BlockSpec((N_IN, N_FAST, LANES), lambda c: (0, 0, c)),
            pl.BlockSpec((N_IN, N_FAST, LANES), lambda c: (0, 0, c)),
            whole, whole,
        ],
        out_specs=pl.BlockSpec(spec_blk, lambda c: (0, 0, c)),
        scratch_shapes=[pltpu.VMEM(spec_blk, F32)] * 2,
        compiler_params=_cparams(("arbitrary",)),
        name="filter_spectrum",
    )(slabs(hf), slabs(hb), tw, f_fwd)


def _short_conv_slab(x_ref, w_ref, b_ref, n1):
    r0 = pl.multiple_of(n1 * N_FAST, N_FAST)
    edge = 16
    cur = x_ref[0, pl.ds(r0, N_FAST), :].astype(F32)
    before = x_ref[0, pl.ds(pl.multiple_of(jnp.maximum(r0 - edge, 0), edge), edge), :]
    after = x_ref[0, pl.ds(pl.multiple_of(jnp.minimum(r0 + N_FAST, SEQ - edge), edge), edge), :]
    before = jnp.where(n1 > 0, before.astype(F32)[edge - 1:edge], 0.0)
    after = jnp.where(n1 < N_IN - 1, after.astype(F32)[0:1], 0.0)
    row = lax.broadcasted_iota(jnp.int32, (ROW_TILE, cur.shape[1]), 0)
    prev = pltpu.roll(cur, 1, axis=0)
    prev = jnp.concatenate([jnp.where(row == 0, before, prev[:ROW_TILE]), prev[ROW_TILE:]], axis=0)
    nxt = pltpu.roll(cur, N_FAST - 1, axis=0)
    nxt = jnp.concatenate([nxt[:-ROW_TILE], jnp.where(row == ROW_TILE - 1, after, nxt[-ROW_TILE:])], axis=0)
    return prev * w_ref[0:1, :] + cur * w_ref[1:2, :] + nxt * w_ref[2:3, :] + b_ref[...]


def _hyena_kernel(x0_ref, x1_ref, v_ref, w0_ref, w1_ref, wv_ref, b0_ref, b1_ref, bv_ref,
                  hs_ref, hbias_ref, tw_ref, ff_ref, fi_ref, o_ref, u_ref, g0_ref, a_ref, z_ref):
    def conv(n1, carry):
        u_ref[n1] = _short_conv_slab(v_ref, wv_ref, bv_ref, n1) * _short_conv_slab(x1_ref, w1_ref, b1_ref, n1)
        g0_ref[n1] = _short_conv_slab(x0_ref, w0_ref, b0_ref, n1)
        return carry

    lax.fori_loop(0, N_IN, conv, 0)
    _slow_forward_pass(u_ref, a_ref)

    def freq(i, carry):
        ks = (2 * i, 2 * i + 1)
        tws = [tw_ref[k] for k in ks]
        a = [jnp.concatenate(_cmul(a_ref[k], tw), axis=0).astype(BF16) for k, tw in zip(ks, tws)]
        x = jnp.dot(ff_ref[...], jnp.concatenate(a, axis=1), preferred_element_type=F32)
        y = [jnp.concatenate(_cmul(x[:, j * LANES:(j + 1) * LANES], hs_ref[k]), axis=0).astype(BF16)
             for j, k in enumerate(ks)]
        z = jnp.dot(fi_ref[...], jnp.concatenate(y, axis=1), preferred_element_type=F32)
        for j, (k, tw) in enumerate(zip(ks, tws)):
            z_ref[k] = jnp.concatenate(_cmul_conj(z[:, j * LANES:(j + 1) * LANES], tw), axis=0)
        return carry

    lax.fori_loop(0, N_K1_PAD // 2, freq, 0, unroll=True)

    def back(j, carry):
        rows = pl.ds(pl.multiple_of(j * ROW_TILE, ROW_TILE), ROW_TILE)
        rows_im = pl.ds(pl.multiple_of(N_FAST + j * ROW_TILE, ROW_TILE), ROW_TILE)
        y = _slow_inverse([(z_ref[k1, rows, :], z_ref[k1, rows_im, :]) for k1 in range(N_K1)])
        for n1 in range(N_IN):
            u_ref[n1, rows, :] = y[n1] + u_ref[n1, rows, :] * hbias_ref[...]
        return carry

    lax.fori_loop(0, N_FAST // ROW_TILE, back, 0)

    def gate(n1, carry):
        rows = pl.ds(pl.multiple_of(n1 * N_FAST, N_FAST), N_FAST)
        o_ref[0, rows, :] = (u_ref[n1] * g0_ref[n1]).astype(BF16)
        return carry

    lax.fori_loop(0, N_IN, gate, 0)


def hyena(p3, sc_w, sc_b, h_spec, h_bias, tables):
    tw, f_fwd, f_inv = tables
    b = p3.shape[0]
    nblk = D_B // LANES
    base = 3 * D_A // LANES
    xblk = lambda off: pl.BlockSpec((1, SEQ, LANES), lambda c, i: (i, 0, base + off * nblk + c))
    wblk = lambda off: pl.BlockSpec((3, LANES), lambda c, i: (0, off * nblk + c))
    bblk = lambda off: pl.BlockSpec((1, LANES), lambda c, i: (0, off * nblk + c))
    whole = pl.BlockSpec(memory_space=pltpu.VMEM)
    sc_b2 = sc_b.reshape(1, 3 * D_B)
    spec_blk = (N_K1_PAD, 2 * N_FAST, LANES)
    return pl.pallas_call(
        _hyena_kernel,
        out_shape=jax.ShapeDtypeStruct((b, SEQ, D_B), BF16),
        grid=(nblk, b),
        in_specs=[
            xblk(0), xblk(1), xblk(2),
            wblk(0), wblk(1), wblk(2),
            bblk(0), bblk(1), bblk(2),
            pl.BlockSpec(spec_blk, lambda c, i: (0, 0, c)),
            pl.BlockSpec((1, LANES), lambda c, i: (0, c)),
            whole, whole, whole,
        ],
        out_specs=pl.BlockSpec((1, SEQ, LANES), lambda c, i: (i, 0, c)),
        scratch_shapes=[
            pltpu.VMEM((N_IN, N_FAST, LANES), F32),
            pltpu.VMEM((N_IN, N_FAST, LANES), F32),
            pltpu.VMEM(spec_blk, F32),
            pltpu.VMEM(spec_blk, F32),
        ],
        compiler_params=_cparams(("parallel", "arbitrary")),
        name="hyena",
    )(p3, p3, p3, sc_w, sc_w, sc_w, sc_b2, sc_b2, sc_b2,
      h_spec, h_bias.reshape(1, D_B), tw, f_fwd.astype(BF16), f_inv.astype(BF16))


def kernel(x_prompt, x_sample, norm_mix, norm_mlp, w_in_ab, sc_w, sc_b, q_gain, k_gain, rpb,
           f_w1, f_b1, f_w2, f_b2, f_w3, f_b3, f_wout, f_freq, h_bias, w_out_ab, w_in_c,
           v_gain, w_s, b_s, w_out_c, w_mlp1, w_mlp2):
    nb_p, nb_s = x_prompt.shape[0], x_sample.shape[0]
    nb = nb_p + nb_s
    x = (x_prompt.reshape(nb_p * SEQ, D_MODEL), x_sample.reshape(nb_s * SEQ, D_MODEL))
    tables = tuple(jnp.asarray(a) for a in _dft_tables())

    w_in_ab, w_out_ab, w_in_c, w_out_c, w_mlp1, w_mlp2, w_s = (
        w.astype(BF16) for w in (w_in_ab, w_out_ab, w_in_c, w_out_c, w_mlp1, w_mlp2, w_s))

    for i in range(DEPTH):
        j = i // 2
        if i % 2 == 0:
            p3 = in_proj(x, norm_mix[i], w_in_ab, j, n=3 * (D_A + D_B), mode="qk_norm",
                         qk_gains=(q_gain[j], k_gain[j])).reshape(nb, SEQ, 3 * (D_A + D_B))
            a = natten(p3, natten_bias_table(rpb[j]))
            hf, hb = hyena_filter(f_w1[j], f_b1[j], f_w2[j], f_b2[j], f_w3[j], f_b3[j],
                                  f_wout[j], f_freq[j])
            h_spec = filter_spectrum(hf, hb, tables[0], tables[1])
            bo = hyena(p3, sc_w[j], sc_b[j], h_spec, h_bias[j], tables)
            x = (proj_residual(x, a.reshape(nb * SEQ, D_A), bo.reshape(nb * SEQ, D_B), w_out_ab, j),)
        else:
            zz = in_proj(x, norm_mix[i], w_in_c, j, n=2 * D_C, mode="gelu")
            b_lanes = jnp.broadcast_to(b_s[j][:, :, None], (G_C, CHUNK, LANES))
            m = gmlp_gate(zz, v_gain[j], w_s[j], b_lanes)
            x = (proj_residual(x, m, m, w_out_c, j, a_col=0, b_col=1),)
        if i < DEPTH - 1:
            x = (mlp_residual(x[0], norm_mlp[i], w_mlp1, w_mlp2, i),)

    y_p, y_s = mlp_residual(x[0], norm_mlp[DEPTH - 1], w_mlp1, w_mlp2, DEPTH - 1,
                            out_rows=(nb_p * SEQ, nb_s * SEQ))
    return y_p.reshape(nb_p, SEQ, D_MODEL), y_s.reshape(nb_s, SEQ, D_MODEL)
```

```python
import functools
import math

import numpy as np
import jax
import jax.numpy as jnp
from jax import lax
from jax.experimental import pallas as pl
from jax.experimental.pallas import tpu as pltpu

F32 = jnp.float32
BF16 = jnp.bfloat16

D_MODEL = 2048
DEPTH = 4
SEQ = 4096
GRID_W = 64
WIN_H = 8
WIN_W = 16
DH = 128
H_A = 8
D_A = H_A * DH
D_B = 1024
EMB = 33
FO = 64
FAST_DECAY = 0.3
SLOW_DECAY = 1.5
DECAY_TARGET = 1e-2
D_C = D_MODEL
G_C = 8
DG_C = D_C // G_C
CHUNK = 128
D_FF = 4 * D_MODEL
EPS = 1e-6

LANES = 128
VMEM_LIMIT_BYTES = 52 * 1024 * 1024

N_FFT = 2 * SEQ
N_FAST = 128
N_SLOW = N_FFT // N_FAST
N_IN = SEQ // N_FAST
N_K1 = N_SLOW // 2 + 1


def _cparams(sem, vmem_limit_bytes=VMEM_LIMIT_BYTES):
    return pltpu.CompilerParams(dimension_semantics=sem, vmem_limit_bytes=vmem_limit_bytes)


def _tile_ranges(parts, tm):
    ranges, lo = [], 0
    for p in parts:
        ranges.append((lo, lo + p.shape[0] // tm))
        lo = ranges[-1][1]
    return ranges


def _part_spec(block, rng, col_fn, cols_outer=False):
    lo, hi = rng

    def index(i, j):
        return jnp.clip(i - lo, 0, hi - lo - 1), jnp.where(_in_range(i, rng), col_fn(j), 0)

    return pl.BlockSpec(block, (lambda j, i: index(i, j)) if cols_outer else index)


def _in_range(i, rng):
    return (i >= rng[0]) & (i < rng[1])


def _rms_scale(x, g):
    return x * lax.rsqrt(jnp.mean(x * x, axis=-1, keepdims=True) + EPS) * g


def _gelu_tanh(x):
    return 0.5 * x * (1.0 + jnp.tanh(math.sqrt(2.0 / math.pi) * (x + 0.044715 * (x * x * x))))


def _split_bf16(x):
    hi = x.astype(BF16)
    lo = (x - hi.astype(F32)).astype(BF16)
    return hi, lo


def _dot3(a, b):
    ah, al = _split_bf16(a)
    bh, bl = _split_bf16(b)
    d = functools.partial(jnp.dot, preferred_element_type=F32)
    return d(ah, bh) + (d(ah, bl) + d(al, bh))


def _in_proj_kernel(*refs, mode, ranges):
    x_refs, refs = refs[:len(ranges)], refs[len(ranges):]
    if mode == "gelu":
        g_ref, w_ref, o_ref, h_ref = refs
    else:
        g_ref, w_ref, qg_ref, kg_ref, o_ref, h_ref = refs
    i, j = pl.program_id(0), pl.program_id(1)

    for x_ref, rng in zip(x_refs, ranges):
        @pl.when((j == 0) & _in_range(i, rng))
        def _():
            h_ref[...] = _rms_scale(x_ref[...], g_ref[...]).astype(BF16)

    acc = jnp.dot(h_ref[...], w_ref[...], preferred_element_type=F32)
    if mode == "gelu":
        o_ref[...] = _gelu_tanh(acc).astype(BF16)
    else:
        gain = jnp.where(j == 0, qg_ref[...] * (DH ** -0.5), kg_ref[...])
        for c in range(0, o_ref.shape[1], DH):
            seg = acc[:, c:c + DH]
            o_ref[:, c:c + DH] = jnp.where(j < 2, _rms_scale(seg, gain), seg).astype(BF16)


def in_proj(x_parts, g, w, layer, *, n, mode, qk_gains=(), tn=1024):
    d = x_parts[0].shape[1]
    tm = 1024
    t = sum(p.shape[0] for p in x_parts)
    x_bytes = len(x_parts) * 2 * tm * d * 4
    other_bytes = 2 * d * tn * 2 + 2 * tm * tn * 2 + tm * d * 2 + tm * tn * 4
    vmem_limit = max(VMEM_LIMIT_BYTES, x_bytes + other_bytes + (4 << 20))
    ranges = _tile_ranges(x_parts, tm)
    return pl.pallas_call(
        functools.partial(_in_proj_kernel, mode=mode, ranges=ranges),
        out_shape=jax.ShapeDtypeStruct((t, n), BF16),
        grid=(t // tm, n // tn),
        in_specs=[_part_spec((tm, d), rng, lambda j: 0) for rng in ranges] + [
            pl.BlockSpec((1, d), lambda i, j: (0, 0)),
            pl.BlockSpec((None, d, tn), lambda i, j: (layer, 0, j)),
        ] + [pl.BlockSpec((1, DH), lambda i, j: (0, 0))] * len(qk_gains),
        out_specs=pl.BlockSpec((tm, tn), lambda i, j: (i, j)),
        scratch_shapes=[pltpu.VMEM((tm, d), BF16)],
        compiler_params=_cparams(("parallel", "arbitrary"), vmem_limit),
        name="in_proj",
    )(*x_parts, g.reshape(1, d), w, *[gain.reshape(1, DH) for gain in qk_gains])


def _mlp_kernel(x_ref, g_ref, w1_ref, w2_ref, *refs, ranges):
    o_refs, h_ref = refs[:-1], refs[-1]

    def run(o_ref):
        @pl.when(pl.program_id(1) == 0)
        def _():
            x = x_ref[...]
            h_ref[...] = _rms_scale(x, g_ref[...]).astype(BF16)
            o_ref[...] = x

        a = jnp.dot(h_ref[...], w1_ref[...], preferred_element_type=F32)
        a = jnp.square(jnp.maximum(a, 0.0)).astype(BF16)
        o_ref[...] += jnp.dot(a, w2_ref[...], preferred_element_type=F32)

    if len(o_refs) == 1:
        run(o_refs[0])
    else:
        for o_ref, rng in zip(o_refs, ranges):
            pl.when(_in_range(pl.program_id(0), rng))(functools.partial(run, o_ref))


def mlp_residual(x, g, w1, w2, layer, *, out_rows=None):
    t, d = x.shape
    f = w1.shape[2]
    out_rows = (t,) if out_rows is None else out_rows
    tm, tf = 512, 1024
    outs = [jax.ShapeDtypeStruct((r, d), F32) for r in out_rows]
    ranges = _tile_ranges(outs, tm)
    res = pl.pallas_call(
        functools.partial(_mlp_kernel, ranges=ranges),
        out_shape=outs,
        grid=(t // tm, f // tf),
        in_specs=[
            pl.BlockSpec((tm, d), lambda i, k: (i, 0)),
            pl.BlockSpec((1, d), lambda i, k: (0, 0)),
            pl.BlockSpec((None, d, tf), lambda i, k: (layer, 0, k)),
            pl.BlockSpec((None, tf, d), lambda i, k: (layer, k, 0)),
        ],
        out_specs=[_part_spec((tm, d), rng, lambda k: 0) for rng in ranges],
        scratch_shapes=[pltpu.VMEM((tm, d), BF16)],
        compiler_params=_cparams(("arbitrary", "arbitrary")),
        name="mlp_residual",
    )(x, g.reshape(1, d), w1, w2)
    return res[0] if len(res) == 1 else res


def _proj_residual_kernel(*refs, ranges):
    x_refs, (a_ref, b_ref, wa_ref, wb_ref, o_ref) = refs[:len(ranges)], refs[len(ranges):]
    acc = jnp.dot(a_ref[...], wa_ref[...], preferred_element_type=F32)
    acc += jnp.dot(b_ref[...], wb_ref[...], preferred_element_type=F32)
    x = x_refs[0][...]
    for x_ref, rng in zip(x_refs[1:], ranges[1:]):
        x = jnp.where(pl.program_id(1) >= rng[0], x_ref[...], x)
    o_ref[...] = x + acc


def proj_residual(x_parts, a, b, w, layer, *, a_col=0, b_col=0, tm=1024, tn=1024):
    n = x_parts[0].shape[1]
    t = sum(p.shape[0] for p in x_parts)
    ranges = _tile_ranges(x_parts, tm)
    k = w.shape[1] // 2
    return pl.pallas_call(
        functools.partial(_proj_residual_kernel, ranges=ranges),
        out_shape=jax.ShapeDtypeStruct((t, n), F32),
        grid=(n // tn, t // tm),
        in_specs=[_part_spec((tm, tn), rng, lambda j: j, cols_outer=True) for rng in ranges] + [
            pl.BlockSpec((tm, k), lambda j, i: (i, a_col)),
            pl.BlockSpec((tm, k), lambda j, i: (i, b_col)),
            pl.BlockSpec((None, k, tn), lambda j, i: (layer, 0, j)),
            pl.BlockSpec((None, k, tn), lambda j, i: (layer, 1, j)),
        ],
        out_specs=pl.BlockSpec((tm, tn), lambda j, i: (i, j)),
        compiler_params=_cparams(("arbitrary", "arbitrary")),
        name="proj_residual",
    )(*x_parts, a, b, w, w)


def _gate_kernel(zz_ref, vg_ref, ws_ref, bs_ref, o_ref, vn_ref, *, tm):
    for c in range(tm // CHUNK):
        rows = slice(c * CHUNK, (c + 1) * CHUNK)
        vn_ref[rows, :] = _rms_scale(zz_ref[rows, D_C:].astype(F32), vg_ref[...]).astype(BF16)
        for g in range(G_C):
            cols = slice(g * DG_C, (g + 1) * DG_C)
            s = jnp.dot(ws_ref[g], vn_ref[rows, cols], preferred_element_type=F32)
            bias = bs_ref[g]
            s = s + jnp.concatenate([bias] * (DG_C // LANES), axis=1)
            o_ref[rows, cols] = (zz_ref[rows, cols].astype(F32) * s).astype(BF16)


def gmlp_gate(zz, v_gain, w_s, b_s_lanes, *, tm=512):
    t = zz.shape[0]
    return pl.pallas_call(
        functools.partial(_gate_kernel, tm=tm),
        out_shape=jax.ShapeDtypeStruct((t, D_C), BF16),
        grid=(t // tm,),
        in_specs=[
            pl.BlockSpec((tm, 2 * D_C), lambda i: (i, 0)),
            pl.BlockSpec((1, D_C), lambda i: (0, 0)),
            pl.BlockSpec((G_C, CHUNK, CHUNK), lambda i: (0, 0, 0)),
            pl.BlockSpec((G_C, CHUNK, LANES), lambda i: (0, 0, 0)),
        ],
        out_specs=pl.BlockSpec((tm, D_C), lambda i: (i, 0)),
        scratch_shapes=[pltpu.VMEM((tm, D_C), BF16)],
        compiler_params=_cparams(("parallel",)),
        name="gmlp_gate",
    )(zz, v_gain.reshape(1, D_C), w_s, b_s_lanes)


N_ROWS = SEQ // GRID_W
GROUP_ROWS = 4
N_GROUPS = N_ROWS // GROUP_ROWS
KEY_ROWS = 12
NQ = GROUP_ROWS * GRID_W
NK = KEY_ROWS * GRID_W
SOFTMAX_ROWS = 64


def _natten_kernel(q_ref, k_ref, v_ref, bias_ref, o_ref, s_ref, p_ref):
    def q_rows(g):
        return pl.ds(g * NQ, NQ)

    def k_rows(g):
        ws = min(max(GROUP_ROWS * g - WIN_H // 2, 0), N_ROWS - KEY_ROWS)
        return pl.ds(ws * GRID_W, NK)

    def scores(g, slot):
        s_ref[slot] = lax.dot_general(q_ref[0, q_rows(g), :], k_ref[0, k_rows(g), :],
                                      (((1,), (1,)), ((), ())), preferred_element_type=F32)

    def softmax(g, slot):
        kind = 0 if g == 0 else (2 if g == N_GROUPS - 1 else 1)
        for c in range(NQ // SOFTMAX_ROWS):
            rows = slice(c * SOFTMAX_ROWS, (c + 1) * SOFTMAX_ROWS)
            s = s_ref[slot, rows, :] + bias_ref[0, kind, rows, :]
            p = jnp.exp(s - jnp.max(s, axis=-1, keepdims=True))
            p = p * (1.0 / jnp.sum(p, axis=-1, keepdims=True))
            p_ref[slot, rows, :] = p.astype(BF16)

    def values(g, slot):
        o = jnp.dot(p_ref[slot], v_ref[0, k_rows(g), :], preferred_element_type=F32)
        o_ref[0, q_rows(g), :] = o.astype(BF16)

    for t in range(N_GROUPS + 2):
        if t < N_GROUPS:
            scores(t, t % 2)
        if 1 <= t <= N_GROUPS:
            softmax(t - 1, (t - 1) % 2)
        if t >= 2:
            values(t - 2, t % 2)


def natten(qkv3, bias_tab):
    b = qkv3.shape[0]
    blk = (1, SEQ, DH)
    return pl.pallas_call(
        _natten_kernel,
        out_shape=jax.ShapeDtypeStruct((b, SEQ, D_A), BF16),
        grid=(H_A, b),
        in_specs=[
            pl.BlockSpec(blk, lambda h, i: (i, 0, h)),
            pl.BlockSpec(blk, lambda h, i: (i, 0, H_A + h)),
            pl.BlockSpec(blk, lambda h, i: (i, 0, 2 * H_A + h)),
            pl.BlockSpec((1, 3, NQ, NK), lambda h, i: (h, 0, 0, 0)),
        ],
        out_specs=pl.BlockSpec(blk, lambda h, i: (i, 0, h)),
        scratch_shapes=[pltpu.VMEM((2, NQ, NK), F32), pltpu.VMEM((2, NQ, NK), BF16)],
        compiler_params=_cparams(("parallel", "arbitrary")),
        name="natten",
    )(qkv3, qkv3, qkv3, bias_tab)


def natten_bias_table(rpb):
    col = np.arange(GRID_W)
    cs = np.clip(col - WIN_W // 2, 0, GRID_W - WIN_W)
    qc, kc = col[:, None], col[None, :]
    col_ok = (kc >= cs[:, None]) & (kc < cs[:, None] + WIN_W)
    dc = np.clip(kc - qc, -(WIN_W - 1), WIN_W - 1) + WIN_W - 1
    bias_c = rpb.astype(F32)[:, :, dc]
    a = np.arange(GROUP_ROWS)
    q_off = np.stack([a, a + WIN_H // 2, a + KEY_ROWS - GROUP_ROWS])
    w_off = np.stack([0 * a, a, 0 * a + KEY_ROWS - WIN_H])
    i = np.arange(KEY_ROWS)
    row_ok = (i >= w_off[..., None]) & (i < w_off[..., None] + WIN_H)
    dr = np.clip(i - q_off[..., None] + WIN_H - 1, 0, 2 * WIN_H - 2)
    tab = bias_c[:, dr]
    ok = row_ok[None, :, :, :, None, None] & col_ok[None, None, None, None]
    tab = jnp.where(ok, tab, -jnp.inf)
    return tab.transpose(0, 1, 2, 4, 3, 5).reshape(H_A, 3, NQ, NK)


FILT_ROWS = 512


def _filter_kernel(z_ref, t_ref, w1_ref, b1_ref, w2_ref, b2_ref, w3_ref, b3_ref, fr_ref,
                   wf_ref, wb_ref, dl_ref, hf_ref, hb_ref, h3_ref):
    @pl.when(pl.program_id(0) == 0)
    def _():
        def mlp(i, carry):
            rows = pl.ds(pl.multiple_of(i * FILT_ROWS, FILT_ROWS), FILT_ROWS)
            fr = fr_ref[...]
            h = jnp.sin(fr * (_dot3(z_ref[rows, :], w1_ref[...]) + b1_ref[...]))
            h = jnp.sin(fr * (_dot3(h, w2_ref[...]) + b2_ref[...]))
            h3_ref[rows, :] = jnp.sin(fr * (_dot3(h, w3_ref[...]) + b3_ref[...]))
            return carry

        lax.fori_loop(0, SEQ // FILT_ROWS, mlp, 0)

    def taps(i, carry):
        sf, sb = carry
        rows = pl.ds(pl.multiple_of(i * FILT_ROWS, FILT_ROWS), FILT_ROWS)
        h3 = h3_ref[rows, :]
        decay = jnp.exp(-t_ref[rows, :] * dl_ref[...])
        hf = _dot3(h3, wf_ref[...]) * decay
        hb = _dot3(h3, wb_ref[...]) * decay
        t_idx = lax.broadcasted_iota(jnp.int32, hb.shape, 0) + i * FILT_ROWS
        hb = jnp.where(t_idx == 0, 0.0, hb)
        hf_ref[rows, :] = hf
        hb_ref[rows, :] = hb
        return (sf + jnp.sum(jnp.abs(hf), axis=0, keepdims=True),
                sb + jnp.sum(jnp.abs(hb), axis=0, keepdims=True))

    zero = jnp.zeros((1, LANES), F32)
    sf, sb = lax.fori_loop(0, SEQ // FILT_ROWS, taps, (zero, zero))
    inv = 1.0 / (sf + sb)

    def scale(i, carry):
        rows = pl.ds(pl.multiple_of(i * FILT_ROWS, FILT_ROWS), FILT_ROWS)
        hf_ref[rows, :] = hf_ref[rows, :] * inv
        hb_ref[rows, :] = hb_ref[rows, :] * inv
        return carry

    lax.fori_loop(0, SEQ // FILT_ROWS, scale, 0)


def _filter_constants():
    t = np.linspace(0.0, 1.0, SEQ, dtype=np.float32).astype(np.float64)[:, None]
    bands = (EMB - 1) // 2
    w = 2.0 * math.pi * np.arange(SEQ, dtype=np.float64)[:, None] / SEQ
    f = np.linspace(1e-4, bands - 1, bands, dtype=np.float32).astype(np.float64)[None, :]
    z = np.concatenate([t, np.cos(f * w), -np.sin(f * w)], axis=-1)
    z_pad = np.zeros((SEQ, LANES), np.float32)
    z_pad[:, :EMB] = z
    t_lanes = np.broadcast_to(t.astype(np.float32), (SEQ, LANES)).copy()
    deltas = np.abs(np.linspace(math.log(DECAY_TARGET) / FAST_DECAY,
                                math.log(DECAY_TARGET) / SLOW_DECAY, D_B, dtype=np.float32))
    return z_pad, t_lanes, deltas.reshape(1, D_B)


def hyena_filter(f_w1, f_b1, f_w2, f_b2, f_w3, f_b3, f_wout, f_freq):
    z_pad, t_lanes, deltas = _filter_constants()
    w1_pad = jnp.zeros((LANES, FO), F32).at[:EMB].set(f_w1)
    nblk = D_B // LANES
    const = lambda shape: pl.BlockSpec(shape, lambda c: (0,) * len(shape))
    return pl.pallas_call(
        _filter_kernel,
        out_shape=[jax.ShapeDtypeStruct((SEQ, D_B), F32)] * 2,
        grid=(nblk,),
        in_specs=[
            const((SEQ, LANES)), const((SEQ, LANES)),
            const((LANES, FO)), const((1, FO)),
            const((FO, FO)), const((1, FO)),
            const((FO, FO)), const((1, FO)),
            const((1, FO)),
            pl.BlockSpec((FO, LANES), lambda c: (0, c)),
            pl.BlockSpec((FO, LANES), lambda c: (0, nblk + c)),
            pl.BlockSpec((1, LANES), lambda c: (0, c)),
        ],
        out_specs=[pl.BlockSpec((SEQ, LANES), lambda c: (0, c))] * 2,
        scratch_shapes=[pltpu.VMEM((SEQ, FO), F32)],
        compiler_params=_cparams(("arbitrary",)),
        name="hyena_filter",
    )(jnp.asarray(z_pad), jnp.asarray(t_lanes), w1_pad, f_b1.reshape(1, FO), f_w2,
      f_b2.reshape(1, FO), f_w3, f_b3.reshape(1, FO), f_freq.reshape(1, FO), f_wout, f_wout,
      jnp.asarray(deltas))


ROW_TILE = 8
N_K1_PAD = N_K1 + 1


@functools.lru_cache(maxsize=None)
def _dft_tables():
    k1 = np.arange(N_K1_PAD)
    n2 = np.arange(N_FAST)
    ang = 2.0 * math.pi * np.outer(k1, n2) / N_FFT
    tw = np.concatenate([np.cos(ang), -np.sin(ang)], axis=1)
    tw = np.broadcast_to(tw[:, :, None], (N_K1_PAD, 2 * N_FAST, LANES))
    ang2 = 2.0 * math.pi * np.outer(n2, n2) / N_FAST
    fr, fi = np.cos(ang2), -np.sin(ang2)
    f_fwd = np.block([[fr, -fi], [fi, fr]])
    f_inv = np.block([[fr, fi], [-fi, fr]])
    f32 = lambda a: np.ascontiguousarray(a, dtype=np.float32)
    return f32(tw), f32(f_fwd), f32(f_inv)


def _cmul_const(x, w):
    xr, xi = x
    c, s = w.real, w.imag
    eps = 1e-12
    if abs(s) < eps and abs(c - 1.0) < eps:
        return xr, xi
    if abs(c) < eps and abs(s - 1.0) < eps:
        return ("neg", xi), xr
    if abs(c) < eps and abs(s + 1.0) < eps:
        return xi, ("neg", xr)
    return xr * c - xi * s, xr * s + xi * c


def _is_neg(v):
    return isinstance(v, tuple) and v[0] == "neg"


def _plain(v):
    return -v[1] if _is_neg(v) else v


def _add(a, b):
    return a - b[1] if _is_neg(b) else a + b


def _sub(a, b):
    return a + b[1] if _is_neg(b) else a - b


def _fft_dit(xs, sign):
    n = len(xs)
    if n == 1:
        return xs
    ev = _fft_dit(xs[0::2], sign)
    od = _fft_dit(xs[1::2], sign)
    out = [None] * n
    for k in range(n // 2):
        w = complex(math.cos(2 * math.pi * k / n), sign * math.sin(2 * math.pi * k / n))
        tr, ti = _cmul_const(od[k], w)
        er, ei = ev[k]
        out[k] = (_add(er, tr), _add(ei, ti))
        out[k + n // 2] = (_sub(er, tr), _sub(ei, ti))
    return out


def _slow_forward(x):
    half, quarter = N_SLOW // 2, N_SLOW // 4
    z = [(x[2 * m], x[2 * m + 1]) for m in range(quarter)]
    zt = [z[0]]
    for m in range(1, quarter):
        tr, ti = _cmul_const(z[m], complex(math.cos(2 * math.pi * m / half), -math.sin(2 * math.pi * m / half)))
        zt.append((_plain(tr), _plain(ti)))
    ze = _fft_dit(z, -1)
    zo = _fft_dit(zt, -1)
    zc = [None] * half
    for j in range(quarter):
        zc[2 * j] = ze[j]
        zc[2 * j + 1] = zo[j]
    out = [None] * (half + 1)
    ar, ai = zc[0]
    out[0] = ((ar + ai) * 2.0, None)
    out[half] = ((ar - ai) * 2.0, None)
    ar, ai = zc[quarter]
    out[quarter] = (ar * 2.0, ai * -2.0)
    for k in range(1, quarter):
        ar, ai = zc[k]
        br, bi = zc[half - k]
        c, s = math.cos(2 * math.pi * k / N_SLOW), math.sin(2 * math.pi * k / N_SLOW)
        p, q, r, t = ar + br, ai - bi, ai + bi, ar - br
        u = r * c - t * s
        vn = t * (-c) - r * s
        out[k] = (p + u, q + vn)
        out[half - k] = (p - u, vn - q)
    return out


def _slow_inverse(xs):
    half, quarter = N_SLOW // 2, N_SLOW // 4
    zc = [None] * half
    ar, br = xs[0][0], xs[half][0]
    zc[0] = (ar + br, ar - br)
    ar, ai = xs[quarter]
    zc[quarter] = (ar * 2.0, ai * -2.0)
    for k in range(1, quarter):
        ar, ai = xs[k]
        br, bi = xs[half - k]
        c, s = math.cos(2 * math.pi * k / N_SLOW), math.sin(2 * math.pi * k / N_SLOW)
        p, q, t, r = ar + br, ai - bi, ar - br, ai + bi
        o_r = t * c - r * s
        o_i = t * s + r * c
        zc[k] = (p - o_i, q + o_r)
        zc[half - k] = (p + o_i, o_r - q)
    ze = _fft_dit(zc[0::2], +1)
    zo = _fft_dit(zc[1::2], +1)
    out = [None] * N_IN
    for m in range(quarter):
        tr, ti = _cmul_const(zo[m], complex(math.cos(2 * math.pi * m / half), math.sin(2 * math.pi * m / half)))
        out[2 * m] = _add(ze[m][0], tr)
        out[2 * m + 1] = _add(ze[m][1], ti)
    return out


def _slow_forward_pass(src_ref, a_ref):
    lanes = src_ref.shape[-1]
    zeros = jnp.zeros((ROW_TILE, lanes), F32)

    def body(j, carry):
        rows = pl.ds(pl.multiple_of(j * ROW_TILE, ROW_TILE), ROW_TILE)
        rows_im = pl.ds(pl.multiple_of(N_FAST + j * ROW_TILE, ROW_TILE), ROW_TILE)
        out = _slow_forward([src_ref[n1, rows, :] for n1 in range(N_IN)])
        for k1, (re, im) in enumerate(out):
            a_ref[k1, rows, :] = re
            a_ref[k1, rows_im, :] = zeros if im is None else im
        a_ref[N_K1, rows, :] = zeros
        a_ref[N_K1, rows_im, :] = zeros
        return carry

    lax.fori_loop(0, N_FAST // ROW_TILE, body, 0)


def _cmul(a, w):
    ar, ai, wr, wi = a[:N_FAST], a[N_FAST:], w[:N_FAST], w[N_FAST:]
    return ar * wr - ai * wi, ar * wi + ai * wr


def _cmul_conj(a, w):
    ar, ai, wr, wi = a[:N_FAST], a[N_FAST:], w[:N_FAST], w[N_FAST:]
    return ar * wr + ai * wi, ai * wr - ar * wi


def _spectrum_kernel(hf_ref, hb_ref, tw_ref, f_ref, o_ref, af_ref, ab_ref):
    _slow_forward_pass(hf_ref, af_ref)
    _slow_forward_pass(hb_ref, ab_ref)

    def body(k1, carry):
        tw = tw_ref[k1]
        f = f_ref[...]
        both = jnp.concatenate([jnp.concatenate(_cmul(af_ref[k1], tw), axis=0),
                                jnp.concatenate(_cmul(ab_ref[k1], tw), axis=0)], axis=1)
        x = _dot3(f, both)
        xf, xb = x[:, :LANES], x[:, LANES:]
        re = xf[:N_FAST] + xb[:N_FAST]
        im = xf[N_FAST:] - xb[N_FAST:]
        o_ref[k1] = jnp.concatenate([re, im], axis=0) * (1.0 / (4 * N_FFT))
        return carry

    lax.fori_loop(0, N_K1_PAD, body, 0)


def filter_spectrum(hf, hb, tw, f_fwd):
    nblk = D_B // LANES
    slabs = lambda a: a.reshape(N_IN, N_FAST, D_B)
    whole = pl.BlockSpec(memory_space=pltpu.VMEM)
    spec_blk = (N_K1_PAD, 2 * N_FAST, LANES)
    return pl.pallas_call(
        _spectrum_kernel,
        out_shape=jax.ShapeDtypeStruct((N_K1_PAD, 2 * N_FAST, D_B), F32),
        grid=(nblk,),
        in_specs=[
            pl.BlockSpec((N_IN, N_FAST, LANES), lambda c: (0, 0, c)),
            pl.BlockSpec((N_IN, N_FAST, LANES), lambda c: (0, 0, c)),
            whole, whole,
        ],
        out_specs=pl.BlockSpec(spec_blk, lambda c: (0, 0, c)),
        scratch_shapes=[pltpu.VMEM(spec_blk, F32)] * 2,
        compiler_params=_cparams(("arbitrary",)),
        name="filter_spectrum",
    )(slabs(hf), slabs(hb), tw, f_fwd)


def _short_conv_slab(x_ref, w_ref, b_ref, n1):
    r0 = pl.multiple_of(n1 * N_FAST, N_FAST)
    edge = 16
    cur = x_ref[0, pl.ds(r0, N_FAST), :].astype(F32)
    before = x_ref[0, pl.ds(pl.multiple_of(jnp.maximum(r0 - edge, 0), edge), edge), :]
    after = x_ref[0, pl.ds(pl.multiple_of(jnp.minimum(r0 + N_FAST, SEQ - edge), edge), edge), :]
    before = jnp.where(n1 > 0, before.astype(F32)[edge - 1:edge], 0.0)
    after = jnp.where(n1 < N_IN - 1, after.astype(F32)[0:1], 0.0)
    row = lax.broadcasted_iota(jnp.int32, (ROW_TILE, cur.shape[1]), 0)
    prev = pltpu.roll(cur, 1, axis=0)
    prev = jnp.concatenate([jnp.where(row == 0, before, prev[:ROW_TILE]), prev[ROW_TILE:]], axis=0)
    nxt = pltpu.roll(cur, N_FAST - 1, axis=0)
    nxt = jnp.concatenate([nxt[:-ROW_TILE], jnp.where(row == ROW_TILE - 1, after, nxt[-ROW_TILE:])], axis=0)
    return prev * w_ref[0:1, :] + cur * w_ref[1:2, :] + nxt * w_ref[2:3, :] + b_ref[...]


def _hyena_kernel(x0_ref, x1_ref, v_ref, w0_ref, w1_ref, wv_ref, b0_ref, b1_ref, bv_ref,
                  hs_ref, hbias_ref, tw_ref, ff_ref, fi_ref, o_ref, u_ref, g0_ref, a_ref, z_ref):
    def conv(n1, carry):
        u_ref[n1] = _short_conv_slab(v_ref, wv_ref, bv_ref, n1) * _short_conv_slab(x1_ref, w1_ref, b1_ref, n1)
        g0_ref[n1] = _short_conv_slab(x0_ref, w0_ref, b0_ref, n1)
        return carry

    lax.fori_loop(0, N_IN, conv, 0)
    _slow_forward_pass(u_ref, a_ref)

    def freq(i, carry):
        ks = (2 * i, 2 * i + 1)
        tws = [tw_ref[k] for k in ks]
        a = [jnp.concatenate(_cmul(a_ref[k], tw), axis=0).astype(BF16) for k, tw in zip(ks, tws)]
        x = jnp.dot(ff_ref[...], jnp.concatenate(a, axis=1), preferred_element_type=F32)
        y = [jnp.concatenate(_cmul(x[:, j * LANES:(j + 1) * LANES], hs_ref[k]), axis=0).astype(BF16)
             for j, k in enumerate(ks)]
        z = jnp.dot(fi_ref[...], jnp.concatenate(y, axis=1), preferred_element_type=F32)
        for j, (k, tw) in enumerate(zip(ks, tws)):
            z_ref[k] = jnp.concatenate(_cmul_conj(z[:, j * LANES:(j + 1) * LANES], tw), axis=0)
        return carry

    lax.fori_loop(0, N_K1_PAD // 2, freq, 0, unroll=True)

    def back(j, carry):
        rows = pl.ds(pl.multiple_of(j * ROW_TILE, ROW_TILE), ROW_TILE)
        rows_im = pl.ds(pl.multiple_of(N_FAST + j * ROW_TILE, ROW_TILE), ROW_TILE)
        y = _slow_inverse([(z_ref[k1, rows, :], z_ref[k1, rows_im, :]) for k1 in range(N_K1)])
        for n1 in range(N_IN):
            u_ref[n1, rows, :] = y[n1] + u_ref[n1, rows, :] * hbias_ref[...]
        return carry

    lax.fori_loop(0, N_FAST // ROW_TILE, back, 0)

    def gate(n1, carry):
        rows = pl.ds(pl.multiple_of(n1 * N_FAST, N_FAST), N_FAST)
        o_ref[0, rows, :] = (u_ref[n1] * g0_ref[n1]).astype(BF16)
        return carry

    lax.fori_loop(0, N_IN, gate, 0)


def hyena(p3, sc_w, sc_b, h_spec, h_bias, tables):
    tw, f_fwd, f_inv = tables
    b = p3.shape[0]
    nblk = D_B // LANES
    base = 3 * D_A // LANES
    xblk = lambda off: pl.BlockSpec((1, SEQ, LANES), lambda c, i: (i, 0, base + off * nblk + c))
    wblk = lambda off: pl.BlockSpec((3, LANES), lambda c, i: (0, off * nblk + c))
    bblk = lambda off: pl.BlockSpec((1, LANES), lambda c, i: (0, off * nblk + c))
    whole = pl.BlockSpec(memory_space=pltpu.VMEM)
    sc_b2 = sc_b.reshape(1, 3 * D_B)
    spec_blk = (N_K1_PAD, 2 * N_FAST, LANES)
    return pl.pallas_call(
        _hyena_kernel,
        out_shape=jax.ShapeDtypeStruct((b, SEQ, D_B), BF16),
        grid=(nblk, b),
        in_specs=[
            xblk(0), xblk(1), xblk(2),
            wblk(0), wblk(1), wblk(2),
            bblk(0), bblk(1), bblk(2),
            pl.BlockSpec(spec_blk, lambda c, i: (0, 0, c)),
            pl.BlockSpec((1, LANES), lambda c, i: (0, c)),
            whole, whole, whole,
        ],
        out_specs=pl.BlockSpec((1, SEQ, LANES), lambda c, i: (i, 0, c)),
        scratch_shapes=[
            pltpu.VMEM((N_IN, N_FAST, LANES), F32),
            pltpu.VMEM((N_IN, N_FAST, LANES), F32),
            pltpu.VMEM(spec_blk, F32),
            pltpu.VMEM(spec_blk, F32),
        ],
        compiler_params=_cparams(("parallel", "arbitrary")),
        name="hyena",
    )(p3, p3, p3, sc_w, sc_w, sc_w, sc_b2, sc_b2, sc_b2,
      h_spec, h_bias.reshape(1, D_B), tw, f_fwd.astype(BF16), f_inv.astype(BF16))


def kernel(x_prompt, x_sample, norm_mix, norm_mlp, w_in_ab, sc_w, sc_b, q_gain, k_gain, rpb,
           f_w1, f_b1, f_w2, f_b2, f_w3, f_b3, f_wout, f_freq, h_bias, w_out_ab, w_in_c,
           v_gain, w_s, b_s, w_out_c, w_mlp1, w_mlp2):
    nb_p, nb_s = x_prompt.shape[0], x_sample.shape[0]
    nb = nb_p + nb_s
    x = (x_prompt.reshape(nb_p * SEQ, D_MODEL), x_sample.reshape(nb_s * SEQ, D_MODEL))
    tables = tuple(jnp.asarray(a) for a in _dft_tables())

    w_in_ab, w_out_ab, w_in_c, w_out_c, w_mlp1, w_mlp2, w_s = (
        w.astype(BF16) for w in (w_in_ab, w_out_ab, w_in_c, w_out_c, w_mlp1, w_mlp2, w_s))

    for i in range(DEPTH):
        j = i // 2
        if i % 2 == 0:
            p3 = in_proj(x, norm_mix[i], w_in_ab, j, n=3 * (D_A + D_B), mode="qk_norm",
                         qk_gains=(q_gain[j], k_gain[j])).reshape(nb, SEQ, 3 * (D_A + D_B))
            a = natten(p3, natten_bias_table(rpb[j]))
            hf, hb = hyena_filter(f_w1[j], f_b1[j], f_w2[j], f_b2[j], f_w3[j], f_b3[j],
                                  f_wout[j], f_freq[j])
            h_spec = filter_spectrum(hf, hb, tables[0], tables[1])
            bo = hyena(p3, sc_w[j], sc_b[j], h_spec, h_bias[j], tables)
            x = (proj_residual(x, a.reshape(nb * SEQ, D_A), bo.reshape(nb * SEQ, D_B), w_out_ab, j),)
        else:
            zz = in_proj(x, norm_mix[i], w_in_c, j, n=2 * D_C, mode="gelu")
            b_lanes = jnp.broadcast_to(b_s[j][:, :, None], (G_C, CHUNK, LANES))
            m = gmlp_gate(zz, v_gain[j], w_s[j], b_lanes)
            x = (proj_residual(x, m, m, w_out_c, j, a_col=0, b_col=1),)
        if i < DEPTH - 1:
            x = (mlp_residual(x[0], norm_mlp[i], w_mlp1, w_mlp2, i),)

    y_p, y_s = mlp_residual(x[0], norm_mlp[DEPTH - 1], w_mlp1, w_mlp2, DEPTH - 1,
                            out_rows=(nb_p * SEQ, nb_s * SEQ))
    return y_p.reshape(nb_p, SEQ, D_MODEL), y_s.reshape(nb_s, SEQ, D_MODEL)
```

```python
import functools
import math

import numpy as np
import jax
import jax.numpy as jnp
from jax import lax
from jax.experimental import pallas as pl
from jax.experimental.pallas import tpu as pltpu

F32 = jnp.float32
BF16 = jnp.bfloat16

D_MODEL = 2048
DEPTH = 4
SEQ = 4096
GRID_W = 64
WIN_H = 8
WIN_W = 16
DH = 128
H_A = 8
D_A = H_A * DH
D_B = 1024
EMB = 33
FO = 64
FAST_DECAY = 0.3
SLOW_DECAY = 1.5
DECAY_TARGET = 1e-2
D_C = D_MODEL
G_C = 8
DG_C = D_C // G_C
CHUNK = 128
D_FF = 4 * D_MODEL
EPS = 1e-6

LANES = 128
VMEM_LIMIT_BYTES = 52 * 1024 * 1024

N_FFT = 2 * SEQ
N_FAST = 128
N_SLOW = N_FFT // N_FAST
N_IN = SEQ // N_FAST
N_K1 = N_SLOW // 2 + 1


def _cparams(sem, vmem_limit_bytes=VMEM_LIMIT_BYTES):
    return pltpu.CompilerParams(dimension_semantics=sem, vmem_limit_bytes=vmem_limit_bytes)


def _tile_ranges(parts, tm):
    ranges, lo = [], 0
    for p in parts:
        ranges.append((lo, lo + p.shape[0] // tm))
        lo = ranges[-1][1]
    return ranges


def _part_spec(block, rng, col_fn, cols_outer=False):
    lo, hi = rng

    def index(i, j):
        return jnp.clip(i - lo, 0, hi - lo - 1), jnp.where(_in_range(i, rng), col_fn(j), 0)

    return pl.BlockSpec(block, (lambda j, i: index(i, j)) if cols_outer else index)


def _in_range(i, rng):
    return (i >= rng[0]) & (i < rng[1])


def _rms_scale(x, g):
    return x * lax.rsqrt(jnp.mean(x * x, axis=-1, keepdims=True) + EPS) * g


def _gelu_tanh(x):
    return 0.5 * x * (1.0 + jnp.tanh(math.sqrt(2.0 / math.pi) * (x + 0.044715 * (x * x * x))))


def _split_bf16(x):
    hi = x.astype(BF16)
    lo = (x - hi.astype(F32)).astype(BF16)
    return hi, lo


def _dot3(a, b):
    ah, al = _split_bf16(a)
    bh, bl = _split_bf16(b)
    d = functools.partial(jnp.dot, preferred_element_type=F32)
    return d(ah, bh) + (d(ah, bl) + d(al, bh))


def _in_proj_kernel(*refs, mode, ranges):
    x_refs, refs = refs[:len(ranges)], refs[len(ranges):]
    if mode == "gelu":
        g_ref, w_ref, o_ref, h_ref = refs
    else:
        g_ref, w_ref, qg_ref, kg_ref, o_ref, h_ref = refs
    i, j = pl.program_id(0), pl.program_id(1)

    for x_ref, rng in zip(x_refs, ranges):
        @pl.when((j == 0) & _in_range(i, rng))
        def _():
            h_ref[...] = _rms_scale(x_ref[...], g_ref[...]).astype(BF16)

    acc = jnp.dot(h_ref[...], w_ref[...], preferred_element_type=F32)
    if mode == "gelu":
        o_ref[...] = _gelu_tanh(acc).astype(BF16)
    else:
        gain = jnp.where(j == 0, qg_ref[...] * (DH ** -0.5), kg_ref[...])
        for c in range(0, o_ref.shape[1], DH):
            seg = acc[:, c:c + DH]
            o_ref[:, c:c + DH] = jnp.where(j < 2, _rms_scale(seg, gain), seg).astype(BF16)


def in_proj(x_parts, g, w, layer, *, n, mode, qk_gains=(), tn=1024):
    d = x_parts[0].shape[1]
    tm = 1024
    t = sum(p.shape[0] for p in x_parts)
    x_bytes = len(x_parts) * 2 * tm * d * 4
    other_bytes = 2 * d * tn * 2 + 2 * tm * tn * 2 + tm * d * 2 + tm * tn * 4
    vmem_limit = max(VMEM_LIMIT_BYTES, x_bytes + other_bytes + (4 << 20))
    ranges = _tile_ranges(x_parts, tm)
    return pl.pallas_call(
        functools.partial(_in_proj_kernel, mode=mode, ranges=ranges),
        out_shape=jax.ShapeDtypeStruct((t, n), BF16),
        grid=(t // tm, n // tn),
        in_specs=[_part_spec((tm, d), rng, lambda j: 0) for rng in ranges] + [
            pl.BlockSpec((1, d), lambda i, j: (0, 0)),
            pl.BlockSpec((None, d, tn), lambda i, j: (layer, 0, j)),
        ] + [pl.BlockSpec((1, DH), lambda i, j: (0, 0))] * len(qk_gains),
        out_specs=pl.BlockSpec((tm, tn), lambda i, j: (i, j)),
        scratch_shapes=[pltpu.VMEM((tm, d), BF16)],
        compiler_params=_cparams(("parallel", "arbitrary"), vmem_limit),
        name="in_proj",
    )(*x_parts, g.reshape(1, d), w, *[gain.reshape(1, DH) for gain in qk_gains])


def _mlp_kernel(x_ref, g_ref, w1_ref, w2_ref, *refs, ranges):
    o_refs, h_ref = refs[:-1], refs[-1]

    def run(o_ref):
        @pl.when(pl.program_id(1) == 0)
        def _():
            x = x_ref[...]
            h_ref[...] = _rms_scale(x, g_ref[...]).astype(BF16)
            o_ref[...] = x

        a = jnp.dot(h_ref[...], w1_ref[...], preferred_element_type=F32)
        a = jnp.square(jnp.maximum(a, 0.0)).astype(BF16)
        o_ref[...] += jnp.dot(a, w2_ref[...], preferred_element_type=F32)

    if len(o_refs) == 1:
        run(o_refs[0])
    else:
        for o_ref, rng in zip(o_refs, ranges):
            pl.when(_in_range(pl.program_id(0), rng))(functools.partial(run, o_ref))


def mlp_residual(x, g, w1, w2, layer, *, out_rows=None):
    t, d = x.shape
    f = w1.shape[2]
    out_rows = (t,) if out_rows is None else out_rows
    tm, tf = 512, 1024
    outs = [jax.ShapeDtypeStruct((r, d), F32) for r in out_rows]
    ranges = _tile_ranges(outs, tm)
    res = pl.pallas_call(
        functools.partial(_mlp_kernel, ranges=ranges),
        out_shape=outs,
        grid=(t // tm, f // tf),
        in_specs=[
            pl.BlockSpec((tm, d), lambda i, k: (i, 0)),
            pl.BlockSpec((1, d), lambda i, k: (0, 0)),
            pl.BlockSpec((None, d, tf), lambda i, k: (layer, 0, k)),
            pl.BlockSpec((None, tf, d), lambda i, k: (layer, k, 0)),
        ],
        out_specs=[_part_spec((tm, d), rng, lambda k: 0) for rng in ranges],
        scratch_shapes=[pltpu.VMEM((tm, d), BF16)],
        compiler_params=_cparams(("arbitrary", "arbitrary")),
        name="mlp_residual",
    )(x, g.reshape(1, d), w1, w2)
    return res[0] if len(res) == 1 else res


def _proj_residual_kernel(*refs, ranges):
    x_refs, (a_ref, b_ref, wa_ref, wb_ref, o_ref) = refs[:len(ranges)], refs[len(ranges):]
    acc = jnp.dot(a_ref[...], wa_ref[...], preferred_element_type=F32)
    acc += jnp.dot(b_ref[...], wb_ref[...], preferred_element_type=F32)
    x = x_refs[0][...]
    for x_ref, rng in zip(x_refs[1:], ranges[1:]):
        x = jnp.where(pl.program_id(1) >= rng[0], x_ref[...], x)
    o_ref[...] = x + acc


def proj_residual(x_parts, a, b, w, layer, *, tm=1024, tn=1024):
    n = x_parts[0].shape[1]
    t = sum(p.shape[0] for p in x_parts)
    ranges = _tile_ranges(x_parts, tm)
    k = w.shape[1] // 2
    return pl.pallas_call(
        functools.partial(_proj_residual_kernel, ranges=ranges),
        out_shape=jax.ShapeDtypeStruct((t, n), F32),
        grid=(n // tn, t // tm),
        in_specs=[_part_spec((tm, tn), rng, lambda j: j, cols_outer=True) for rng in ranges] + [
            pl.BlockSpec((tm, k), lambda j, i: (i, 0)),
            pl.BlockSpec((tm, k), lambda j, i: (i, 0)),
            pl.BlockSpec((None, k, tn), lambda j, i: (layer, 0, j)),
            pl.BlockSpec((None, k, tn), lambda j, i: (layer, 1, j)),
        ],
        out_specs=pl.BlockSpec((tm, tn), lambda j, i: (i, j)),
        compiler_params=_cparams(("arbitrary", "arbitrary")),
        name="proj_residual",
    )(*x_parts, a, b, w, w)


def _gate_proj_kernel(x_ref, zz_ref, vg_ref, ws_ref, bs_ref, w_ref, o_ref, vn_ref, m_ref, *, tm):
    for c in range(tm // CHUNK):
        rows = slice(c * CHUNK, (c + 1) * CHUNK)
        vn_ref[rows, :] = _rms_scale(zz_ref[rows, D_C:].astype(F32), vg_ref[...]).astype(BF16)
        for g in range(G_C):
            cols = slice(g * DG_C, (g + 1) * DG_C)
            s = jnp.dot(ws_ref[g], vn_ref[rows, cols], preferred_element_type=F32)
            bias = bs_ref[g]
            s = s + jnp.concatenate([bias] * (DG_C // LANES), axis=1)
            m_ref[rows, cols] = (zz_ref[rows, cols].astype(F32) * s).astype(BF16)
    o_ref[...] = x_ref[...] + jnp.dot(m_ref[...], w_ref[...], preferred_element_type=F32)


def gate_proj_residual(x, zz, v_gain, w_s, b_s_lanes, w, layer, *, tm=512):
    t = zz.shape[0]
    return pl.pallas_call(
        functools.partial(_gate_proj_kernel, tm=tm),
        out_shape=jax.ShapeDtypeStruct((t, D_MODEL), F32),
        grid=(t // tm,),
        in_specs=[
            pl.BlockSpec((tm, D_MODEL), lambda i: (i, 0)),
            pl.BlockSpec((tm, 2 * D_C), lambda i: (i, 0)),
            pl.BlockSpec((1, D_C), lambda i: (0, 0)),
            pl.BlockSpec((G_C, CHUNK, CHUNK), lambda i: (0, 0, 0)),
            pl.BlockSpec((G_C, CHUNK, LANES), lambda i: (0, 0, 0)),
            pl.BlockSpec((None, D_C, D_MODEL), lambda i: (layer, 0, 0)),
        ],
        out_specs=pl.BlockSpec((tm, D_MODEL), lambda i: (i, 0)),
        scratch_shapes=[pltpu.VMEM((tm, D_C), BF16), pltpu.VMEM((tm, D_C), BF16)],
        compiler_params=_cparams(("parallel",)),
        name="gate_proj_residual",
    )(x, zz, v_gain.reshape(1, D_C), w_s, b_s_lanes, w)


N_ROWS = SEQ // GRID_W
GROUP_ROWS = 4
N_GROUPS = N_ROWS // GROUP_ROWS
KEY_ROWS = 12
NQ = GROUP_ROWS * GRID_W
NK = KEY_ROWS * GRID_W
SOFTMAX_ROWS = 64


def _natten_kernel(q_ref, k_ref, v_ref, bias_ref, o_ref, s_ref, p_ref):
    def q_rows(g):
        return pl.ds(g * NQ, NQ)

    def k_rows(g):
        ws = min(max(GROUP_ROWS * g - WIN_H // 2, 0), N_ROWS - KEY_ROWS)
        return pl.ds(ws * GRID_W, NK)

    def scores(g, slot):
        s_ref[slot] = lax.dot_general(q_ref[0, q_rows(g), :], k_ref[0, k_rows(g), :],
                                      (((1,), (1,)), ((), ())), preferred_element_type=F32)

    def softmax(g, slot):
        kind = 0 if g == 0 else (2 if g == N_GROUPS - 1 else 1)
        for c in range(NQ // SOFTMAX_ROWS):
            rows = slice(c * SOFTMAX_ROWS, (c + 1) * SOFTMAX_ROWS)
            s = s_ref[slot, rows, :] + bias_ref[0, kind, rows, :]
            p = jnp.exp(s - jnp.max(s, axis=-1, keepdims=True))
            p = p * (1.0 / jnp.sum(p, axis=-1, keepdims=True))
            p_ref[slot, rows, :] = p.astype(BF16)

    def values(g, slot):
        o = jnp.dot(p_ref[slot], v_ref[0, k_rows(g), :], preferred_element_type=F32)
        o_ref[0, q_rows(g), :] = o.astype(BF16)

    for t in range(N_GROUPS + 2):
        if t < N_GROUPS:
            scores(t, t % 2)
        if 1 <= t <= N_GROUPS:
            softmax(t - 1, (t - 1) % 2)
        if t >= 2:
            values(t - 2, t % 2)


def natten(qkv3, bias_tab):
    b = qkv3.shape[0]
    blk = (1, SEQ, DH)
    return pl.pallas_call(
        _natten_kernel,
        out_shape=jax.ShapeDtypeStruct((b, SEQ, D_A), BF16),
        grid=(H_A, b),
        in_specs=[
            pl.BlockSpec(blk, lambda h, i: (i, 0, h)),
            pl.BlockSpec(blk, lambda h, i: (i, 0, H_A + h)),
            pl.BlockSpec(blk, lambda h, i: (i, 0, 2 * H_A + h)),
            pl.BlockSpec((1, 3, NQ, NK), lambda h, i: (h, 0, 0, 0)),
        ],
        out_specs=pl.BlockSpec(blk, lambda h, i: (i, 0, h)),
        scratch_shapes=[pltpu.VMEM((2, NQ, NK), F32), pltpu.VMEM((2, NQ, NK), BF16)],
        compiler_params=_cparams(("parallel", "arbitrary")),
        name="natten",
    )(qkv3, qkv3, qkv3, bias_tab)


def natten_bias_table(rpb):
    col = np.arange(GRID_W)
    cs = np.clip(col - WIN_W // 2, 0, GRID_W - WIN_W)
    qc, kc = col[:, None], col[None, :]
    col_ok = (kc >= cs[:, None]) & (kc < cs[:, None] + WIN_W)
    dc = np.clip(kc - qc, -(WIN_W - 1), WIN_W - 1) + WIN_W - 1
    bias_c = rpb.astype(F32)[:, :, dc]
    a = np.arange(GROUP_ROWS)
    q_off = np.stack([a, a + WIN_H // 2, a + KEY_ROWS - GROUP_ROWS])
    w_off = np.stack([0 * a, a, 0 * a + KEY_ROWS - WIN_H])
    i = np.arange(KEY_ROWS)
    row_ok = (i >= w_off[..., None]) & (i < w_off[..., None] + WIN_H)
    dr = np.clip(i - q_off[..., None] + WIN_H - 1, 0, 2 * WIN_H - 2)
    tab = bias_c[:, dr]
    ok = row_ok[None, :, :, :, None, None] & col_ok[None, None, None, None]
    tab = jnp.where(ok, tab, -jnp.inf)
    return tab.transpose(0, 1, 2, 4, 3, 5).reshape(H_A, 3, NQ, NK)


FILT_ROWS = 512


def _filter_kernel(z_ref, t_ref, w1_ref, b1_ref, w2_ref, b2_ref, w3_ref, b3_ref, fr_ref,
                   wf_ref, wb_ref, dl_ref, hf_ref, hb_ref, h3_ref):
    @pl.when(pl.program_id(0) == 0)
    def _():
        def mlp(i, carry):
            rows = pl.ds(pl.multiple_of(i * FILT_ROWS, FILT_ROWS), FILT_ROWS)
            fr = fr_ref[...]
            h = jnp.sin(fr * (_dot3(z_ref[rows, :], w1_ref[...]) + b1_ref[...]))
            h = jnp.sin(fr * (_dot3(h, w2_ref[...]) + b2_ref[...]))
            h3_ref[rows, :] = jnp.sin(fr * (_dot3(h, w3_ref[...]) + b3_ref[...]))
            return carry

        lax.fori_loop(0, SEQ // FILT_ROWS, mlp, 0)

    def taps(i, carry):
        sf, sb = carry
        rows = pl.ds(pl.multiple_of(i * FILT_ROWS, FILT_ROWS), FILT_ROWS)
        h3 = h3_ref[rows, :]
        decay = jnp.exp(-t_ref[rows, :] * dl_ref[...])
        hf = _dot3(h3, wf_ref[...]) * decay
        hb = _dot3(h3, wb_ref[...]) * decay
        t_idx = lax.broadcasted_iota(jnp.int32, hb.shape, 0) + i * FILT_ROWS
        hb = jnp.where(t_idx == 0, 0.0, hb)
        hf_ref[rows, :] = hf
        hb_ref[rows, :] = hb
        return (sf + jnp.sum(jnp.abs(hf), axis=0, keepdims=True),
                sb + jnp.sum(jnp.abs(hb), axis=0, keepdims=True))

    zero = jnp.zeros((1, LANES), F32)
    sf, sb = lax.fori_loop(0, SEQ // FILT_ROWS, taps, (zero, zero))
    inv = 1.0 / (sf + sb)

    def scale(i, carry):
        rows = pl.ds(pl.multiple_of(i * FILT_ROWS, FILT_ROWS), FILT_ROWS)
        hf_ref[rows, :] = hf_ref[rows, :] * inv
        hb_ref[rows, :] = hb_ref[rows, :] * inv
        return carry

    lax.fori_loop(0, SEQ // FILT_ROWS, scale, 0)


def _filter_constants():
    t = np.linspace(0.0, 1.0, SEQ, dtype=np.float32).astype(np.float64)[:, None]
    bands = (EMB - 1) // 2
    w = 2.0 * math.pi * np.arange(SEQ, dtype=np.float64)[:, None] / SEQ
    f = np.linspace(1e-4, bands - 1, bands, dtype=np.float32).astype(np.float64)[None, :]
    z = np.concatenate([t, np.cos(f * w), -np.sin(f * w)], axis=-1)
    z_pad = np.zeros((SEQ, LANES), np.float32)
    z_pad[:, :EMB] = z
    t_lanes = np.broadcast_to(t.astype(np.float32), (SEQ, LANES)).copy()
    deltas = np.abs(np.linspace(math.log(DECAY_TARGET) / FAST_DECAY,
                                math.log(DECAY_TARGET) / SLOW_DECAY, D_B, dtype=np.float32))
    return z_pad, t_lanes, deltas.reshape(1, D_B)


def hyena_filter(f_w1, f_b1, f_w2, f_b2, f_w3, f_b3, f_wout, f_freq):
    z_pad, t_lanes, deltas = _filter_constants()
    w1_pad = jnp.zeros((LANES, FO), F32).at[:EMB].set(f_w1)
    nblk = D_B // LANES
    const = lambda shape: pl.BlockSpec(shape, lambda c: (0,) * len(shape))
    return pl.pallas_call(
        _filter_kernel,
        out_shape=[jax.ShapeDtypeStruct((SEQ, D_B), F32)] * 2,
        grid=(nblk,),
        in_specs=[
            const((SEQ, LANES)), const((SEQ, LANES)),
            const((LANES, FO)), const((1, FO)),
            const((FO, FO)), const((1, FO)),
            const((FO, FO)), const((1, FO)),
            const((1, FO)),
            pl.BlockSpec((FO, LANES), lambda c: (0, c)),
            pl.BlockSpec((FO, LANES), lambda c: (0, nblk + c)),
            pl.BlockSpec((1, LANES), lambda c: (0, c)),
        ],
        out_specs=[pl.BlockSpec((SEQ, LANES), lambda c: (0, c))] * 2,
        scratch_shapes=[pltpu.VMEM((SEQ, FO), F32)],
        compiler_params=_cparams(("arbitrary",)),
        name="hyena_filter",
    )(jnp.asarray(z_pad), jnp.asarray(t_lanes), w1_pad, f_b1.reshape(1, FO), f_w2,
      f_b2.reshape(1, FO), f_w3, f_b3.reshape(1, FO), f_freq.reshape(1, FO), f_wout, f_wout,
      jnp.asarray(deltas))


ROW_TILE = 8
N_K1_PAD = N_K1 + 1


@functools.lru_cache(maxsize=None)
def _dft_tables():
    k1 = np.arange(N_K1_PAD)
    n2 = np.arange(N_FAST)
    ang = 2.0 * math.pi * np.outer(k1, n2) / N_FFT
    tw = np.concatenate([np.cos(ang), -np.sin(ang)], axis=1)
    tw = np.broadcast_to(tw[:, :, None], (N_K1_PAD, 2 * N_FAST, LANES))
    ang2 = 2.0 * math.pi * np.outer(n2, n2) / N_FAST
    fr, fi = np.cos(ang2), -np.sin(ang2)
    f_fwd = np.block([[fr, -fi], [fi, fr]])
    f_inv = np.block([[fr, fi], [-fi, fr]])
    f32 = lambda a: np.ascontiguousarray(a, dtype=np.float32)
    return f32(tw), f32(f_fwd), f32(f_inv)


def _cmul_const(x, w):
    xr, xi = x
    c, s = w.real, w.imag
    eps = 1e-12
    if abs(s) < eps and abs(c - 1.0) < eps:
        return xr, xi
    if abs(c) < eps and abs(s - 1.0) < eps:
        return ("neg", xi), xr
    if abs(c) < eps and abs(s + 1.0) < eps:
        return xi, ("neg", xr)
    return xr * c - xi * s, xr * s + xi * c


def _is_neg(v):
    return isinstance(v, tuple) and v[0] == "neg"


def _plain(v):
    return -v[1] if _is_neg(v) else v


def _add(a, b):
    return a - b[1] if _is_neg(b) else a + b


def _sub(a, b):
    return a + b[1] if _is_neg(b) else a - b


def _fft_dit(xs, sign):
    n = len(xs)
    if n == 1:
        return xs
    ev = _fft_dit(xs[0::2], sign)
    od = _fft_dit(xs[1::2], sign)
    out = [None] * n
    for k in range(n // 2):
        w = complex(math.cos(2 * math.pi * k / n), sign * math.sin(2 * math.pi * k / n))
        tr, ti = _cmul_const(od[k], w)
        er, ei = ev[k]
        out[k] = (_add(er, tr), _add(ei, ti))
        out[k + n // 2] = (_sub(er, tr), _sub(ei, ti))
    return out


def _slow_forward(x):
    half, quarter = N_SLOW // 2, N_SLOW // 4
    z = [(x[2 * m], x[2 * m + 1]) for m in range(quarter)]
    zt = [z[0]]
    for m in range(1, quarter):
        tr, ti = _cmul_const(z[m], complex(math.cos(2 * math.pi * m / half), -math.sin(2 * math.pi * m / half)))
        zt.append((_plain(tr), _plain(ti)))
    ze = _fft_dit(z, -1)
    zo = _fft_dit(zt, -1)
    zc = [None] * half
    for j in range(quarter):
        zc[2 * j] = ze[j]
        zc[2 * j + 1] = zo[j]
    out = [None] * (half + 1)
    ar, ai = zc[0]
    out[0] = ((ar + ai) * 2.0, None)
    out[half] = ((ar - ai) * 2.0, None)
    ar, ai = zc[quarter]
    out[quarter] = (ar * 2.0, ai * -2.0)
    for k in range(1, quarter):
        ar, ai = zc[k]
        br, bi = zc[half - k]
        c, s = math.cos(2 * math.pi * k / N_SLOW), math.sin(2 * math.pi * k / N_SLOW)
        p, q, r, t = ar + br, ai - bi, ai + bi, ar - br
        u = r * c - t * s
        vn = t * (-c) - r * s
        out[k] = (p + u, q + vn)
        out[half - k] = (p - u, vn - q)
    return out


def _slow_inverse(xs):
    half, quarter = N_SLOW // 2, N_SLOW // 4
    zc = [None] * half
    ar, br = xs[0][0], xs[half][0]
    zc[0] = (ar + br, ar - br)
    ar, ai = xs[quarter]
    zc[quarter] = (ar * 2.0, ai * -2.0)
    for k in range(1, quarter):
        ar, ai = xs[k]
        br, bi = xs[half - k]
        c, s = math.cos(2 * math.pi * k / N_SLOW), math.sin(2 * math.pi * k / N_SLOW)
        p, q, t, r = ar + br, ai - bi, ar - br, ai + bi
        o_r = t * c - r * s
        o_i = t * s + r * c
        zc[k] = (p - o_i, q + o_r)
        zc[half - k] = (p + o_i, o_r - q)
    ze = _fft_dit(zc[0::2], +1)
    zo = _fft_dit(zc[1::2], +1)
    out = [None] * N_IN
    for m in range(quarter):
        tr, ti = _cmul_const(zo[m], complex(math.cos(2 * math.pi * m / half), math.sin(2 * math.pi * m / half)))
        out[2 * m] = _add(ze[m][0], tr)
        out[2 * m + 1] = _add(ze[m][1], ti)
    return out


def _slow_forward_pass(src_ref, a_ref):
    lanes = src_ref.shape[-1]
    zeros = jnp.zeros((ROW_TILE, lanes), F32)

    def body(j, carry):
        rows = pl.ds(pl.multiple_of(j * ROW_TILE, ROW_TILE), ROW_TILE)
        rows_im = pl.ds(pl.multiple_of(N_FAST + j * ROW_TILE, ROW_TILE), ROW_TILE)
        out = _slow_forward([src_ref[n1, rows, :] for n1 in range(N_IN)])
        for k1, (re, im) in enumerate(out):
            a_ref[k1, rows, :] = re
            a_ref[k1, rows_im, :] = zeros if im is None else im
        a_ref[N_K1, rows, :] = zeros
        a_ref[N_K1, rows_im, :] = zeros
        return carry

    lax.fori_loop(0, N_FAST // ROW_TILE, body, 0)


def _cmul(a, w):
    ar, ai, wr, wi = a[:N_FAST], a[N_FAST:], w[:N_FAST], w[N_FAST:]
    return ar * wr - ai * wi, ar * wi + ai * wr


def _cmul_conj(a, w):
    ar, ai, wr, wi = a[:N_FAST], a[N_FAST:], w[:N_FAST], w[N_FAST:]
    return ar * wr + ai * wi, ai * wr - ar * wi


def _spectrum_kernel(hf_ref, hb_ref, tw_ref, f_ref, o_ref, af_ref, ab_ref):
    _slow_forward_pass(hf_ref, af_ref)
    _slow_forward_pass(hb_ref, ab_ref)

    def body(k1, carry):
        tw = tw_ref[k1]
        f = f_ref[...]
        both = jnp.concatenate([jnp.concatenate(_cmul(af_ref[k1], tw), axis=0),
                                jnp.concatenate(_cmul(ab_ref[k1], tw), axis=0)], axis=1)
        x = _dot3(f, both)
        xf, xb = x[:, :LANES], x[:, LANES:]
        re = xf[:N_FAST] + xb[:N_FAST]
        im = xf[N_FAST:] - xb[N_FAST:]
        o_ref[k1] = jnp.concatenate([re, im], axis=0) * (1.0 / (4 * N_FFT))
        return carry

    lax.fori_loop(0, N_K1_PAD, body, 0)


def filter_spectrum(hf, hb, tw, f_fwd):
    nblk = D_B // LANES
    slabs = lambda a: a.reshape(N_IN, N_FAST, D_B)
    whole = pl.BlockSpec(memory_space=pltpu.VMEM)
    spec_blk = (N_K1_PAD, 2 * N_FAST, LANES)
    return pl.pallas_call(
        _spectrum_kernel,
        out_shape=jax.ShapeDtypeStruct((N_K1_PAD, 2 * N_FAST, D_B), F32),
        grid=(nblk,),
        in_specs=[
            pl.BlockSpec((N_IN, N_FAST, LANES), lambda c: (0, 0, c)),
            pl.BlockSpec((N_IN, N_FAST, LANES), lambda c: (0, 0, c)),
            whole, whole,
        ],
        out_specs=pl.BlockSpec(spec_blk, lambda c: (0, 0, c)),
        scratch_shapes=[pltpu.VMEM(spec_blk, F32)] * 2,
        compiler_params=_cparams(("arbitrary",)),
        name="filter_spectrum",
    )(slabs(hf), slabs(hb), tw, f_fwd)


def _short_conv_slab(x_ref, w_ref, b_ref, n1):
    r0 = pl.multiple_of(n1 * N_FAST, N_FAST)
    edge = 16
    cur = x_ref[0, pl.ds(r0, N_FAST), :].astype(F32)
    before = x_ref[0, pl.ds(pl.multiple_of(jnp.maximum(r0 - edge, 0), edge), edge), :]
    after = x_ref[0, pl.ds(pl.multiple_of(jnp.minimum(r0 + N_FAST, SEQ - edge), edge), edge), :]
    before = jnp.where(n1 > 0, before.astype(F32)[edge - 1:edge], 0.0)
    after = jnp.where(n1 < N_IN - 1, after.astype(F32)[0:1], 0.0)
    row = lax.broadcasted_iota(jnp.int32, (ROW_TILE, cur.shape[1]), 0)
    prev = pltpu.roll(cur, 1, axis=0)
    prev = jnp.concatenate([jnp.where(row == 0, before, prev[:ROW_TILE]), prev[ROW_TILE:]], axis=0)
    nxt = pltpu.roll(cur, N_FAST - 1, axis=0)
    nxt = jnp.concatenate([nxt[:-ROW_TILE], jnp.where(row == ROW_TILE - 1, after, nxt[-ROW_TILE:])], axis=0)
    return prev * w_ref[0:1, :] + cur * w_ref[1:2, :] + nxt * w_ref[2:3, :] + b_ref[...]


def _hyena_kernel(x0_ref, x1_ref, v_ref, w0_ref, w1_ref, wv_ref, b0_ref, b1_ref, bv_ref,
                  hs_ref, hbias_ref, tw_ref, ff_ref, fi_ref, o_ref, u_ref, g0_ref, a_ref, z_ref):
    def conv(n1, carry):
        u_ref[n1] = _short_conv_slab(v_ref, wv_ref, bv_ref, n1) * _short_conv_slab(x1_ref, w1_ref, b1_ref, n1)
        g0_ref[n1] = _short_conv_slab(x0_ref, w0_ref, b0_ref, n1)
        return carry

    lax.fori_loop(0, N_IN, conv, 0)
    _slow_forward_pass(u_ref, a_ref)

    def freq(i, carry):
        ks = (2 * i, 2 * i + 1)
        tws = [tw_ref[k] for k in ks]
        a = [jnp.concatenate(_cmul(a_ref[k], tw), axis=0).astype(BF16) for k, tw in zip(ks, tws)]
        x = jnp.dot(ff_ref[...], jnp.concatenate(a, axis=1), preferred_element_type=F32)
        y = [jnp.concatenate(_cmul(x[:, j * LANES:(j + 1) * LANES], hs_ref[k]), axis=0).astype(BF16)
             for j, k in enumerate(ks)]
        z = jnp.dot(fi_ref[...], jnp.concatenate(y, axis=1), preferred_element_type=F32)
        for j, (k, tw) in enumerate(zip(ks, tws)):
            z_ref[k] = jnp.concatenate(_cmul_conj(z[:, j * LANES:(j + 1) * LANES], tw), axis=0)
        return carry

    lax.fori_loop(0, N_K1_PAD // 2, freq, 0, unroll=True)

    def back(j, carry):
        rows = pl.ds(pl.multiple_of(j * ROW_TILE, ROW_TILE), ROW_TILE)
        rows_im = pl.ds(pl.multiple_of(N_FAST + j * ROW_TILE, ROW_TILE), ROW_TILE)
        y = _slow_inverse([(z_ref[k1, rows, :], z_ref[k1, rows_im, :]) for k1 in range(N_K1)])
        for n1 in range(N_IN):
            u_ref[n1, rows, :] = y[n1] + u_ref[n1, rows, :] * hbias_ref[...]
        return carry

    lax.fori_loop(0, N_FAST // ROW_TILE, back, 0)

    def gate(n1, carry):
        rows = pl.ds(pl.multiple_of(n1 * N_FAST, N_FAST), N_FAST)
        o_ref[0, rows, :] = (u_ref[n1] * g0_ref[n1]).astype(BF16)
        return carry

    lax.fori_loop(0, N_IN, gate, 0)


def hyena(p3, sc_w, sc_b, h_spec, h_bias, tables):
    tw, f_fwd, f_inv = tables
    b = p3.shape[0]
    nblk = D_B // LANES
    base = 3 * D_A // LANES
    xblk = lambda off: pl.BlockSpec((1, SEQ, LANES), lambda c, i: (i, 0, base + off * nblk + c))
    wblk = lambda off: pl.BlockSpec((3, LANES), lambda c, i: (0, off * nblk + c))
    bblk = lambda off: pl.BlockSpec((1, LANES), lambda c, i: (0, off * nblk + c))
    whole = pl.BlockSpec(memory_space=pltpu.VMEM)
    sc_b2 = sc_b.reshape(1, 3 * D_B)
    spec_blk = (N_K1_PAD, 2 * N_FAST, LANES)
    return pl.pallas_call(
        _hyena_kernel,
        out_shape=jax.ShapeDtypeStruct((b, SEQ, D_B), BF16),
        grid=(nblk, b),
        in_specs=[
            xblk(0), xblk(1), xblk(2),
            wblk(0), wblk(1), wblk(2),
            bblk(0), bblk(1), bblk(2),
            pl.BlockSpec(spec_blk, lambda c, i: (0, 0, c)),
            pl.BlockSpec((1, LANES), lambda c, i: (0, c)),
            whole, whole, whole,
        ],
        out_specs=pl.BlockSpec((1, SEQ, LANES), lambda c, i: (i, 0, c)),
        scratch_shapes=[
            pltpu.VMEM((N_IN, N_FAST, LANES), F32),
            pltpu.VMEM((N_IN, N_FAST, LANES), F32),
            pltpu.VMEM(spec_blk, F32),
            pltpu.VMEM(spec_blk, F32),
        ],
        compiler_params=_cparams(("parallel", "arbitrary")),
        name="hyena",
    )(p3, p3, p3, sc_w, sc_w, sc_w, sc_b2, sc_b2, sc_b2,
      h_spec, h_bias.reshape(1, D_B), tw, f_fwd.astype(BF16), f_inv.astype(BF16))


def kernel(x_prompt, x_sample, norm_mix, norm_mlp, w_in_ab, sc_w, sc_b, q_gain, k_gain, rpb,
           f_w1, f_b1, f_w2, f_b2, f_w3, f_b3, f_wout, f_freq, h_bias, w_out_ab, w_in_c,
           v_gain, w_s, b_s, w_out_c, w_mlp1, w_mlp2):
    nb_p, nb_s = x_prompt.shape[0], x_sample.shape[0]
    nb = nb_p + nb_s
    x = (x_prompt.reshape(nb_p * SEQ, D_MODEL), x_sample.reshape(nb_s * SEQ, D_MODEL))
    tables = tuple(jnp.asarray(a) for a in _dft_tables())

    w_in_ab, w_out_ab, w_in_c, w_out_c, w_mlp1, w_mlp2, w_s = (
        w.astype(BF16) for w in (w_in_ab, w_out_ab, w_in_c, w_out_c, w_mlp1, w_mlp2, w_s))

    for i in range(DEPTH):
        j = i // 2
        if i % 2 == 0:
            p3 = in_proj(x, norm_mix[i], w_in_ab, j, n=3 * (D_A + D_B), mode="qk_norm",
                         qk_gains=(q_gain[j], k_gain[j])).reshape(nb, SEQ, 3 * (D_A + D_B))
            a = natten(p3, natten_bias_table(rpb[j]))
            hf, hb = hyena_filter(f_w1[j], f_b1[j], f_w2[j], f_b2[j], f_w3[j], f_b3[j],
                                  f_wout[j], f_freq[j])
            h_spec = filter_spectrum(hf, hb, tables[0], tables[1])
            bo = hyena(p3, sc_w[j], sc_b[j], h_spec, h_bias[j], tables)
            x = (proj_residual(x, a.reshape(nb * SEQ, D_A), bo.reshape(nb * SEQ, D_B), w_out_ab, j),)
        else:
            zz = in_proj(x, norm_mix[i], w_in_c, j, n=2 * D_C, mode="gelu")
            b_lanes = jnp.broadcast_to(b_s[j][:, :, None], (G_C, CHUNK, LANES))
            x = (gate_proj_residual(x[0], zz, v_gain[j], w_s[j], b_lanes, w_out_c, j),)
        if i < DEPTH - 1:
            x = (mlp_residual(x[0], norm_mlp[i], w_mlp1, w_mlp2, i),)

    y_p, y_s = mlp_residual(x[0], norm_mlp[DEPTH - 1], w_mlp1, w_mlp2, DEPTH - 1,
                            out_rows=(nb_p * SEQ, nb_s * SEQ))
    return y_p.reshape(nb_p, SEQ, D_MODEL), y_s.reshape(nb_s, SEQ, D_MODEL)
```

```python
import functools
import math

import numpy as np
import jax
import jax.numpy as jnp
from jax import lax
from jax.experimental import pallas as pl
from jax.experimental.pallas import tpu as pltpu

F32 = jnp.float32
BF16 = jnp.bfloat16

D_MODEL = 2048
DEPTH = 4
SEQ = 4096
GRID_W = 64
WIN_H = 8
WIN_W = 16
DH = 128
H_A = 8
D_A = H_A * DH
D_B = 1024
EMB = 33
FO = 64
FAST_DECAY = 0.3
SLOW_DECAY = 1.5
DECAY_TARGET = 1e-2
D_C = D_MODEL
G_C = 8
DG_C = D_C // G_C
CHUNK = 128
D_FF = 4 * D_MODEL
EPS = 1e-6

LANES = 128
VMEM_LIMIT_BYTES = 52 * 1024 * 1024

N_FFT = 2 * SEQ
N_FAST = 128
N_SLOW = N_FFT // N_FAST
N_IN = SEQ // N_FAST
N_K1 = N_SLOW // 2 + 1


def _cparams(sem, vmem_limit_bytes=VMEM_LIMIT_BYTES):
    return pltpu.CompilerParams(dimension_semantics=sem, vmem_limit_bytes=vmem_limit_bytes)


def _tile_ranges(parts, tm):
    ranges, lo = [], 0
    for p in parts:
        ranges.append((lo, lo + p.shape[0] // tm))
        lo = ranges[-1][1]
    return ranges


def _part_spec(block, rng, col_fn, cols_outer=False):
    lo, hi = rng

    def index(i, j):
        return jnp.clip(i - lo, 0, hi - lo - 1), jnp.where(_in_range(i, rng), col_fn(j), 0)

    return pl.BlockSpec(block, (lambda j, i: index(i, j)) if cols_outer else index)


def _in_range(i, rng):
    return (i >= rng[0]) & (i < rng[1])


def _rms_scale(x, g):
    return x * lax.rsqrt(jnp.mean(x * x, axis=-1, keepdims=True) + EPS) * g


def _gelu_tanh(x):
    return 0.5 * x * (1.0 + jnp.tanh(math.sqrt(2.0 / math.pi) * (x + 0.044715 * (x * x * x))))


def _split_bf16(x):
    hi = x.astype(BF16)
    lo = (x - hi.astype(F32)).astype(BF16)
    return hi, lo


def _dot3(a, b):
    ah, al = _split_bf16(a)
    bh, bl = _split_bf16(b)
    d = functools.partial(jnp.dot, preferred_element_type=F32)
    return d(ah, bh) + (d(ah, bl) + d(al, bh))


def _in_proj_kernel(*refs, mode, ranges):
    x_refs, refs = refs[:len(ranges)], refs[len(ranges):]
    if mode == "gelu":
        g_ref, w_ref, o_ref, h_ref = refs
    else:
        g_ref, w_ref, qg_ref, kg_ref, o_ref, h_ref = refs
    i, j = pl.program_id(0), pl.program_id(1)

    def tile(rows):
        acc = jnp.dot(h_ref[rows, :], w_ref[...], preferred_element_type=F32)
        if mode == "gelu":
            o_ref[rows, :] = _gelu_tanh(acc).astype(BF16)
        else:
            gain = jnp.where(j == 0, qg_ref[...] * (DH ** -0.5), kg_ref[...])
            for c in range(0, o_ref.shape[1], DH):
                seg = acc[:, c:c + DH]
                o_ref[rows, c:c + DH] = jnp.where(j < 2, _rms_scale(seg, gain), seg).astype(BF16)

    half = o_ref.shape[0] // 2
    for x_ref, rng in zip(x_refs, ranges):
        @pl.when((j == 0) & _in_range(i, rng))
        def _():
            for rows in (slice(0, half), slice(half, 2 * half)):
                h_ref[rows, :] = _rms_scale(x_ref[rows, :], g_ref[...]).astype(BF16)
                tile(rows)

    @pl.when(j > 0)
    def _():
        tile(slice(None))


def in_proj(x_parts, g, w, layer, *, n, mode, qk_gains=(), tn=1024):
    d = x_parts[0].shape[1]
    tm = 1024
    t = sum(p.shape[0] for p in x_parts)
    x_bytes = len(x_parts) * 2 * tm * d * 4
    other_bytes = 2 * d * tn * 2 + 2 * tm * tn * 2 + tm * d * 2 + tm * tn * 4
    vmem_limit = max(VMEM_LIMIT_BYTES, x_bytes + other_bytes + (4 << 20))
    ranges = _tile_ranges(x_parts, tm)
    return pl.pallas_call(
        functools.partial(_in_proj_kernel, mode=mode, ranges=ranges),
        out_shape=jax.ShapeDtypeStruct((t, n), BF16),
        grid=(t // tm, n // tn),
        in_specs=[_part_spec((tm, d), rng, lambda j: 0) for rng in ranges] + [
            pl.BlockSpec((1, d), lambda i, j: (0, 0)),
            pl.BlockSpec((None, d, tn), lambda i, j: (layer, 0, j)),
        ] + [pl.BlockSpec((1, DH), lambda i, j: (0, 0))] * len(qk_gains),
        out_specs=pl.BlockSpec((tm, tn), lambda i, j: (i, j)),
        scratch_shapes=[pltpu.VMEM((tm, d), BF16)],
        compiler_params=_cparams(("parallel", "arbitrary"), vmem_limit),
        name="in_proj",
    )(*x_parts, g.reshape(1, d), w, *[gain.reshape(1, DH) for gain in qk_gains])


def _mlp_kernel(x_ref, g_ref, w1_ref, w2_ref, *refs, ranges):
    o_refs, h_ref = refs[:-1], refs[-1]

    def run(o_ref):
        def tile(rows, first):
            a = jnp.dot(h_ref[rows, :], w1_ref[...], preferred_element_type=F32)
            a = jnp.square(jnp.maximum(a, 0.0)).astype(BF16)
            y = jnp.dot(a, w2_ref[...], preferred_element_type=F32)
            o_ref[rows, :] = (x_ref[rows, :] if first else o_ref[rows, :]) + y

        half = o_ref.shape[0] // 2

        @pl.when(pl.program_id(1) == 0)
        def _():
            for rows in (slice(0, half), slice(half, 2 * half)):
                h_ref[rows, :] = _rms_scale(x_ref[rows, :], g_ref[...]).astype(BF16)
                tile(rows, True)

        @pl.when(pl.program_id(1) > 0)
        def _():
            tile(slice(None), False)

    if len(o_refs) == 1:
        run(o_refs[0])
    else:
        for o_ref, rng in zip(o_refs, ranges):
            pl.when(_in_range(pl.program_id(0), rng))(functools.partial(run, o_ref))


def mlp_residual(x, g, w1, w2, layer, *, out_rows=None):
    t, d = x.shape
    f = w1.shape[2]
    out_rows = (t,) if out_rows is None else out_rows
    tm, tf = 512, 1024
    outs = [jax.ShapeDtypeStruct((r, d), F32) for r in out_rows]
    ranges = _tile_ranges(outs, tm)
    res = pl.pallas_call(
        functools.partial(_mlp_kernel, ranges=ranges),
        out_shape=outs,
        grid=(t // tm, f // tf),
        in_specs=[
            pl.BlockSpec((tm, d), lambda i, k: (i, 0)),
            pl.BlockSpec((1, d), lambda i, k: (0, 0)),
            pl.BlockSpec((None, d, tf), lambda i, k: (layer, 0, k)),
            pl.BlockSpec((None, tf, d), lambda i, k: (layer, k, 0)),
        ],
        out_specs=[_part_spec((tm, d), rng, lambda k: 0) for rng in ranges],
        scratch_shapes=[pltpu.VMEM((tm, d), BF16)],
        compiler_params=_cparams(("arbitrary", "arbitrary")),
        name="mlp_residual",
    )(x, g.reshape(1, d), w1, w2)
    return res[0] if len(res) == 1 else res


def _proj_residual_kernel(*refs, ranges):
    x_refs, (a_ref, b_ref, wa_ref, wb_ref, o_ref) = refs[:len(ranges)], refs[len(ranges):]
    acc = jnp.dot(a_ref[...], wa_ref[...], preferred_element_type=F32)
    acc += jnp.dot(b_ref[...], wb_ref[...], preferred_element_type=F32)
    x = x_refs[0][...]
    for x_ref, rng in zip(x_refs[1:], ranges[1:]):
        x = jnp.where(pl.program_id(1) >= rng[0], x_ref[...], x)
    o_ref[...] = x + acc


def proj_residual(x_parts, a, b, w, layer, *, tm=1024, tn=1024):
    n = x_parts[0].shape[1]
    t = sum(p.shape[0] for p in x_parts)
    ranges = _tile_ranges(x_parts, tm)
    k = w.shape[1] // 2
    return pl.pallas_call(
        functools.partial(_proj_residual_kernel, ranges=ranges),
        out_shape=jax.ShapeDtypeStruct((t, n), F32),
        grid=(n // tn, t // tm),
        in_specs=[_part_spec((tm, tn), rng, lambda j: j, cols_outer=True) for rng in ranges] + [
            pl.BlockSpec((tm, k), lambda j, i: (i, 0)),
            pl.BlockSpec((tm, k), lambda j, i: (i, 0)),
            pl.BlockSpec((None, k, tn), lambda j, i: (layer, 0, j)),
            pl.BlockSpec((None, k, tn), lambda j, i: (layer, 1, j)),
        ],
        out_specs=pl.BlockSpec((tm, tn), lambda j, i: (i, j)),
        compiler_params=_cparams(("arbitrary", "arbitrary")),
        name="proj_residual",
    )(*x_parts, a, b, w, w)


def _gate_proj_kernel(x_ref, zz_ref, vg_ref, ws_ref, bs_ref, w_ref, o_ref, vn_ref, m_ref, *, tm):
    for c in range(tm // CHUNK):
        rows = slice(c * CHUNK, (c + 1) * CHUNK)
        vn_ref[rows, :] = _rms_scale(zz_ref[rows, D_C:].astype(F32), vg_ref[...]).astype(BF16)
        for g in range(G_C):
            cols = slice(g * DG_C, (g + 1) * DG_C)
            s = jnp.dot(ws_ref[g], vn_ref[rows, cols], preferred_element_type=F32)
            bias = bs_ref[g]
            s = s + jnp.concatenate([bias] * (DG_C // LANES), axis=1)
            m_ref[rows, cols] = (zz_ref[rows, cols].astype(F32) * s).astype(BF16)
    o_ref[...] = x_ref[...] + jnp.dot(m_ref[...], w_ref[...], preferred_element_type=F32)


def gate_proj_residual(x, zz, v_gain, w_s, b_s_lanes, w, layer, *, tm=512):
    t = zz.shape[0]
    return pl.pallas_call(
        functools.partial(_gate_proj_kernel, tm=tm),
        out_shape=jax.ShapeDtypeStruct((t, D_MODEL), F32),
        grid=(t // tm,),
        in_specs=[
            pl.BlockSpec((tm, D_MODEL), lambda i: (i, 0)),
            pl.BlockSpec((tm, 2 * D_C), lambda i: (i, 0)),
            pl.BlockSpec((1, D_C), lambda i: (0, 0)),
            pl.BlockSpec((G_C, CHUNK, CHUNK), lambda i: (0, 0, 0)),
            pl.BlockSpec((G_C, CHUNK, LANES), lambda i: (0, 0, 0)),
            pl.BlockSpec((None, D_C, D_MODEL), lambda i: (layer, 0, 0)),
        ],
        out_specs=pl.BlockSpec((tm, D_MODEL), lambda i: (i, 0)),
        scratch_shapes=[pltpu.VMEM((tm, D_C), BF16), pltpu.VMEM((tm, D_C), BF16)],
        compiler_params=_cparams(("parallel",)),
        name="gate_proj_residual",
    )(x, zz, v_gain.reshape(1, D_C), w_s, b_s_lanes, w)


N_ROWS = SEQ // GRID_W
GROUP_ROWS = 4
N_GROUPS = N_ROWS // GROUP_ROWS
KEY_ROWS = 12
NQ = GROUP_ROWS * GRID_W
NK = KEY_ROWS * GRID_W
SOFTMAX_ROWS = 64


def _natten_kernel(q_ref, k_ref, v_ref, bias_ref, o_ref, s_ref, p_ref):
    def q_rows(g):
        return pl.ds(g * NQ, NQ)

    def k_rows(g):
        ws = min(max(GROUP_ROWS * g - WIN_H // 2, 0), N_ROWS - KEY_ROWS)
        return pl.ds(ws * GRID_W, NK)

    def scores(g, slot):
        s_ref[slot] = lax.dot_general(q_ref[0, q_rows(g), :], k_ref[0, k_rows(g), :],
                                      (((1,), (1,)), ((), ())), preferred_element_type=F32)

    def softmax(g, slot):
        kind = 0 if g == 0 else (2 if g == N_GROUPS - 1 else 1)
        for c in range(NQ // SOFTMAX_ROWS):
            rows = slice(c * SOFTMAX_ROWS, (c + 1) * SOFTMAX_ROWS)
            s = s_ref[slot, rows, :] + bias_ref[0, kind, rows, :]
            p = jnp.exp(s - jnp.max(s, axis=-1, keepdims=True))
            p = p * (1.0 / jnp.sum(p, axis=-1, keepdims=True))
            p_ref[slot, rows, :] = p.astype(BF16)

    def values(g, slot):
        o = jnp.dot(p_ref[slot], v_ref[0, k_rows(g), :], preferred_element_type=F32)
        o_ref[0, q_rows(g), :] = o.astype(BF16)

    for t in range(N_GROUPS + 2):
        if t < N_GROUPS:
            scores(t, t % 2)
        if 1 <= t <= N_GROUPS:
            softmax(t - 1, (t - 1) % 2)
        if t >= 2:
            values(t - 2, t % 2)


def natten(qkv3, bias_tab):
    b = qkv3.shape[0]
    blk = (1, SEQ, DH)
    return pl.pallas_call(
        _natten_kernel,
        out_shape=jax.ShapeDtypeStruct((b, SEQ, D_A), BF16),
        grid=(H_A, b),
        in_specs=[
            pl.BlockSpec(blk, lambda h, i: (i, 0, h)),
            pl.BlockSpec(blk, lambda h, i: (i, 0, H_A + h)),
            pl.BlockSpec(blk, lambda h, i: (i, 0, 2 * H_A + h)),
            pl.BlockSpec((1, 3, NQ, NK), lambda h, i: (h, 0, 0, 0)),
        ],
        out_specs=pl.BlockSpec(blk, lambda h, i: (i, 0, h)),
        scratch_shapes=[pltpu.VMEM((2, NQ, NK), F32), pltpu.VMEM((2, NQ, NK), BF16)],
        compiler_params=_cparams(("parallel", "arbitrary")),
        name="natten",
    )(qkv3, qkv3, qkv3, bias_tab)


def natten_bias_table(rpb):
    col = np.arange(GRID_W)
    cs = np.clip(col - WIN_W // 2, 0, GRID_W - WIN_W)
    qc, kc = col[:, None], col[None, :]
    col_ok = (kc >= cs[:, None]) & (kc < cs[:, None] + WIN_W)
    dc = np.clip(kc - qc, -(WIN_W - 1), WIN_W - 1) + WIN_W - 1
    bias_c = rpb.astype(F32)[:, :, dc]
    a = np.arange(GROUP_ROWS)
    q_off = np.stack([a, a + WIN_H // 2, a + KEY_ROWS - GROUP_ROWS])
    w_off = np.stack([0 * a, a, 0 * a + KEY_ROWS - WIN_H])
    i = np.arange(KEY_ROWS)
    row_ok = (i >= w_off[..., None]) & (i < w_off[..., None] + WIN_H)
    dr = np.clip(i - q_off[..., None] + WIN_H - 1, 0, 2 * WIN_H - 2)
    tab = bias_c[:, dr]
    ok = row_ok[None, :, :, :, None, None] & col_ok[None, None, None, None]
    tab = jnp.where(ok, tab, -jnp.inf)
    return tab.transpose(0, 1, 2, 4, 3, 5).reshape(H_A, 3, NQ, NK)


FILT_ROWS = 512


def _filter_kernel(z_ref, t_ref, w1_ref, b1_ref, w2_ref, b2_ref, w3_ref, b3_ref, fr_ref,
                   wf_ref, wb_ref, dl_ref, hf_ref, hb_ref, h3_ref):
    @pl.when(pl.program_id(0) == 0)
    def _():
        def mlp(i, carry):
            rows = pl.ds(pl.multiple_of(i * FILT_ROWS, FILT_ROWS), FILT_ROWS)
            fr = fr_ref[...]
            h = jnp.sin(fr * (_dot3(z_ref[rows, :], w1_ref[...]) + b1_ref[...]))
            h = jnp.sin(fr * (_dot3(h, w2_ref[...]) + b2_ref[...]))
            h3_ref[rows, :] = jnp.sin(fr * (_dot3(h, w3_ref[...]) + b3_ref[...]))
            return carry

        lax.fori_loop(0, SEQ // FILT_ROWS, mlp, 0)

    def taps(i, carry):
        sf, sb = carry
        rows = pl.ds(pl.multiple_of(i * FILT_ROWS, FILT_ROWS), FILT_ROWS)
        h3 = h3_ref[rows, :]
        decay = jnp.exp(-t_ref[rows, :] * dl_ref[...])
        hf = _dot3(h3, wf_ref[...]) * decay
        hb = _dot3(h3, wb_ref[...]) * decay
        t_idx = lax.broadcasted_iota(jnp.int32, hb.shape, 0) + i * FILT_ROWS
        hb = jnp.where(t_idx == 0, 0.0, hb)
        hf_ref[rows, :] = hf
        hb_ref[rows, :] = hb
        return (sf + jnp.sum(jnp.abs(hf), axis=0, keepdims=True),
                sb + jnp.sum(jnp.abs(hb), axis=0, keepdims=True))

    zero = jnp.zeros((1, LANES), F32)
    sf, sb = lax.fori_loop(0, SEQ // FILT_ROWS, taps, (zero, zero))
    inv = 1.0 / (sf + sb)

    def scale(i, carry):
        rows = pl.ds(pl.multiple_of(i * FILT_ROWS, FILT_ROWS), FILT_ROWS)
        hf_ref[rows, :] = hf_ref[rows, :] * inv
        hb_ref[rows, :] = hb_ref[rows, :] * inv
        return carry

    lax.fori_loop(0, SEQ // FILT_ROWS, scale, 0)


def _filter_constants():
    t = np.linspace(0.0, 1.0, SEQ, dtype=np.float32).astype(np.float64)[:, None]
    bands = (EMB - 1) // 2
    w = 2.0 * math.pi * np.arange(SEQ, dtype=np.float64)[:, None] / SEQ
    f = np.linspace(1e-4, bands - 1, bands, dtype=np.float32).astype(np.float64)[None, :]
    z = np.concatenate([t, np.cos(f * w), -np.sin(f * w)], axis=-1)
    z_pad = np.zeros((SEQ, LANES), np.float32)
    z_pad[:, :EMB] = z
    t_lanes = np.broadcast_to(t.astype(np.float32), (SEQ, LANES)).copy()
    deltas = np.abs(np.linspace(math.log(DECAY_TARGET) / FAST_DECAY,
                                math.log(DECAY_TARGET) / SLOW_DECAY, D_B, dtype=np.float32))
    return z_pad, t_lanes, deltas.reshape(1, D_B)


def hyena_filter(f_w1, f_b1, f_w2, f_b2, f_w3, f_b3, f_wout, f_freq):
    z_pad, t_lanes, deltas = _filter_constants()
    w1_pad = jnp.zeros((LANES, FO), F32).at[:EMB].set(f_w1)
    nblk = D_B // LANES
    const = lambda shape: pl.BlockSpec(shape, lambda c: (0,) * len(shape))
    return pl.pallas_call(
        _filter_kernel,
        out_shape=[jax.ShapeDtypeStruct((SEQ, D_B), F32)] * 2,
        grid=(nblk,),
        in_specs=[
            const((SEQ, LANES)), const((SEQ, LANES)),
            const((LANES, FO)), const((1, FO)),
            const((FO, FO)), const((1, FO)),
            const((FO, FO)), const((1, FO)),
            const((1, FO)),
            pl.BlockSpec((FO, LANES), lambda c: (0, c)),
            pl.BlockSpec((FO, LANES), lambda c: (0, nblk + c)),
            pl.BlockSpec((1, LANES), lambda c: (0, c)),
        ],
        out_specs=[pl.BlockSpec((SEQ, LANES), lambda c: (0, c))] * 2,
        scratch_shapes=[pltpu.VMEM((SEQ, FO), F32)],
        compiler_params=_cparams(("arbitrary",)),
        name="hyena_filter",
    )(jnp.asarray(z_pad), jnp.asarray(t_lanes), w1_pad, f_b1.reshape(1, FO), f_w2,
      f_b2.reshape(1, FO), f_w3, f_b3.reshape(1, FO), f_freq.reshape(1, FO), f_wout, f_wout,
      jnp.asarray(deltas))


ROW_TILE = 8
N_K1_PAD = N_K1 + 1


@functools.lru_cache(maxsize=None)
def _dft_tables():
    k1 = np.arange(N_K1_PAD)
    n2 = np.arange(N_FAST)
    ang = 2.0 * math.pi * np.outer(k1, n2) / N_FFT
    tw = np.concatenate([np.cos(ang), -np.sin(ang)], axis=1)
    tw = np.broadcast_to(tw[:, :, None], (N_K1_PAD, 2 * N_FAST, LANES))
    ang2 = 2.0 * math.pi * np.outer(n2, n2) / N_FAST
    fr, fi = np.cos(ang2), -np.sin(ang2)
    f_fwd = np.block([[fr, -fi], [fi, fr]])
    f_inv = np.block([[fr, fi], [-fi, fr]])
    f32 = lambda a: np.ascontiguousarray(a, dtype=np.float32)
    return f32(tw), f32(f_fwd), f32(f_inv)


def _cmul_const(x, w):
    xr, xi = x
    c, s = w.real, w.imag
    eps = 1e-12
    if abs(s) < eps and abs(c - 1.0) < eps:
        return xr, xi
    if abs(c) < eps and abs(s - 1.0) < eps:
        return ("neg", xi), xr
    if abs(c) < eps and abs(s + 1.0) < eps:
        return xi, ("neg", xr)
    return xr * c - xi * s, xr * s + xi * c


def _is_neg(v):
    return isinstance(v, tuple) and v[0] == "neg"


def _plain(v):
    return -v[1] if _is_neg(v) else v


def _add(a, b):
    return a - b[1] if _is_neg(b) else a + b


def _sub(a, b):
    return a + b[1] if _is_neg(b) else a - b


def _fft_dit(xs, sign):
    n = len(xs)
    if n == 1:
        return xs
    ev = _fft_dit(xs[0::2], sign)
    od = _fft_dit(xs[1::2], sign)
    out = [None] * n
    for k in range(n // 2):
        w = complex(math.cos(2 * math.pi * k / n), sign * math.sin(2 * math.pi * k / n))
        tr, ti = _cmul_const(od[k], w)
        er, ei = ev[k]
        out[k] = (_add(er, tr), _add(ei, ti))
        out[k + n // 2] = (_sub(er, tr), _sub(ei, ti))
    return out


def _slow_forward(x):
    half, quarter = N_SLOW // 2, N_SLOW // 4
    z = [(x[2 * m], x[2 * m + 1]) for m in range(quarter)]
    zt = [z[0]]
    for m in range(1, quarter):
        tr, ti = _cmul_const(z[m], complex(math.cos(2 * math.pi * m / half), -math.sin(2 * math.pi * m / half)))
        zt.append((_plain(tr), _plain(ti)))
    ze = _fft_dit(z, -1)
    zo = _fft_dit(zt, -1)
    zc = [None] * half
    for j in range(quarter):
        zc[2 * j] = ze[j]
        zc[2 * j + 1] = zo[j]
    out = [None] * (half + 1)
    ar, ai = zc[0]
    out[0] = ((ar + ai) * 2.0, None)
    out[half] = ((ar - ai) * 2.0, None)
    ar, ai = zc[quarter]
    out[quarter] = (ar * 2.0, ai * -2.0)
    for k in range(1, quarter):
        ar, ai = zc[k]
        br, bi = zc[half - k]
        c, s = math.cos(2 * math.pi * k / N_SLOW), math.sin(2 * math.pi * k / N_SLOW)
        p, q, r, t = ar + br, ai - bi, ai + bi, ar - br
        u = r * c - t * s
        vn = t * (-c) - r * s
        out[k] = (p + u, q + vn)
        out[half - k] = (p - u, vn - q)
    return out


def _slow_inverse(xs):
    half, quarter = N_SLOW // 2, N_SLOW // 4
    zc = [None] * half
    ar, br = xs[0][0], xs[half][0]
    zc[0] = (ar + br, ar - br)
    ar, ai = xs[quarter]
    zc[quarter] = (ar * 2.0, ai * -2.0)
    for k in range(1, quarter):
        ar, ai = xs[k]
        br, bi = xs[half - k]
        c, s = math.cos(2 * math.pi * k / N_SLOW), math.sin(2 * math.pi * k / N_SLOW)
        p, q, t, r = ar + br, ai - bi, ar - br, ai + bi
        o_r = t * c - r * s
        o_i = t * s + r * c
        zc[k] = (p - o_i, q + o_r)
        zc[half - k] = (p + o_i, o_r - q)
    ze = _fft_dit(zc[0::2], +1)
    zo = _fft_dit(zc[1::2], +1)
    out = [None] * N_IN
    for m in range(quarter):
        tr, ti = _cmul_const(zo[m], complex(math.cos(2 * math.pi * m / half), math.sin(2 * math.pi * m / half)))
        out[2 * m] = _add(ze[m][0], tr)
        out[2 * m + 1] = _add(ze[m][1], ti)
    return out


def _slow_forward_pass(src_ref, a_ref):
    lanes = src_ref.shape[-1]
    zeros = jnp.zeros((ROW_TILE, lanes), F32)

    def body(j, carry):
        rows = pl.ds(pl.multiple_of(j * ROW_TILE, ROW_TILE), ROW_TILE)
        rows_im = pl.ds(pl.multiple_of(N_FAST + j * ROW_TILE, ROW_TILE), ROW_TILE)
        out = _slow_forward([src_ref[n1, rows, :] for n1 in range(N_IN)])
        for k1, (re, im) in enumerate(out):
            a_ref[k1, rows, :] = re
            a_ref[k1, rows_im, :] = zeros if im is None else im
        a_ref[N_K1, rows, :] = zeros
        a_ref[N_K1, rows_im, :] = zeros
        return carry

    lax.fori_loop(0, N_FAST // ROW_TILE, body, 0)


def _cmul(a, w):
    ar, ai, wr, wi = a[:N_FAST], a[N_FAST:], w[:N_FAST], w[N_FAST:]
    return ar * wr - ai * wi, ar * wi + ai * wr


def _cmul_conj(a, w):
    ar, ai, wr, wi = a[:N_FAST], a[N_FAST:], w[:N_FAST], w[N_FAST:]
    return ar * wr + ai * wi, ai * wr - ar * wi


def _spectrum_kernel(hf_ref, hb_ref, tw_ref, f_ref, o_ref, af_ref, ab_ref):
    _slow_forward_pass(hf_ref, af_ref)
    _slow_forward_pass(hb_ref, ab_ref)

    def body(k1, carry):
        tw = tw_ref[k1]
        f = f_ref[...]
        both = jnp.concatenate([jnp.concatenate(_cmul(af_ref[k1], tw), axis=0),
                                jnp.concatenate(_cmul(ab_ref[k1], tw), axis=0)], axis=1)
        x = _dot3(f, both)
        xf, xb = x[:, :LANES], x[:, LANES:]
        re = xf[:N_FAST] + xb[:N_FAST]
        im = xf[N_FAST:] - xb[N_FAST:]
        o_ref[k1] = jnp.concatenate([re, im], axis=0) * (1.0 / (4 * N_FFT))
        return carry

    lax.fori_loop(0, N_K1_PAD, body, 0)


def filter_spectrum(hf, hb, tw, f_fwd):
    nblk = D_B // LANES
    slabs = lambda a: a.reshape(N_IN, N_FAST, D_B)
    whole = pl.BlockSpec(memory_space=pltpu.VMEM)
    spec_blk = (N_K1_PAD, 2 * N_FAST, LANES)
    return pl.pallas_call(
        _spectrum_kernel,
        out_shape=jax.ShapeDtypeStruct((N_K1_PAD, 2 * N_FAST, D_B), F32),
        grid=(nblk,),
        in_specs=[
            pl.BlockSpec((N_IN, N_FAST, LANES), lambda c: (0, 0, c)),
            pl.BlockSpec((N_IN, N_FAST, LANES), lambda c: (0, 0, c)),
            whole, whole,
        ],
        out_specs=pl.BlockSpec(spec_blk, lambda c: (0, 0, c)),
        scratch_shapes=[pltpu.VMEM(spec_blk, F32)] * 2,
        compiler_params=_cparams(("arbitrary",)),
        name="filter_spectrum",
    )(slabs(hf), slabs(hb), tw, f_fwd)


def _short_conv_slab(x_ref, w_ref, b_ref, n1):
    r0 = pl.multiple_of(n1 * N_FAST, N_FAST)
    edge = 16
    cur = x_ref[0, pl.ds(r0, N_FAST), :].astype(F32)
    before = x_ref[0, pl.ds(pl.multiple_of(jnp.maximum(r0 - edge, 0), edge), edge), :]
    after = x_ref[0, pl.ds(pl.multiple_of(jnp.minimum(r0 + N_FAST, SEQ - edge), edge), edge), :]
    before = jnp.where(n1 > 0, before.astype(F32)[edge - 1:edge], 0.0)
    after = jnp.where(n1 < N_IN - 1, after.astype(F32)[0:1], 0.0)
    row = lax.broadcasted_iota(jnp.int32, (ROW_TILE, cur.shape[1]), 0)
    prev = pltpu.roll(cur, 1, axis=0)
    prev = jnp.concatenate([jnp.where(row == 0, before, prev[:ROW_TILE]), prev[ROW_TILE:]], axis=0)
    nxt = pltpu.roll(cur, N_FAST - 1, axis=0)
    nxt = jnp.concatenate([nxt[:-ROW_TILE], jnp.where(row == ROW_TILE - 1, after, nxt[-ROW_TILE:])], axis=0)
    return prev * w_ref[0:1, :] + cur * w_ref[1:2, :] + nxt * w_ref[2:3, :] + b_ref[...]


def _hyena_kernel(x0_ref, x1_ref, v_ref, w0_ref, w1_ref, wv_ref, b0_ref, b1_ref, bv_ref,
                  hs_ref, hbias_ref, tw_ref, ff_ref, fi_ref, o_ref, u_ref, g0_ref, a_ref, z_ref):
    def conv(n1, carry):
        u_ref[n1] = _short_conv_slab(v_ref, wv_ref, bv_ref, n1) * _short_conv_slab(x1_ref, w1_ref, b1_ref, n1)
        g0_ref[n1] = _short_conv_slab(x0_ref, w0_ref, b0_ref, n1)
        return carry

    lax.fori_loop(0, N_IN, conv, 0)
    _slow_forward_pass(u_ref, a_ref)

    def freq(i, carry):
        ks = (2 * i, 2 * i + 1)
        tws = [tw_ref[k] for k in ks]
        a = [jnp.concatenate(_cmul(a_ref[k], tw), axis=0).astype(BF16) for k, tw in zip(ks, tws)]
        x = jnp.dot(ff_ref[...], jnp.concatenate(a, axis=1), preferred_element_type=F32)
        y = [jnp.concatenate(_cmul(x[:, j * LANES:(j + 1) * LANES], hs_ref[k]), axis=0).astype(BF16)
             for j, k in enumerate(ks)]
        z = jnp.dot(fi_ref[...], jnp.concatenate(y, axis=1), preferred_element_type=F32)
        for j, (k, tw) in enumerate(zip(ks, tws)):
            z_ref[k] = jnp.concatenate(_cmul_conj(z[:, j * LANES:(j + 1) * LANES], tw), axis=0)
        return carry

    lax.fori_loop(0, N_K1_PAD // 2, freq, 0, unroll=True)

    def back(j, carry):
        rows = pl.ds(pl.multiple_of(j * ROW_TILE, ROW_TILE), ROW_TILE)
        rows_im = pl.ds(pl.multiple_of(N_FAST + j * ROW_TILE, ROW_TILE), ROW_TILE)
        y = _slow_inverse([(z_ref[k1, rows, :], z_ref[k1, rows_im, :]) for k1 in range(N_K1)])
        for n1 in range(N_IN):
            u_ref[n1, rows, :] = y[n1] + u_ref[n1, rows, :] * hbias_ref[...]
        return carry

    lax.fori_loop(0, N_FAST // ROW_TILE, back, 0)

    def gate(n1, carry):
        rows = pl.ds(pl.multiple_of(n1 * N_FAST, N_FAST), N_FAST)
        o_ref[0, rows, :] = (u_ref[n1] * g0_ref[n1]).astype(BF16)
        return carry

    lax.fori_loop(0, N_IN, gate, 0)


def hyena(p3, sc_w, sc_b, h_spec, h_bias, tables):
    tw, f_fwd, f_inv = tables
    b = p3.shape[0]
    nblk = D_B // LANES
    base = 3 * D_A // LANES
    xblk = lambda off: pl.BlockSpec((1, SEQ, LANES), lambda c, i: (i, 0, base + off * nblk + c))
    wblk = lambda off: pl.BlockSpec((3, LANES), lambda c, i: (0, off * nblk + c))
    bblk = lambda off: pl.BlockSpec((1, LANES), lambda c, i: (0, off * nblk + c))
    whole = pl.BlockSpec(memory_space=pltpu.VMEM)
    sc_b2 = sc_b.reshape(1, 3 * D_B)
    spec_blk = (N_K1_PAD, 2 * N_FAST, LANES)
    return pl.pallas_call(
        _hyena_kernel,
        out_shape=jax.ShapeDtypeStruct((b, SEQ, D_B), BF16),
        grid=(nblk, b),
        in_specs=[
            xblk(0), xblk(1), xblk(2),
            wblk(0), wblk(1), wblk(2),
            bblk(0), bblk(1), bblk(2),
            pl.BlockSpec(spec_blk, lambda c, i: (0, 0, c)),
            pl.BlockSpec((1, LANES), lambda c, i: (0, c)),
            whole, whole, whole,
        ],
        out_specs=pl.BlockSpec((1, SEQ, LANES), lambda c, i: (i, 0, c)),
        scratch_shapes=[
            pltpu.VMEM((N_IN, N_FAST, LANES), F32),
            pltpu.VMEM((N_IN, N_FAST, LANES), F32),
            pltpu.VMEM(spec_blk, F32),
            pltpu.VMEM(spec_blk, F32),
        ],
        compiler_params=_cparams(("parallel", "arbitrary")),
        name="hyena",
    )(p3, p3, p3, sc_w, sc_w, sc_w, sc_b2, sc_b2, sc_b2,
      h_spec, h_bias.reshape(1, D_B), tw, f_fwd.astype(BF16), f_inv.astype(BF16))


def kernel(x_prompt, x_sample, norm_mix, norm_mlp, w_in_ab, sc_w, sc_b, q_gain, k_gain, rpb,
           f_w1, f_b1, f_w2, f_b2, f_w3, f_b3, f_wout, f_freq, h_bias, w_out_ab, w_in_c,
           v_gain, w_s, b_s, w_out_c, w_mlp1, w_mlp2):
    nb_p, nb_s = x_prompt.shape[0], x_sample.shape[0]
    nb = nb_p + nb_s
    x = (x_prompt.reshape(nb_p * SEQ, D_MODEL), x_sample.reshape(nb_s * SEQ, D_MODEL))
    tables = tuple(jnp.asarray(a) for a in _dft_tables())

    w_in_ab, w_out_ab, w_in_c, w_out_c, w_mlp1, w_mlp2, w_s = (
        w.astype(BF16) for w in (w_in_ab, w_out_ab, w_in_c, w_out_c, w_mlp1, w_mlp2, w_s))

    for i in range(DEPTH):
        j = i // 2
        if i % 2 == 0:
            p3 = in_proj(x, norm_mix[i], w_in_ab, j, n=3 * (D_A + D_B), mode="qk_norm",
                         qk_gains=(q_gain[j], k_gain[j])).reshape(nb, SEQ, 3 * (D_A + D_B))
            a = natten(p3, natten_bias_table(rpb[j]))
            hf, hb = hyena_filter(f_w1[j], f_b1[j], f_w2[j], f_b2[j], f_w3[j], f_b3[j],
                                  f_wout[j], f_freq[j])
            h_spec = filter_spectrum(hf, hb, tables[0], tables[1])
            bo = hyena(p3, sc_w[j], sc_b[j], h_spec, h_bias[j], tables)
            x = (proj_residual(x, a.reshape(nb * SEQ, D_A), bo.reshape(nb * SEQ, D_B), w_out_ab, j),)
        else:
            zz = in_proj(x, norm_mix[i], w_in_c, j, n=2 * D_C, mode="gelu")
            b_lanes = jnp.broadcast_to(b_s[j][:, :, None], (G_C, CHUNK, LANES))
            x = (gate_proj_residual(x[0], zz, v_gain[j], w_s[j], b_lanes, w_out_c, j),)
        if i < DEPTH - 1:
            x = (mlp_residual(x[0], norm_mlp[i], w_mlp1, w_mlp2, i),)

    y_p, y_s = mlp_residual(x[0], norm_mlp[DEPTH - 1], w_mlp1, w_mlp2, DEPTH - 1,
                            out_rows=(nb_p * SEQ, nb_s * SEQ))
    return y_p.reshape(nb_p, SEQ, D_MODEL), y_s.reshape(nb_s, SEQ, D_MODEL)
```

```python
import functools
import math

import numpy as np
import jax
import jax.numpy as jnp
from jax import lax
from jax.experimental import pallas as pl
from jax.experimental.pallas import tpu as pltpu

F32 = jnp.float32
BF16 = jnp.bfloat16

D_MODEL = 2048
DEPTH = 4
SEQ = 4096
GRID_W = 64
WIN_H = 8
WIN_W = 16
DH = 128
H_A = 8
D_A = H_A * DH
D_B = 1024
EMB = 33
FO = 64
FAST_DECAY = 0.3
SLOW_DECAY = 1.5
DECAY_TARGET = 1e-2
D_C = D_MODEL
G_C = 8
DG_C = D_C // G_C
CHUNK = 128
D_FF = 4 * D_MODEL
EPS = 1e-6

LANES = 128
VMEM_LIMIT_BYTES = 52 * 1024 * 1024

N_FFT = 2 * SEQ
N_FAST = 128
N_SLOW = N_FFT // N_FAST
N_IN = SEQ // N_FAST
N_K1 = N_SLOW // 2 + 1


def _cparams(sem, vmem_limit_bytes=VMEM_LIMIT_BYTES):
    return pltpu.CompilerParams(dimension_semantics=sem, vmem_limit_bytes=vmem_limit_bytes)


def _tile_ranges(parts, tm):
    ranges, lo = [], 0
    for p in parts:
        ranges.append((lo, lo + p.shape[0] // tm))
        lo = ranges[-1][1]
    return ranges


def _part_spec(block, rng, col_fn, cols_outer=False):
    lo, hi = rng

    def index(i, j):
        return jnp.clip(i - lo, 0, hi - lo - 1), jnp.where(_in_range(i, rng), col_fn(j), 0)

    return pl.BlockSpec(block, (lambda j, i: index(i, j)) if cols_outer else index)


def _in_range(i, rng):
    return (i >= rng[0]) & (i < rng[1])


def _rms_scale(x, g):
    return x * lax.rsqrt(jnp.mean(x * x, axis=-1, keepdims=True) + EPS) * g


def _gelu_tanh(x):
    return 0.5 * x * (1.0 + jnp.tanh(math.sqrt(2.0 / math.pi) * (x + 0.044715 * (x * x * x))))


def _split_bf16(x):
    hi = x.astype(BF16)
    lo = (x - hi.astype(F32)).astype(BF16)
    return hi, lo


def _dot3(a, b):
    ah, al = _split_bf16(a)
    bh, bl = _split_bf16(b)
    d = functools.partial(jnp.dot, preferred_element_type=F32)
    return d(ah, bh) + (d(ah, bl) + d(al, bh))


def _in_proj_kernel(*refs, mode, ranges):
    x_refs, refs = refs[:len(ranges)], refs[len(ranges):]
    if mode == "gelu":
        g_ref, w_ref, o_ref, h_ref = refs
    else:
        g_ref, w_ref, qg_ref, kg_ref, o_ref, h_ref = refs
    i, j = pl.program_id(0), pl.program_id(1)

    def tile(rows):
        acc = jnp.dot(h_ref[rows, :], w_ref[...], preferred_element_type=F32)
        if mode == "gelu":
            o_ref[rows, :] = _gelu_tanh(acc).astype(BF16)
        else:
            gain = jnp.where(j == 0, qg_ref[...] * (DH ** -0.5), kg_ref[...])
            for c in range(0, o_ref.shape[1], DH):
                seg = acc[:, c:c + DH]
                o_ref[rows, c:c + DH] = jnp.where(j < 2, _rms_scale(seg, gain), seg).astype(BF16)

    half = o_ref.shape[0] // 2
    for x_ref, rng in zip(x_refs, ranges):
        @pl.when((j == 0) & _in_range(i, rng))
        def _():
            for rows in (slice(0, half), slice(half, 2 * half)):
                h_ref[rows, :] = _rms_scale(x_ref[rows, :], g_ref[...]).astype(BF16)
                tile(rows)

    @pl.when(j > 0)
    def _():
        tile(slice(None))


def in_proj(x_parts, g, w, layer, *, n, mode, qk_gains=(), tn=1024):
    d = x_parts[0].shape[1]
    tm = 1024
    t = sum(p.shape[0] for p in x_parts)
    x_bytes = len(x_parts) * 2 * tm * d * 4
    other_bytes = 2 * d * tn * 2 + 2 * tm * tn * 2 + tm * d * 2 + tm * tn * 4
    vmem_limit = max(VMEM_LIMIT_BYTES, x_bytes + other_bytes + (4 << 20))
    ranges = _tile_ranges(x_parts, tm)
    return pl.pallas_call(
        functools.partial(_in_proj_kernel, mode=mode, ranges=ranges),
        out_shape=jax.ShapeDtypeStruct((t, n), BF16),
        grid=(t // tm, n // tn),
        in_specs=[_part_spec((tm, d), rng, lambda j: 0) for rng in ranges] + [
            pl.BlockSpec((1, d), lambda i, j: (0, 0)),
            pl.BlockSpec((None, d, tn), lambda i, j: (layer, 0, j)),
        ] + [pl.BlockSpec((1, DH), lambda i, j: (0, 0))] * len(qk_gains),
        out_specs=pl.BlockSpec((tm, tn), lambda i, j: (i, j)),
        scratch_shapes=[pltpu.VMEM((tm, d), BF16)],
        compiler_params=_cparams(("parallel", "arbitrary"), vmem_limit),
        name="in_proj",
    )(*x_parts, g.reshape(1, d), w, *[gain.reshape(1, DH) for gain in qk_gains])


def _mlp_kernel(x_ref, g_ref, w1_ref, w2_ref, *refs, ranges):
    o_refs, h_ref = refs[:-1], refs[-1]

    def run(o_ref):
        def tile(rows, first):
            a = jnp.dot(h_ref[rows, :], w1_ref[...], preferred_element_type=F32)
            a = jnp.square(jnp.maximum(a, 0.0)).astype(BF16)
            y = jnp.dot(a, w2_ref[...], preferred_element_type=F32)
            o_ref[rows, :] = (x_ref[rows, :] if first else o_ref[rows, :]) + y

        half = o_ref.shape[0] // 2

        @pl.when(pl.program_id(1) == 0)
        def _():
            for rows in (slice(0, half), slice(half, 2 * half)):
                h_ref[rows, :] = _rms_scale(x_ref[rows, :], g_ref[...]).astype(BF16)
                tile(rows, True)

        @pl.when(pl.program_id(1) > 0)
        def _():
            tile(slice(None), False)

    if len(o_refs) == 1:
        run(o_refs[0])
    else:
        for o_ref, rng in zip(o_refs, ranges):
            pl.when(_in_range(pl.program_id(0), rng))(functools.partial(run, o_ref))


def mlp_residual(x, g, w1, w2, layer, *, out_rows=None):
    t, d = x.shape
    f = w1.shape[2]
    out_rows = (t,) if out_rows is None else out_rows
    tm, tf = 512, 1024
    outs = [jax.ShapeDtypeStruct((r, d), F32) for r in out_rows]
    ranges = _tile_ranges(outs, tm)
    res = pl.pallas_call(
        functools.partial(_mlp_kernel, ranges=ranges),
        out_shape=outs,
        grid=(t // tm, f // tf),
        in_specs=[
            pl.BlockSpec((tm, d), lambda i, k: (i, 0)),
            pl.BlockSpec((1, d), lambda i, k: (0, 0)),
            pl.BlockSpec((None, d, tf), lambda i, k: (layer, 0, k)),
            pl.BlockSpec((None, tf, d), lambda i, k: (layer, k, 0)),
        ],
        out_specs=[_part_spec((tm, d), rng, lambda k: 0) for rng in ranges],
        scratch_shapes=[pltpu.VMEM((tm, d), BF16)],
        compiler_params=_cparams(("arbitrary", "arbitrary")),
        name="mlp_residual",
    )(x, g.reshape(1, d), w1, w2)
    return res[0] if len(res) == 1 else res


def _proj_residual_kernel(*refs, ranges):
    x_refs, (a_ref, b_ref, wa_ref, wb_ref, o_ref) = refs[:len(ranges)], refs[len(ranges):]
    acc = jnp.dot(a_ref[...], wa_ref[...], preferred_element_type=F32)
    acc += jnp.dot(b_ref[...], wb_ref[...], preferred_element_type=F32)
    x = x_refs[0][...]
    for x_ref, rng in zip(x_refs[1:], ranges[1:]):
        x = jnp.where(pl.program_id(1) >= rng[0], x_ref[...], x)
    o_ref[...] = x + acc


def proj_residual(x_parts, a, b, w, layer, *, tm=1024, tn=1024):
    n = x_parts[0].shape[1]
    t = sum(p.shape[0] for p in x_parts)
    ranges = _tile_ranges(x_parts, tm)
    k = w.shape[1] // 2
    return pl.pallas_call(
        functools.partial(_proj_residual_kernel, ranges=ranges),
        out_shape=jax.ShapeDtypeStruct((t, n), F32),
        grid=(n // tn, t // tm),
        in_specs=[_part_spec((tm, tn), rng, lambda j: j, cols_outer=True) for rng in ranges] + [
            pl.BlockSpec((tm, k), lambda j, i: (i, 0)),
            pl.BlockSpec((tm, k), lambda j, i: (i, 0)),
            pl.BlockSpec((None, k, tn), lambda j, i: (layer, 0, j)),
            pl.BlockSpec((None, k, tn), lambda j, i: (layer, 1, j)),
        ],
        out_specs=pl.BlockSpec((tm, tn), lambda j, i: (i, j)),
        compiler_params=_cparams(("arbitrary", "arbitrary")),
        name="proj_residual",
    )(*x_parts, a, b, w, w)


def _gate_proj_kernel(x_ref, zz_ref, vg_ref, ws_ref, bs_ref, w_ref, o_ref, vn_ref, m_ref, *, tm):
    for c in range(tm // CHUNK):
        rows = slice(c * CHUNK, (c + 1) * CHUNK)
        vn_ref[rows, :] = _rms_scale(zz_ref[rows, D_C:].astype(F32), vg_ref[...]).astype(BF16)
        for g in range(G_C):
            cols = slice(g * DG_C, (g + 1) * DG_C)
            s = jnp.dot(ws_ref[g], vn_ref[rows, cols], preferred_element_type=F32)
            bias = bs_ref[g]
            s = s + jnp.concatenate([bias] * (DG_C // LANES), axis=1)
            m_ref[rows, cols] = (zz_ref[rows, cols].astype(F32) * s).astype(BF16)
    o_ref[...] = x_ref[...] + jnp.dot(m_ref[...], w_ref[...], preferred_element_type=F32)


def gate_proj_residual(x, zz, v_gain, w_s, b_s_lanes, w, layer, *, tm=512):
    t = zz.shape[0]
    return pl.pallas_call(
        functools.partial(_gate_proj_kernel, tm=tm),
        out_shape=jax.ShapeDtypeStruct((t, D_MODEL), F32),
        grid=(t // tm,),
        in_specs=[
            pl.BlockSpec((tm, D_MODEL), lambda i: (i, 0)),
            pl.BlockSpec((tm, 2 * D_C), lambda i: (i, 0)),
            pl.BlockSpec((1, D_C), lambda i: (0, 0)),
            pl.BlockSpec((G_C, CHUNK, CHUNK), lambda i: (0, 0, 0)),
            pl.BlockSpec((G_C, CHUNK, LANES), lambda i: (0, 0, 0)),
            pl.BlockSpec((None, D_C, D_MODEL), lambda i: (layer, 0, 0)),
        ],
        out_specs=pl.BlockSpec((tm, D_MODEL), lambda i: (i, 0)),
        scratch_shapes=[pltpu.VMEM((tm, D_C), BF16), pltpu.VMEM((tm, D_C), BF16)],
        compiler_params=_cparams(("parallel",)),
        name="gate_proj_residual",
    )(x, zz, v_gain.reshape(1, D_C), w_s, b_s_lanes, w)


N_ROWS = SEQ // GRID_W
GROUP_ROWS = 4
N_GROUPS = N_ROWS // GROUP_ROWS
KEY_ROWS = 12
NQ = GROUP_ROWS * GRID_W
NK = KEY_ROWS * GRID_W
SOFTMAX_ROWS = 64


def _natten_kernel(q_ref, k_ref, v_ref, bias_ref, o_ref, s_ref, p_ref):
    def q_rows(g):
        return pl.ds(g * NQ, NQ)

    def k_rows(g):
        ws = min(max(GROUP_ROWS * g - WIN_H // 2, 0), N_ROWS - KEY_ROWS)
        return pl.ds(ws * GRID_W, NK)

    def scores(g, slot):
        s_ref[slot] = lax.dot_general(q_ref[0, q_rows(g), :], k_ref[0, k_rows(g), :],
                                      (((1,), (1,)), ((), ())), preferred_element_type=F32)

    def softmax(g, slot):
        kind = 0 if g == 0 else (2 if g == N_GROUPS - 1 else 1)
        for c in range(NQ // SOFTMAX_ROWS):
            rows = slice(c * SOFTMAX_ROWS, (c + 1) * SOFTMAX_ROWS)
            s = s_ref[slot, rows, :] + bias_ref[0, kind, rows, :]
            p = jnp.exp(s - jnp.max(s, axis=-1, keepdims=True))
            p = p * (1.0 / jnp.sum(p, axis=-1, keepdims=True))
            p_ref[slot, rows, :] = p.astype(BF16)

    def values(g, slot):
        o = jnp.dot(p_ref[slot], v_ref[0, k_rows(g), :], preferred_element_type=F32)
        o_ref[0, q_rows(g), :] = o.astype(BF16)

    for t in range(N_GROUPS + 2):
        if t < N_GROUPS:
            scores(t, t % 2)
        if 1 <= t <= N_GROUPS:
            softmax(t - 1, (t - 1) % 2)
        if t >= 2:
            values(t - 2, t % 2)


def natten(qkv3, bias_tab):
    b = qkv3.shape[0]
    blk = (1, SEQ, DH)
    return pl.pallas_call(
        _natten_kernel,
        out_shape=jax.ShapeDtypeStruct((b, SEQ, D_A), BF16),
        grid=(H_A, b),
        in_specs=[
            pl.BlockSpec(blk, lambda h, i: (i, 0, h)),
            pl.BlockSpec(blk, lambda h, i: (i, 0, H_A + h)),
            pl.BlockSpec(blk, lambda h, i: (i, 0, 2 * H_A + h)),
            pl.BlockSpec((1, 3, NQ, NK), lambda h, i: (h, 0, 0, 0)),
        ],
        out_specs=pl.BlockSpec(blk, lambda h, i: (i, 0, h)),
        scratch_shapes=[pltpu.VMEM((2, NQ, NK), F32), pltpu.VMEM((2, NQ, NK), BF16)],
        compiler_params=_cparams(("parallel", "arbitrary")),
        name="natten",
    )(qkv3, qkv3, qkv3, bias_tab)


def natten_bias_table(rpb):
    col = np.arange(GRID_W)
    cs = np.clip(col - WIN_W // 2, 0, GRID_W - WIN_W)
    qc, kc = col[:, None], col[None, :]
    col_ok = (kc >= cs[:, None]) & (kc < cs[:, None] + WIN_W)
    dc = np.clip(kc - qc, -(WIN_W - 1), WIN_W - 1) + WIN_W - 1
    bias_c = rpb.astype(F32)[:, :, dc]
    a = np.arange(GROUP_ROWS)
    q_off = np.stack([a, a + WIN_H // 2, a + KEY_ROWS - GROUP_ROWS])
    w_off = np.stack([0 * a, a, 0 * a + KEY_ROWS - WIN_H])
    i = np.arange(KEY_ROWS)
    row_ok = (i >= w_off[..., None]) & (i < w_off[..., None] + WIN_H)
    dr = np.clip(i - q_off[..., None] + WIN_H - 1, 0, 2 * WIN_H - 2)
    tab = bias_c[:, dr]
    ok = row_ok[None, :, :, :, None, None] & col_ok[None, None, None, None]
    tab = jnp.where(ok, tab, -jnp.inf)
    return tab.transpose(0, 1, 2, 4, 3, 5).reshape(H_A, 3, NQ, NK)


FILT_ROWS = 512


def _filter_kernel(z_ref, t_ref, w1_ref, b1_ref, w2_ref, b2_ref, w3_ref, b3_ref, fr_ref,
                   wf_ref, wb_ref, dl_ref, hf_ref, hb_ref, h3_ref):
    @pl.when(pl.program_id(0) == 0)
    def _():
        def mlp(i, carry):
            rows = pl.ds(pl.multiple_of(i * FILT_ROWS, FILT_ROWS), FILT_ROWS)
            fr = fr_ref[...]
            h = jnp.sin(fr * (_dot3(z_ref[rows, :], w1_ref[...]) + b1_ref[...]))
            h = jnp.sin(fr * (_dot3(h, w2_ref[...]) + b2_ref[...]))
            h3_ref[rows, :] = jnp.sin(fr * (_dot3(h, w3_ref[...]) + b3_ref[...]))
            return carry

        lax.fori_loop(0, SEQ // FILT_ROWS, mlp, 0)

    def taps(i, carry):
        sf, sb = carry
        rows = pl.ds(pl.multiple_of(i * FILT_ROWS, FILT_ROWS), FILT_ROWS)
        h3 = h3_ref[rows, :]
        decay = jnp.exp(-t_ref[rows, :] * dl_ref[...])
        hf = _dot3(h3, wf_ref[...]) * decay
        hb = _dot3(h3, wb_ref[...]) * decay
        t_idx = lax.broadcasted_iota(jnp.int32, hb.shape, 0) + i * FILT_ROWS
        hb = jnp.where(t_idx == 0, 0.0, hb)
        hf_ref[rows, :] = hf
        hb_ref[rows, :] = hb
        return (sf + jnp.sum(jnp.abs(hf), axis=0, keepdims=True),
                sb + jnp.sum(jnp.abs(hb), axis=0, keepdims=True))

    zero = jnp.zeros((1, LANES), F32)
    sf, sb = lax.fori_loop(0, SEQ // FILT_ROWS, taps, (zero, zero))
    inv = 1.0 / (sf + sb)

    def scale(i, carry):
        rows = pl.ds(pl.multiple_of(i * FILT_ROWS, FILT_ROWS), FILT_ROWS)
        hf_ref[rows, :] = hf_ref[rows, :] * inv
        hb_ref[rows, :] = hb_ref[rows, :] * inv
        return carry

    lax.fori_loop(0, SEQ // FILT_ROWS, scale, 0)


def _filter_constants():
    t = np.linspace(0.0, 1.0, SEQ, dtype=np.float32).astype(np.float64)[:, None]
    bands = (EMB - 1) // 2
    w = 2.0 * math.pi * np.arange(SEQ, dtype=np.float64)[:, None] / SEQ
    f = np.linspace(1e-4, bands - 1, bands, dtype=np.float32).astype(np.float64)[None, :]
    z = np.concatenate([t, np.cos(f * w), -np.sin(f * w)], axis=-1)
    z_pad = np.zeros((SEQ, LANES), np.float32)
    z_pad[:, :EMB] = z
    t_lanes = np.broadcast_to(t.astype(np.float32), (SEQ, LANES)).copy()
    deltas = np.abs(np.linspace(math.log(DECAY_TARGET) / FAST_DECAY,
                                math.log(DECAY_TARGET) / SLOW_DECAY, D_B, dtype=np.float32))
    return z_pad, t_lanes, deltas.reshape(1, D_B)


def hyena_filter(f_w1, f_b1, f_w2, f_b2, f_w3, f_b3, f_wout, f_freq):
    z_pad, t_lanes, deltas = _filter_constants()
    w1_pad = jnp.zeros((LANES, FO), F32).at[:EMB].set(f_w1)
    nblk = D_B // LANES
    const = lambda shape: pl.BlockSpec(shape, lambda c: (0,) * len(shape))
    return pl.pallas_call(
        _filter_kernel,
        out_shape=[jax.ShapeDtypeStruct((SEQ, D_B), F32)] * 2,
        grid=(nblk,),
        in_specs=[
            const((SEQ, LANES)), const((SEQ, LANES)),
            const((LANES, FO)), const((1, FO)),
            const((FO, FO)), const((1, FO)),
            const((FO, FO)), const((1, FO)),
            const((1, FO)),
            pl.BlockSpec((FO, LANES), lambda c: (0, c)),
            pl.BlockSpec((FO, LANES), lambda c: (0, nblk + c)),
            pl.BlockSpec((1, LANES), lambda c: (0, c)),
        ],
        out_specs=[pl.BlockSpec((SEQ, LANES), lambda c: (0, c))] * 2,
        scratch_shapes=[pltpu.VMEM((SEQ, FO), F32)],
        compiler_params=_cparams(("arbitrary",)),
        name="hyena_filter",
    )(jnp.asarray(z_pad), jnp.asarray(t_lanes), w1_pad, f_b1.reshape(1, FO), f_w2,
      f_b2.reshape(1, FO), f_w3, f_b3.reshape(1, FO), f_freq.reshape(1, FO), f_wout, f_wout,
      jnp.asarray(deltas))


ROW_TILE = 8
N_K1_PAD = N_K1 + 1


@functools.lru_cache(maxsize=None)
def _dft_tables():
    k1 = np.arange(N_K1_PAD)
    n2 = np.arange(N_FAST)
    ang = 2.0 * math.pi * np.outer(k1, n2) / N_FFT
    tw = np.concatenate([np.cos(ang), -np.sin(ang)], axis=1)
    tw = np.broadcast_to(tw[:, :, None], (N_K1_PAD, 2 * N_FAST, LANES))
    ang2 = 2.0 * math.pi * np.outer(n2, n2) / N_FAST
    fr, fi = np.cos(ang2), -np.sin(ang2)
    f_fwd = np.block([[fr, -fi], [fi, fr]])
    f_inv = np.block([[fr, fi], [-fi, fr]])
    f32 = lambda a: np.ascontiguousarray(a, dtype=np.float32)
    return f32(tw), f32(f_fwd), f32(f_inv)


def _cmul_const(x, w):
    xr, xi = x
    c, s = w.real, w.imag
    eps = 1e-12
    if abs(s) < eps and abs(c - 1.0) < eps:
        return xr, xi
    if abs(c) < eps and abs(s - 1.0) < eps:
        return ("neg", xi), xr
    if abs(c) < eps and abs(s + 1.0) < eps:
        return xi, ("neg", xr)
    return xr * c - xi * s, xr * s + xi * c


def _is_neg(v):
    return isinstance(v, tuple) and v[0] == "neg"


def _plain(v):
    return -v[1] if _is_neg(v) else v


def _add(a, b):
    return a - b[1] if _is_neg(b) else a + b


def _sub(a, b):
    return a + b[1] if _is_neg(b) else a - b


def _fft_dit(xs, sign):
    n = len(xs)
    if n == 1:
        return xs
    ev = _fft_dit(xs[0::2], sign)
    od = _fft_dit(xs[1::2], sign)
    out = [None] * n
    for k in range(n // 2):
        w = complex(math.cos(2 * math.pi * k / n), sign * math.sin(2 * math.pi * k / n))
        tr, ti = _cmul_const(od[k], w)
        er, ei = ev[k]
        out[k] = (_add(er, tr), _add(ei, ti))
        out[k + n // 2] = (_sub(er, tr), _sub(ei, ti))
    return out


def _slow_forward(x):
    half, quarter = N_SLOW // 2, N_SLOW // 4
    z = [(x[2 * m], x[2 * m + 1]) for m in range(quarter)]
    zt = [z[0]]
    for m in range(1, quarter):
        tr, ti = _cmul_const(z[m], complex(math.cos(2 * math.pi * m / half), -math.sin(2 * math.pi * m / half)))
        zt.append((_plain(tr), _plain(ti)))
    ze = _fft_dit(z, -1)
    zo = _fft_dit(zt, -1)
    zc = [None] * half
    for j in range(quarter):
        zc[2 * j] = ze[j]
        zc[2 * j + 1] = zo[j]
    out = [None] * (half + 1)
    ar, ai = zc[0]
    out[0] = ((ar + ai) * 2.0, None)
    out[half] = ((ar - ai) * 2.0, None)
    ar, ai = zc[quarter]
    out[quarter] = (ar * 2.0, ai * -2.0)
    for k in range(1, quarter):
        ar, ai = zc[k]
        br, bi = zc[half - k]
        c, s = math.cos(2 * math.pi * k / N_SLOW), math.sin(2 * math.pi * k / N_SLOW)
        p, q, r, t = ar + br, ai - bi, ai + bi, ar - br
        u = r * c - t * s
        vn = t * (-c) - r * s
        out[k] = (p + u, q + vn)
        out[half - k] = (p - u, vn - q)
    return out


def _slow_inverse(xs):
    half, quarter = N_SLOW // 2, N_SLOW // 4
    zc = [None] * half
    ar, br = xs[0][0], xs[half][0]
    zc[0] = (ar + br, ar - br)
    ar, ai = xs[quarter]
    zc[quarter] = (ar * 2.0, ai * -2.0)
    for k in range(1, quarter):
        ar, ai = xs[k]
        br, bi = xs[half - k]
        c, s = math.cos(2 * math.pi * k / N_SLOW), math.sin(2 * math.pi * k / N_SLOW)
        p, q, t, r = ar + br, ai - bi, ar - br, ai + bi
        o_r = t * c - r * s
        o_i = t * s + r * c
        zc[k] = (p - o_i, q + o_r)
        zc[half - k] = (p + o_i, o_r - q)
    ze = _fft_dit(zc[0::2], +1)
    zo = _fft_dit(zc[1::2], +1)
    out = [None] * N_IN
    for m in range(quarter):
        tr, ti = _cmul_const(zo[m], complex(math.cos(2 * math.pi * m / half), math.sin(2 * math.pi * m / half)))
        out[2 * m] = _add(ze[m][0], tr)
        out[2 * m + 1] = _add(ze[m][1], ti)
    return out


def _slow_forward_pass(src_ref, a_ref):
    lanes = src_ref.shape[-1]
    zeros = jnp.zeros((ROW_TILE, lanes), F32)

    def body(j, carry):
        rows = pl.ds(pl.multiple_of(j * ROW_TILE, ROW_TILE), ROW_TILE)
        rows_im = pl.ds(pl.multiple_of(N_FAST + j * ROW_TILE, ROW_TILE), ROW_TILE)
        out = _slow_forward([src_ref[n1, rows, :] for n1 in range(N_IN)])
        for k1, (re, im) in enumerate(out):
            a_ref[k1, rows, :] = re
            a_ref[k1, rows_im, :] = zeros if im is None else im
        a_ref[N_K1, rows, :] = zeros
        a_ref[N_K1, rows_im, :] = zeros
        return carry

    lax.fori_loop(0, N_FAST // ROW_TILE, body, 0)


def _cmul(a, w):
    ar, ai, wr, wi = a[:N_FAST], a[N_FAST:], w[:N_FAST], w[N_FAST:]
    return ar * wr - ai * wi, ar * wi + ai * wr


def _cmul_conj(a, w):
    ar, ai, wr, wi = a[:N_FAST], a[N_FAST:], w[:N_FAST], w[N_FAST:]
    return ar * wr + ai * wi, ai * wr - ar * wi


def _spectrum_kernel(hf_ref, hb_ref, tw_ref, f_ref, o_ref, af_ref, ab_ref):
    _slow_forward_pass(hf_ref, af_ref)
    _slow_forward_pass(hb_ref, ab_ref)

    def body(k1, carry):
        tw = tw_ref[k1]
        f = f_ref[...]
        both = jnp.concatenate([jnp.concatenate(_cmul(af_ref[k1], tw), axis=0),
                                jnp.concatenate(_cmul(ab_ref[k1], tw), axis=0)], axis=1)
        x = _dot3(f, both)
        xf, xb = x[:, :LANES], x[:, LANES:]
        re = xf[:N_FAST] + xb[:N_FAST]
        im = xf[N_FAST:] - xb[N_FAST:]
        o_ref[k1] = jnp.concatenate([re, im], axis=0) * (1.0 / (4 * N_FFT))
        return carry

    lax.fori_loop(0, N_K1_PAD, body, 0)


def filter_spectrum(hf, hb, tw, f_fwd):
    nblk = D_B // LANES
    slabs = lambda a: a.reshape(N_IN, N_FAST, D_B)
    whole = pl.BlockSpec(memory_space=pltpu.VMEM)
    spec_blk = (N_K1_PAD, 2 * N_FAST, LANES)
    return pl.pallas_call(
        _spectrum_kernel,
        out_shape=jax.ShapeDtypeStruct((N_K1_PAD, 2 * N_FAST, D_B), F32),
        grid=(nblk,),
        in_specs=[
            pl.BlockSpec((N_IN, N_FAST, LANES), lambda c: (0, 0, c)),
            pl.BlockSpec((N_IN, N_FAST, LANES), lambda c: (0, 0, c)),
            whole, whole,
        ],
        out_specs=pl.BlockSpec(spec_blk, lambda c: (0, 0, c)),
        scratch_shapes=[pltpu.VMEM(spec_blk, F32)] * 2,
        compiler_params=_cparams(("arbitrary",)),
        name="filter_spectrum",
    )(slabs(hf), slabs(hb), tw, f_fwd)


def _hyena_kernel(x0_ref, x1_ref, v_ref, w0_ref, w1_ref, wv_ref, b0_ref, b1_ref, bv_ref,
                  hs_ref, hbias_ref, tw_ref, ff_ref, fi_ref, o_ref, u_ref, g0_ref, a_ref, z_ref, st_ref):
    zero_rows = jnp.zeros((ROW_TILE, LANES), F32)
    for op in range(3):
        st_ref[op, 0:ROW_TILE, :] = zero_rows
        st_ref[op, SEQ + ROW_TILE:SEQ + 2 * ROW_TILE, :] = zero_rows

    def stage(n1, carry):
        r0 = pl.multiple_of(n1 * N_FAST, N_FAST)
        for op, x_ref in enumerate((x0_ref, x1_ref, v_ref)):
            st_ref[op, pl.ds(r0 + ROW_TILE, N_FAST), :] = x_ref[0, pl.ds(r0, N_FAST), :].astype(F32)
        return carry

    lax.fori_loop(0, N_IN, stage, 0)

    def tap3(op, w_ref, b_ref, n1):
        r0 = n1 * N_FAST + ROW_TILE
        prev = st_ref[op, pl.ds(r0 - 1, N_FAST), :]
        cur = st_ref[op, pl.ds(r0, N_FAST), :]
        nxt = st_ref[op, pl.ds(r0 + 1, N_FAST), :]
        return prev * w_ref[0:1, :] + cur * w_ref[1:2, :] + nxt * w_ref[2:3, :] + b_ref[...]

    def conv(n1, carry):
        u_ref[n1] = tap3(2, wv_ref, bv_ref, n1) * tap3(1, w1_ref, b1_ref, n1)
        g0_ref[n1] = tap3(0, w0_ref, b0_ref, n1)
        return carry

    lax.fori_loop(0, N_IN, conv, 0)
    _slow_forward_pass(u_ref, a_ref)

    def freq(i, carry):
        ks = (2 * i, 2 * i + 1)
        tws = [tw_ref[k] for k in ks]
        a = [jnp.concatenate(_cmul(a_ref[k], tw), axis=0).astype(BF16) for k, tw in zip(ks, tws)]
        x = jnp.dot(ff_ref[...], jnp.concatenate(a, axis=1), preferred_element_type=F32)
        y = [jnp.concatenate(_cmul(x[:, j * LANES:(j + 1) * LANES], hs_ref[k]), axis=0).astype(BF16)
             for j, k in enumerate(ks)]
        z = jnp.dot(fi_ref[...], jnp.concatenate(y, axis=1), preferred_element_type=F32)
        for j, (k, tw) in enumerate(zip(ks, tws)):
            z_ref[k] = jnp.concatenate(_cmul_conj(z[:, j * LANES:(j + 1) * LANES], tw), axis=0)
        return carry

    lax.fori_loop(0, N_K1_PAD // 2, freq, 0, unroll=True)

    def back(j, carry):
        rows = pl.ds(pl.multiple_of(j * ROW_TILE, ROW_TILE), ROW_TILE)
        rows_im = pl.ds(pl.multiple_of(N_FAST + j * ROW_TILE, ROW_TILE), ROW_TILE)
        y = _slow_inverse([(z_ref[k1, rows, :], z_ref[k1, rows_im, :]) for k1 in range(N_K1)])
        for n1 in range(N_IN):
            u_ref[n1, rows, :] = y[n1] + u_ref[n1, rows, :] * hbias_ref[...]
        return carry

    lax.fori_loop(0, N_FAST // ROW_TILE, back, 0)

    def gate(n1, carry):
        rows = pl.ds(pl.multiple_of(n1 * N_FAST, N_FAST), N_FAST)
        o_ref[0, rows, :] = (u_ref[n1] * g0_ref[n1]).astype(BF16)
        return carry

    lax.fori_loop(0, N_IN, gate, 0)


def hyena(p3, sc_w, sc_b, h_spec, h_bias, tables):
    tw, f_fwd, f_inv = tables
    b = p3.shape[0]
    nblk = D_B // LANES
    base = 3 * D_A // LANES
    xblk = lambda off: pl.BlockSpec((1, SEQ, LANES), lambda c, i: (i, 0, base + off * nblk + c))
    wblk = lambda off: pl.BlockSpec((3, LANES), lambda c, i: (0, off * nblk + c))
    bblk = lambda off: pl.BlockSpec((1, LANES), lambda c, i: (0, off * nblk + c))
    whole = pl.BlockSpec(memory_space=pltpu.VMEM)
    sc_b2 = sc_b.reshape(1, 3 * D_B)
    spec_blk = (N_K1_PAD, 2 * N_FAST, LANES)
    return pl.pallas_call(
        _hyena_kernel,
        out_shape=jax.ShapeDtypeStruct((b, SEQ, D_B), BF16),
        grid=(nblk, b),
        in_specs=[
            xblk(0), xblk(1), xblk(2),
            wblk(0), wblk(1), wblk(2),
            bblk(0), bblk(1), bblk(2),
            pl.BlockSpec(spec_blk, lambda c, i: (0, 0, c)),
            pl.BlockSpec((1, LANES), lambda c, i: (0, c)),
            whole, whole, whole,
        ],
        out_specs=pl.BlockSpec((1, SEQ, LANES), lambda c, i: (i, 0, c)),
        scratch_shapes=[
            pltpu.VMEM((N_IN, N_FAST, LANES), F32),
            pltpu.VMEM((N_IN, N_FAST, LANES), F32),
            pltpu.VMEM(spec_blk, F32),
            pltpu.VMEM(spec_blk, F32),
            pltpu.VMEM((3, SEQ + 2 * ROW_TILE, LANES), F32),
        ],
        compiler_params=_cparams(("parallel", "arbitrary")),
        name="hyena",
    )(p3, p3, p3, sc_w, sc_w, sc_w, sc_b2, sc_b2, sc_b2,
      h_spec, h_bias.reshape(1, D_B), tw, f_fwd.astype(BF16), f_inv.astype(BF16))


def kernel(x_prompt, x_sample, norm_mix, norm_mlp, w_in_ab, sc_w, sc_b, q_gain, k_gain, rpb,
           f_w1, f_b1, f_w2, f_b2, f_w3, f_b3, f_wout, f_freq, h_bias, w_out_ab, w_in_c,
           v_gain, w_s, b_s, w_out_c, w_mlp1, w_mlp2):
    nb_p, nb_s = x_prompt.shape[0], x_sample.shape[0]
    nb = nb_p + nb_s
    x = (x_prompt.reshape(nb_p * SEQ, D_MODEL), x_sample.reshape(nb_s * SEQ, D_MODEL))
    tables = tuple(jnp.asarray(a) for a in _dft_tables())

    w_in_ab, w_out_ab, w_in_c, w_out_c, w_mlp1, w_mlp2, w_s = (
        w.astype(BF16) for w in (w_in_ab, w_out_ab, w_in_c, w_out_c, w_mlp1, w_mlp2, w_s))

    for i in range(DEPTH):
        j = i // 2
        if i % 2 == 0:
            p3 = in_proj(x, norm_mix[i], w_in_ab, j, n=3 * (D_A + D_B), mode="qk_norm",
                         qk_gains=(q_gain[j], k_gain[j])).reshape(nb, SEQ, 3 * (D_A + D_B))
            a = natten(p3, natten_bias_table(rpb[j]))
            hf, hb = hyena_filter(f_w1[j], f_b1[j], f_w2[j], f_b2[j], f_w3[j], f_b3[j],
                                  f_wout[j], f_freq[j])
            h_spec = filter_spectrum(hf, hb, tables[0], tables[1])
            bo = hyena(p3, sc_w[j], sc_b[j], h_spec, h_bias[j], tables)
            x = (proj_residual(x, a.reshape(nb * SEQ, D_A), bo.reshape(nb * SEQ, D_B), w_out_ab, j),)
        else:
            zz = in_proj(x, norm_mix[i], w_in_c, j, n=2 * D_C, mode="gelu")
            b_lanes = jnp.broadcast_to(b_s[j][:, :, None], (G_C, CHUNK, LANES))
            x = (gate_proj_residual(x[0], zz, v_gain[j], w_s[j], b_lanes, w_out_c, j),)
        if i < DEPTH - 1:
            x = (mlp_residual(x[0], norm_mlp[i], w_mlp1, w_mlp2, i),)

    y_p, y_s = mlp_residual(x[0], norm_mlp[DEPTH - 1], w_mlp1, w_mlp2, DEPTH - 1,
                            out_rows=(nb_p * SEQ, nb_s * SEQ))
    return y_p.reshape(nb_p, SEQ, D_MODEL), y_s.reshape(nb_s, SEQ, D_MODEL)
```

```python
import functools
import math

import numpy as np
import jax
import jax.numpy as jnp
from jax import lax
from jax.experimental import pallas as pl
from jax.experimental.pallas import tpu as pltpu

F32 = jnp.float32
BF16 = jnp.bfloat16

D_MODEL = 2048
DEPTH = 4
SEQ = 4096
GRID_W = 64
WIN_H = 8
WIN_W = 16
DH = 128
H_A = 8
D_A = H_A * DH
D_B = 1024
EMB = 33
FO = 64
FAST_DECAY = 0.3
SLOW_DECAY = 1.5
DECAY_TARGET = 1e-2
D_C = D_MODEL
G_C = 8
DG_C = D_C // G_C
CHUNK = 128
D_FF = 4 * D_MODEL
EPS = 1e-6

LANES = 128
VMEM_LIMIT_BYTES = 52 * 1024 * 1024

N_FFT = 2 * SEQ
N_FAST = 128
N_SLOW = N_FFT // N_FAST
N_IN = SEQ // N_FAST
N_K1 = N_SLOW // 2 + 1


def _cparams(sem, vmem_limit_bytes=VMEM_LIMIT_BYTES):
    return pltpu.CompilerParams(dimension_semantics=sem, vmem_limit_bytes=vmem_limit_bytes)


def _tile_ranges(parts, tm):
    ranges, lo = [], 0
    for p in parts:
        ranges.append((lo, lo + p.shape[0] // tm))
        lo = ranges[-1][1]
    return ranges


def _part_spec(block, rng, col_fn, cols_outer=False):
    lo, hi = rng

    def index(i, j):
        return jnp.clip(i - lo, 0, hi - lo - 1), jnp.where(_in_range(i, rng), col_fn(j), 0)

    return pl.BlockSpec(block, (lambda j, i: index(i, j)) if cols_outer else index)


def _in_range(i, rng):
    return (i >= rng[0]) & (i < rng[1])


def _rms_scale(x, g):
    return x * lax.rsqrt(jnp.mean(x * x, axis=-1, keepdims=True) + EPS) * g


def _gelu_tanh(x):
    return 0.5 * x * (1.0 + jnp.tanh(math.sqrt(2.0 / math.pi) * (x + 0.044715 * (x * x * x))))


def _split_bf16(x):
    hi = x.astype(BF16)
    lo = (x - hi.astype(F32)).astype(BF16)
    return hi, lo


def _dot3(a, b):
    ah, al = _split_bf16(a)
    bh, bl = _split_bf16(b)
    d = functools.partial(jnp.dot, preferred_element_type=F32)
    return d(ah, bh) + (d(ah, bl) + d(al, bh))


def _in_proj_kernel(*refs, mode, ranges):
    x_refs, refs = refs[:len(ranges)], refs[len(ranges):]
    if mode == "gelu":
        g_ref, w_ref, o_ref, h_ref = refs
    else:
        g_ref, w_ref, qg_ref, kg_ref, o_ref, h_ref = refs
    i, j = pl.program_id(0), pl.program_id(1)

    def tile(rows):
        acc = jnp.dot(h_ref[rows, :], w_ref[...], preferred_element_type=F32)
        if mode == "gelu":
            o_ref[rows, :] = _gelu_tanh(acc).astype(BF16)
        else:
            gain = jnp.where(j == 0, qg_ref[...] * (DH ** -0.5), kg_ref[...])
            for c in range(0, o_ref.shape[1], DH):
                seg = acc[:, c:c + DH]
                o_ref[rows, c:c + DH] = jnp.where(j < 2, _rms_scale(seg, gain), seg).astype(BF16)

    half = o_ref.shape[0] // 2
    for x_ref, rng in zip(x_refs, ranges):
        @pl.when((j == 0) & _in_range(i, rng))
        def _():
            for rows in (slice(0, half), slice(half, 2 * half)):
                h_ref[rows, :] = _rms_scale(x_ref[rows, :], g_ref[...]).astype(BF16)
                tile(rows)

    @pl.when(j > 0)
    def _():
        tile(slice(None))


def in_proj(x_parts, g, w, layer, *, n, mode, qk_gains=(), tn=1024):
    d = x_parts[0].shape[1]
    tm = 1024
    t = sum(p.shape[0] for p in x_parts)
    x_bytes = len(x_parts) * 2 * tm * d * 4
    other_bytes = 2 * d * tn * 2 + 2 * tm * tn * 2 + tm * d * 2 + tm * tn * 4
    vmem_limit = max(VMEM_LIMIT_BYTES, x_bytes + other_bytes + (4 << 20))
    ranges = _tile_ranges(x_parts, tm)
    return pl.pallas_call(
        functools.partial(_in_proj_kernel, mode=mode, ranges=ranges),
        out_shape=jax.ShapeDtypeStruct((t, n), BF16),
        grid=(t // tm, n // tn),
        in_specs=[_part_spec((tm, d), rng, lambda j: 0) for rng in ranges] + [
            pl.BlockSpec((1, d), lambda i, j: (0, 0)),
            pl.BlockSpec((None, d, tn), lambda i, j: (layer, 0, j)),
        ] + [pl.BlockSpec((1, DH), lambda i, j: (0, 0))] * len(qk_gains),
        out_specs=pl.BlockSpec((tm, tn), lambda i, j: (i, j)),
        scratch_shapes=[pltpu.VMEM((tm, d), BF16)],
        compiler_params=_cparams(("parallel", "arbitrary"), vmem_limit),
        name="in_proj",
    )(*x_parts, g.reshape(1, d), w, *[gain.reshape(1, DH) for gain in qk_gains])


def _mlp_kernel(x_ref, g_ref, w1_ref, w2_ref, *refs, ranges):
    o_refs, h_ref = refs[:-1], refs[-1]

    def run(o_ref):
        def tile(rows, first):
            a = jnp.dot(h_ref[rows, :], w1_ref[...], preferred_element_type=F32)
            a = jnp.square(jnp.maximum(a, 0.0)).astype(BF16)
            y = jnp.dot(a, w2_ref[...], preferred_element_type=F32)
            o_ref[rows, :] = (x_ref[rows, :] if first else o_ref[rows, :]) + y

        half = o_ref.shape[0] // 2

        @pl.when(pl.program_id(1) == 0)
        def _():
            for rows in (slice(0, half), slice(half, 2 * half)):
                h_ref[rows, :] = _rms_scale(x_ref[rows, :], g_ref[...]).astype(BF16)
                tile(rows, True)

        @pl.when(pl.program_id(1) > 0)
        def _():
            tile(slice(None), False)

    if len(o_refs) == 1:
        run(o_refs[0])
    else:
        for o_ref, rng in zip(o_refs, ranges):
            pl.when(_in_range(pl.program_id(0), rng))(functools.partial(run, o_ref))


def mlp_residual(x, g, w1, w2, layer, *, out_rows=None):
    t, d = x.shape
    f = w1.shape[2]
    out_rows = (t,) if out_rows is None else out_rows
    tm, tf = 512, 1024
    outs = [jax.ShapeDtypeStruct((r, d), F32) for r in out_rows]
    ranges = _tile_ranges(outs, tm)
    res = pl.pallas_call(
        functools.partial(_mlp_kernel, ranges=ranges),
        out_shape=outs,
        grid=(t // tm, f // tf),
        in_specs=[
            pl.BlockSpec((tm, d), lambda i, k: (i, 0)),
            pl.BlockSpec((1, d), lambda i, k: (0, 0)),
            pl.BlockSpec((None, d, tf), lambda i, k: (layer, 0, k)),
            pl.BlockSpec((None, tf, d), lambda i, k: (layer, k, 0)),
        ],
        out_specs=[_part_spec((tm, d), rng, lambda k: 0) for rng in ranges],
        scratch_shapes=[pltpu.VMEM((tm, d), BF16)],
        compiler_params=_cparams(("arbitrary", "arbitrary")),
        name="mlp_residual",
    )(x, g.reshape(1, d), w1, w2)
    return res[0] if len(res) == 1 else res


def _proj_residual_kernel(*refs, ranges):
    x_refs, (a_ref, b_ref, wa_ref, wb_ref, o_ref) = refs[:len(ranges)], refs[len(ranges):]
    acc = jnp.dot(a_ref[...], wa_ref[...], preferred_element_type=F32)
    acc += jnp.dot(b_ref[...], wb_ref[...], preferred_element_type=F32)
    x = x_refs[0][...]
    for x_ref, rng in zip(x_refs[1:], ranges[1:]):
        x = jnp.where(pl.program_id(1) >= rng[0], x_ref[...], x)
    o_ref[...] = x + acc


def proj_residual(x_parts, a, b, w, layer, *, tm=1024, tn=1024):
    n = x_parts[0].shape[1]
    t = sum(p.shape[0] for p in x_parts)
    ranges = _tile_ranges(x_parts, tm)
    k = w.shape[1] // 2
    return pl.pallas_call(
        functools.partial(_proj_residual_kernel, ranges=ranges),
        out_shape=jax.ShapeDtypeStruct((t, n), F32),
        grid=(n // tn, t // tm),
        in_specs=[_part_spec((tm, tn), rng, lambda j: j, cols_outer=True) for rng in ranges] + [
            pl.BlockSpec((tm, k), lambda j, i: (i, 0)),
            pl.BlockSpec((tm, k), lambda j, i: (i, 0)),
            pl.BlockSpec((None, k, tn), lambda j, i: (layer, 0, j)),
            pl.BlockSpec((None, k, tn), lambda j, i: (layer, 1, j)),
        ],
        out_specs=pl.BlockSpec((tm, tn), lambda j, i: (i, j)),
        compiler_params=_cparams(("arbitrary", "arbitrary")),
        name="proj_residual",
    )(*x_parts, a, b, w, w)


def _gate_proj_kernel(x_ref, zz_ref, vg_ref, ws_ref, bs_ref, w_ref, o_ref, vn_ref, m_ref, *, tm):
    for c in range(tm // CHUNK):
        rows = slice(c * CHUNK, (c + 1) * CHUNK)
        vn_ref[rows, :] = _rms_scale(zz_ref[rows, D_C:].astype(F32), vg_ref[...]).astype(BF16)
        for g in range(G_C):
            cols = slice(g * DG_C, (g + 1) * DG_C)
            s = jnp.dot(ws_ref[g], vn_ref[rows, cols], preferred_element_type=F32)
            bias = bs_ref[g]
            s = s + jnp.concatenate([bias] * (DG_C // LANES), axis=1)
            m_ref[rows, cols] = (zz_ref[rows, cols].astype(F32) * s).astype(BF16)
    o_ref[...] = x_ref[...] + jnp.dot(m_ref[...], w_ref[...], preferred_element_type=F32)


def gate_proj_residual(x, zz, v_gain, w_s, b_s_lanes, w, layer, *, tm=512):
    t = zz.shape[0]
    return pl.pallas_call(
        functools.partial(_gate_proj_kernel, tm=tm),
        out_shape=jax.ShapeDtypeStruct((t, D_MODEL), F32),
        grid=(t // tm,),
        in_specs=[
            pl.BlockSpec((tm, D_MODEL), lambda i: (i, 0)),
            pl.BlockSpec((tm, 2 * D_C), lambda i: (i, 0)),
            pl.BlockSpec((1, D_C), lambda i: (0, 0)),
            pl.BlockSpec((G_C, CHUNK, CHUNK), lambda i: (0, 0, 0)),
            pl.BlockSpec((G_C, CHUNK, LANES), lambda i: (0, 0, 0)),
            pl.BlockSpec((None, D_C, D_MODEL), lambda i: (layer, 0, 0)),
        ],
        out_specs=pl.BlockSpec((tm, D_MODEL), lambda i: (i, 0)),
        scratch_shapes=[pltpu.VMEM((tm, D_C), BF16), pltpu.VMEM((tm, D_C), BF16)],
        compiler_params=_cparams(("parallel",)),
        name="gate_proj_residual",
    )(x, zz, v_gain.reshape(1, D_C), w_s, b_s_lanes, w)


N_ROWS = SEQ // GRID_W
GROUP_ROWS = 4
N_GROUPS = N_ROWS // GROUP_ROWS
KEY_ROWS = 12
NQ = GROUP_ROWS * GRID_W
NK = KEY_ROWS * GRID_W


def _natten_kernel(q_ref, k_ref, v_ref, bias_ref, o_ref, s_ref, p_ref):
    def q_rows(g):
        return pl.ds(g * NQ, NQ)

    def k_rows(g):
        ws = min(max(GROUP_ROWS * g - WIN_H // 2, 0), N_ROWS - KEY_ROWS)
        return pl.ds(ws * GRID_W, NK)

    def scores(g, slot):
        s_ref[slot] = lax.dot_general(q_ref[0, q_rows(g), :], k_ref[0, k_rows(g), :],
                                      (((1,), (1,)), ((), ())), preferred_element_type=F32)

    def softmax(g, slot):
        kind = 0 if g == 0 else (2 if g == N_GROUPS - 1 else 1)
        for a in range(GROUP_ROWS):
            rows = slice(a * GRID_W, (a + 1) * GRID_W)
            w0 = (0, a, KEY_ROWS - WIN_H)[kind]
            lo = (w0 * GRID_W) // LANES * LANES
            hi = -(-(w0 + WIN_H) * GRID_W // LANES) * LANES
            s = s_ref[slot, rows, lo:hi] + bias_ref[0, kind, rows, lo:hi]
            p = jnp.exp(s - jnp.max(s, axis=-1, keepdims=True))
            p = p * (1.0 / jnp.sum(p, axis=-1, keepdims=True))
            pieces = [jnp.zeros((GRID_W, lo), BF16), p.astype(BF16), jnp.zeros((GRID_W, NK - hi), BF16)]
            p_ref[slot, rows, :] = jnp.concatenate([x for x in pieces if x.shape[1]], axis=1)

    def values(g, slot):
        o = jnp.dot(p_ref[slot], v_ref[0, k_rows(g), :], preferred_element_type=F32)
        o_ref[0, q_rows(g), :] = o.astype(BF16)

    for t in range(N_GROUPS + 2):
        if t < N_GROUPS:
            scores(t, t % 2)
        if 1 <= t <= N_GROUPS:
            softmax(t - 1, (t - 1) % 2)
        if t >= 2:
            values(t - 2, t % 2)


def natten(qkv3, bias_tab):
    b = qkv3.shape[0]
    blk = (1, SEQ, DH)
    return pl.pallas_call(
        _natten_kernel,
        out_shape=jax.ShapeDtypeStruct((b, SEQ, D_A), BF16),
        grid=(H_A, b),
        in_specs=[
            pl.BlockSpec(blk, lambda h, i: (i, 0, h)),
            pl.BlockSpec(blk, lambda h, i: (i, 0, H_A + h)),
            pl.BlockSpec(blk, lambda h, i: (i, 0, 2 * H_A + h)),
            pl.BlockSpec((1, 3, NQ, NK), lambda h, i: (h, 0, 0, 0)),
        ],
        out_specs=pl.BlockSpec(blk, lambda h, i: (i, 0, h)),
        scratch_shapes=[pltpu.VMEM((2, NQ, NK), F32), pltpu.VMEM((2, NQ, NK), BF16)],
        compiler_params=_cparams(("parallel", "arbitrary")),
        name="natten",
    )(qkv3, qkv3, qkv3, bias_tab)


def natten_bias_table(rpb):
    col = np.arange(GRID_W)
    cs = np.clip(col - WIN_W // 2, 0, GRID_W - WIN_W)
    qc, kc = col[:, None], col[None, :]
    col_ok = (kc >= cs[:, None]) & (kc < cs[:, None] + WIN_W)
    dc = np.clip(kc - qc, -(WIN_W - 1), WIN_W - 1) + WIN_W - 1
    bias_c = rpb.astype(F32)[:, :, dc]
    a = np.arange(GROUP_ROWS)
    q_off = np.stack([a, a + WIN_H // 2, a + KEY_ROWS - GROUP_ROWS])
    w_off = np.stack([0 * a, a, 0 * a + KEY_ROWS - WIN_H])
    i = np.arange(KEY_ROWS)
    row_ok = (i >= w_off[..., None]) & (i < w_off[..., None] + WIN_H)
    dr = np.clip(i - q_off[..., None] + WIN_H - 1, 0, 2 * WIN_H - 2)
    tab = bias_c[:, dr]
    ok = row_ok[None, :, :, :, None, None] & col_ok[None, None, None, None]
    tab = jnp.where(ok, tab, -jnp.inf)
    return tab.transpose(0, 1, 2, 4, 3, 5).reshape(H_A, 3, NQ, NK)


FILT_ROWS = 512


def _filter_kernel(z_ref, t_ref, w1_ref, b1_ref, w2_ref, b2_ref, w3_ref, b3_ref, fr_ref,
                   wf_ref, wb_ref, dl_ref, hf_ref, hb_ref, h3_ref):
    @pl.when(pl.program_id(0) == 0)
    def _():
        def mlp(i, carry):
            rows = pl.ds(pl.multiple_of(i * FILT_ROWS, FILT_ROWS), FILT_ROWS)
            fr = fr_ref[...]
            h = jnp.sin(fr * (_dot3(z_ref[rows, :], w1_ref[...]) + b1_ref[...]))
            h = jnp.sin(fr * (_dot3(h, w2_ref[...]) + b2_ref[...]))
            h3_ref[rows, :] = jnp.sin(fr * (_dot3(h, w3_ref[...]) + b3_ref[...]))
            return carry

        lax.fori_loop(0, SEQ // FILT_ROWS, mlp, 0)

    def taps(i, carry):
        sf, sb = carry
        rows = pl.ds(pl.multiple_of(i * FILT_ROWS, FILT_ROWS), FILT_ROWS)
        h3 = h3_ref[rows, :]
        decay = jnp.exp(-t_ref[rows, :] * dl_ref[...])
        hf = _dot3(h3, wf_ref[...]) * decay
        hb = _dot3(h3, wb_ref[...]) * decay
        t_idx = lax.broadcasted_iota(jnp.int32, hb.shape, 0) + i * FILT_ROWS
        hb = jnp.where(t_idx == 0, 0.0, hb)
        hf_ref[rows, :] = hf
        hb_ref[rows, :] = hb
        return (sf + jnp.sum(jnp.abs(hf), axis=0, keepdims=True),
                sb + jnp.sum(jnp.abs(hb), axis=0, keepdims=True))

    zero = jnp.zeros((1, LANES), F32)
    sf, sb = lax.fori_loop(0, SEQ // FILT_ROWS, taps, (zero, zero))
    inv = 1.0 / (sf + sb)

    def scale(i, carry):
        rows = pl.ds(pl.multiple_of(i * FILT_ROWS, FILT_ROWS), FILT_ROWS)
        hf_ref[rows, :] = hf_ref[rows, :] * inv
        hb_ref[rows, :] = hb_ref[rows, :] * inv
        return carry

    lax.fori_loop(0, SEQ // FILT_ROWS, scale, 0)


def _filter_constants():
    t = np.linspace(0.0, 1.0, SEQ, dtype=np.float32).astype(np.float64)[:, None]
    bands = (EMB - 1) // 2
    w = 2.0 * math.pi * np.arange(SEQ, dtype=np.float64)[:, None] / SEQ
    f = np.linspace(1e-4, bands - 1, bands, dtype=np.float32).astype(np.float64)[None, :]
    z = np.concatenate([t, np.cos(f * w), -np.sin(f * w)], axis=-1)
    z_pad = np.zeros((SEQ, LANES), np.float32)
    z_pad[:, :EMB] = z
    t_lanes = np.broadcast_to(t.astype(np.float32), (SEQ, LANES)).copy()
    deltas = np.abs(np.linspace(math.log(DECAY_TARGET) / FAST_DECAY,
                                math.log(DECAY_TARGET) / SLOW_DECAY, D_B, dtype=np.float32))
    return z_pad, t_lanes, deltas.reshape(1, D_B)


def hyena_filter(f_w1, f_b1, f_w2, f_b2, f_w3, f_b3, f_wout, f_freq):
    z_pad, t_lanes, deltas = _filter_constants()
    w1_pad = jnp.zeros((LANES, FO), F32).at[:EMB].set(f_w1)
    nblk = D_B // LANES
    const = lambda shape: pl.BlockSpec(shape, lambda c: (0,) * len(shape))
    return pl.pallas_call(
        _filter_kernel,
        out_shape=[jax.ShapeDtypeStruct((SEQ, D_B), F32)] * 2,
        grid=(nblk,),
        in_specs=[
            const((SEQ, LANES)), const((SEQ, LANES)),
            const((LANES, FO)), const((1, FO)),
            const((FO, FO)), const((1, FO)),
            const((FO, FO)), const((1, FO)),
            const((1, FO)),
            pl.BlockSpec((FO, LANES), lambda c: (0, c)),
            pl.BlockSpec((FO, LANES), lambda c: (0, nblk + c)),
            pl.BlockSpec((1, LANES), lambda c: (0, c)),
        ],
        out_specs=[pl.BlockSpec((SEQ, LANES), lambda c: (0, c))] * 2,
        scratch_shapes=[pltpu.VMEM((SEQ, FO), F32)],
        compiler_params=_cparams(("arbitrary",)),
        name="hyena_filter",
    )(jnp.asarray(z_pad), jnp.asarray(t_lanes), w1_pad, f_b1.reshape(1, FO), f_w2,
      f_b2.reshape(1, FO), f_w3, f_b3.reshape(1, FO), f_freq.reshape(1, FO), f_wout, f_wout,
      jnp.asarray(deltas))


ROW_TILE = 8
N_K1_PAD = N_K1 + 1


@functools.lru_cache(maxsize=None)
def _dft_tables():
    k1 = np.arange(N_K1_PAD)
    n2 = np.arange(N_FAST)
    ang = 2.0 * math.pi * np.outer(k1, n2) / N_FFT
    tw = np.concatenate([np.cos(ang), -np.sin(ang)], axis=1)
    tw = np.broadcast_to(tw[:, :, None], (N_K1_PAD, 2 * N_FAST, LANES))
    ang2 = 2.0 * math.pi * np.outer(n2, n2) / N_FAST
    fr, fi = np.cos(ang2), -np.sin(ang2)
    f_fwd = np.block([[fr, -fi], [fi, fr]])
    f_inv = np.block([[fr, fi], [-fi, fr]])
    f32 = lambda a: np.ascontiguousarray(a, dtype=np.float32)
    return f32(tw), f32(f_fwd), f32(f_inv)


def _cmul_const(x, w):
    xr, xi = x
    c, s = w.real, w.imag
    eps = 1e-12
    if abs(s) < eps and abs(c - 1.0) < eps:
        return xr, xi
    if abs(c) < eps and abs(s - 1.0) < eps:
        return ("neg", xi), xr
    if abs(c) < eps and abs(s + 1.0) < eps:
        return xi, ("neg", xr)
    return xr * c - xi * s, xr * s + xi * c


def _is_neg(v):
    return isinstance(v, tuple) and v[0] == "neg"


def _plain(v):
    return -v[1] if _is_neg(v) else v


def _add(a, b):
    return a - b[1] if _is_neg(b) else a + b


def _sub(a, b):
    return a + b[1] if _is_neg(b) else a - b


def _fft_dit(xs, sign):
    n = len(xs)
    if n == 1:
        return xs
    ev = _fft_dit(xs[0::2], sign)
    od = _fft_dit(xs[1::2], sign)
    out = [None] * n
    for k in range(n // 2):
        w = complex(math.cos(2 * math.pi * k / n), sign * math.sin(2 * math.pi * k / n))
        tr, ti = _cmul_const(od[k], w)
        er, ei = ev[k]
        out[k] = (_add(er, tr), _add(ei, ti))
        out[k + n // 2] = (_sub(er, tr), _sub(ei, ti))
    return out


def _slow_forward(x):
    half, quarter = N_SLOW // 2, N_SLOW // 4
    z = [(x[2 * m], x[2 * m + 1]) for m in range(quarter)]
    zt = [z[0]]
    for m in range(1, quarter):
        tr, ti = _cmul_const(z[m], complex(math.cos(2 * math.pi * m / half), -math.sin(2 * math.pi * m / half)))
        zt.append((_plain(tr), _plain(ti)))
    ze = _fft_dit(z, -1)
    zo = _fft_dit(zt, -1)
    zc = [None] * half
    for j in range(quarter):
        zc[2 * j] = ze[j]
        zc[2 * j + 1] = zo[j]
    out = [None] * (half + 1)
    ar, ai = zc[0]
    out[0] = ((ar + ai) * 2.0, None)
    out[half] = ((ar - ai) * 2.0, None)
    ar, ai = zc[quarter]
    out[quarter] = (ar * 2.0, ai * -2.0)
    for k in range(1, quarter):
        ar, ai = zc[k]
        br, bi = zc[half - k]
        c, s = math.cos(2 * math.pi * k / N_SLOW), math.sin(2 * math.pi * k / N_SLOW)
        p, q, r, t = ar + br, ai - bi, ai + bi, ar - br
        u = r * c - t * s
        vn = t * (-c) - r * s
        out[k] = (p + u, q + vn)
        out[half - k] = (p - u, vn - q)
    return out


def _slow_inverse(xs):
    half, quarter = N_SLOW // 2, N_SLOW // 4
    zc = [None] * half
    ar, br = xs[0][0], xs[half][0]
    zc[0] = (ar + br, ar - br)
    ar, ai = xs[quarter]
    zc[quarter] = (ar * 2.0, ai * -2.0)
    for k in range(1, quarter):
        ar, ai = xs[k]
        br, bi = xs[half - k]
        c, s = math.cos(2 * math.pi * k / N_SLOW), math.sin(2 * math.pi * k / N_SLOW)
        p, q, t, r = ar + br, ai - bi, ar - br, ai + bi
        o_r = t * c - r * s
        o_i = t * s + r * c
        zc[k] = (p - o_i, q + o_r)
        zc[half - k] = (p + o_i, o_r - q)
    ze = _fft_dit(zc[0::2], +1)
    zo = _fft_dit(zc[1::2], +1)
    out = [None] * N_IN
    for m in range(quarter):
        tr, ti = _cmul_const(zo[m], complex(math.cos(2 * math.pi * m / half), math.sin(2 * math.pi * m / half)))
        out[2 * m] = _add(ze[m][0], tr)
        out[2 * m + 1] = _add(ze[m][1], ti)
    return out


def _slow_forward_pass(src_ref, a_ref):
    lanes = src_ref.shape[-1]
    zeros = jnp.zeros((ROW_TILE, lanes), F32)

    def body(j, carry):
        rows = pl.ds(pl.multiple_of(j * ROW_TILE, ROW_TILE), ROW_TILE)
        rows_im = pl.ds(pl.multiple_of(N_FAST + j * ROW_TILE, ROW_TILE), ROW_TILE)
        out = _slow_forward([src_ref[n1, rows, :] for n1 in range(N_IN)])
        for k1, (re, im) in enumerate(out):
            a_ref[k1, rows, :] = re
            a_ref[k1, rows_im, :] = zeros if im is None else im
        a_ref[N_K1, rows, :] = zeros
        a_ref[N_K1, rows_im, :] = zeros
        return carry

    lax.fori_loop(0, N_FAST // ROW_TILE, body, 0)


def _cmul(a, w):
    ar, ai, wr, wi = a[:N_FAST], a[N_FAST:], w[:N_FAST], w[N_FAST:]
    return ar * wr - ai * wi, ar * wi + ai * wr


def _cmul_conj(a, w):
    ar, ai, wr, wi = a[:N_FAST], a[N_FAST:], w[:N_FAST], w[N_FAST:]
    return ar * wr + ai * wi, ai * wr - ar * wi


def _spectrum_kernel(hf_ref, hb_ref, tw_ref, f_ref, o_ref, af_ref, ab_ref):
    _slow_forward_pass(hf_ref, af_ref)
    _slow_forward_pass(hb_ref, ab_ref)

    def body(k1, carry):
        tw = tw_ref[k1]
        f = f_ref[...]
        both = jnp.concatenate([jnp.concatenate(_cmul(af_ref[k1], tw), axis=0),
                                jnp.concatenate(_cmul(ab_ref[k1], tw), axis=0)], axis=1)
        x = _dot3(f, both)
        xf, xb = x[:, :LANES], x[:, LANES:]
        re = xf[:N_FAST] + xb[:N_FAST]
        im = xf[N_FAST:] - xb[N_FAST:]
        o_ref[k1] = jnp.concatenate([re, im], axis=0) * (1.0 / (4 * N_FFT))
        return carry

    lax.fori_loop(0, N_K1_PAD, body, 0)


def filter_spectrum(hf, hb, tw, f_fwd):
    nblk = D_B // LANES
    slabs = lambda a: a.reshape(N_IN, N_FAST, D_B)
    whole = pl.BlockSpec(memory_space=pltpu.VMEM)
    spec_blk = (N_K1_PAD, 2 * N_FAST, LANES)
    return pl.pallas_call(
        _spectrum_kernel,
        out_shape=jax.ShapeDtypeStruct((N_K1_PAD, 2 * N_FAST, D_B), F32),
        grid=(nblk,),
        in_specs=[
            pl.BlockSpec((N_IN, N_FAST, LANES), lambda c: (0, 0, c)),
            pl.BlockSpec((N_IN, N_FAST, LANES), lambda c: (0, 0, c)),
            whole, whole,
        ],
        out_specs=pl.BlockSpec(spec_blk, lambda c: (0, 0, c)),
        scratch_shapes=[pltpu.VMEM(spec_blk, F32)] * 2,
        compiler_params=_cparams(("arbitrary",)),
        name="filter_spectrum",
    )(slabs(hf), slabs(hb), tw, f_fwd)


def _hyena_kernel(x0_ref, x1_ref, v_ref, w0_ref, w1_ref, wv_ref, b0_ref, b1_ref, bv_ref,
                  hs_ref, hbias_ref, tw_ref, ff_ref, fi_ref, o_ref, u_ref, g0_ref, a_ref, z_ref, st_ref):
    zero_rows = jnp.zeros((ROW_TILE, LANES), F32)
    for op in range(3):
        st_ref[op, 0:ROW_TILE, :] = zero_rows
        st_ref[op, SEQ + ROW_TILE:SEQ + 2 * ROW_TILE, :] = zero_rows

    def stage(n1, carry):
        r0 = pl.multiple_of(n1 * N_FAST, N_FAST)
        for op, x_ref in enumerate((x0_ref, x1_ref, v_ref)):
            st_ref[op, pl.ds(r0 + ROW_TILE, N_FAST), :] = x_ref[0, pl.ds(r0, N_FAST), :].astype(F32)
        return carry

    lax.fori_loop(0, N_IN, stage, 0)

    def tap3(op, w_ref, b_ref, n1):
        r0 = n1 * N_FAST + ROW_TILE
        prev = st_ref[op, pl.ds(r0 - 1, N_FAST), :]
        cur = st_ref[op, pl.ds(r0, N_FAST), :]
        nxt = st_ref[op, pl.ds(r0 + 1, N_FAST), :]
        return prev * w_ref[0:1, :] + cur * w_ref[1:2, :] + nxt * w_ref[2:3, :] + b_ref[...]

    def conv(n1, carry):
        u_ref[n1] = tap3(2, wv_ref, bv_ref, n1) * tap3(1, w1_ref, b1_ref, n1)
        g0_ref[n1] = tap3(0, w0_ref, b0_ref, n1)
        return carry

    lax.fori_loop(0, N_IN, conv, 0)
    _slow_forward_pass(u_ref, a_ref)

    def freq(i, carry):
        ks = (2 * i, 2 * i + 1)
        tws = [tw_ref[k] for k in ks]
        a = [jnp.concatenate(_cmul(a_ref[k], tw), axis=0).astype(BF16) for k, tw in zip(ks, tws)]
        x = jnp.dot(ff_ref[...], jnp.concatenate(a, axis=1), preferred_element_type=F32)
        y = [jnp.concatenate(_cmul(x[:, j * LANES:(j + 1) * LANES], hs_ref[k]), axis=0).astype(BF16)
             for j, k in enumerate(ks)]
        z = jnp.dot(fi_ref[...], jnp.concatenate(y, axis=1), preferred_element_type=F32)
        for j, (k, tw) in enumerate(zip(ks, tws)):
            z_ref[k] = jnp.concatenate(_cmul_conj(z[:, j * LANES:(j + 1) * LANES], tw), axis=0)
        return carry

    lax.fori_loop(0, N_K1_PAD // 2, freq, 0, unroll=True)

    def back(j, carry):
        rows = pl.ds(pl.multiple_of(j * ROW_TILE, ROW_TILE), ROW_TILE)
        rows_im = pl.ds(pl.multiple_of(N_FAST + j * ROW_TILE, ROW_TILE), ROW_TILE)
        y = _slow_inverse([(z_ref[k1, rows, :], z_ref[k1, rows_im, :]) for k1 in range(N_K1)])
        for n1 in range(N_IN):
            u_ref[n1, rows, :] = y[n1] + u_ref[n1, rows, :] * hbias_ref[...]
        return carry

    lax.fori_loop(0, N_FAST // ROW_TILE, back, 0)

    def gate(n1, carry):
        rows = pl.ds(pl.multiple_of(n1 * N_FAST, N_FAST), N_FAST)
        o_ref[0, rows, :] = (u_ref[n1] * g0_ref[n1]).astype(BF16)
        return carry

    lax.fori_loop(0, N_IN, gate, 0)


def hyena(p3, sc_w, sc_b, h_spec, h_bias, tables):
    tw, f_fwd, f_inv = tables
    b = p3.shape[0]
    nblk = D_B // LANES
    base = 3 * D_A // LANES
    xblk = lambda off: pl.BlockSpec((1, SEQ, LANES), lambda c, i: (i, 0, base + off * nblk + c))
    wblk = lambda off: pl.BlockSpec((3, LANES), lambda c, i: (0, off * nblk + c))
    bblk = lambda off: pl.BlockSpec((1, LANES), lambda c, i: (0, off * nblk + c))
    whole = pl.BlockSpec(memory_space=pltpu.VMEM)
    sc_b2 = sc_b.reshape(1, 3 * D_B)
    spec_blk = (N_K1_PAD, 2 * N_FAST, LANES)
    return pl.pallas_call(
        _hyena_kernel,
        out_shape=jax.ShapeDtypeStruct((b, SEQ, D_B), BF16),
        grid=(nblk, b),
        in_specs=[
            xblk(0), xblk(1), xblk(2),
            wblk(0), wblk(1), wblk(2),
            bblk(0), bblk(1), bblk(2),
            pl.BlockSpec(spec_blk, lambda c, i: (0, 0, c)),
            pl.BlockSpec((1, LANES), lambda c, i: (0, c)),
            whole, whole, whole,
        ],
        out_specs=pl.BlockSpec((1, SEQ, LANES), lambda c, i: (i, 0, c)),
        scratch_shapes=[
            pltpu.VMEM((N_IN, N_FAST, LANES), F32),
            pltpu.VMEM((N_IN, N_FAST, LANES), F32),
            pltpu.VMEM(spec_blk, F32),
            pltpu.VMEM(spec_blk, F32),
            pltpu.VMEM((3, SEQ + 2 * ROW_TILE, LANES), F32),
        ],
        compiler_params=_cparams(("parallel", "arbitrary")),
        name="hyena",
    )(p3, p3, p3, sc_w, sc_w, sc_w, sc_b2, sc_b2, sc_b2,
      h_spec, h_bias.reshape(1, D_B), tw, f_fwd.astype(BF16), f_inv.astype(BF16))


def kernel(x_prompt, x_sample, norm_mix, norm_mlp, w_in_ab, sc_w, sc_b, q_gain, k_gain, rpb,
           f_w1, f_b1, f_w2, f_b2, f_w3, f_b3, f_wout, f_freq, h_bias, w_out_ab, w_in_c,
           v_gain, w_s, b_s, w_out_c, w_mlp1, w_mlp2):
    assert x_prompt.shape[1:] == (SEQ, D_MODEL) and x_sample.shape[1:] == (SEQ, D_MODEL)
    assert w_mlp1.shape == (DEPTH, D_MODEL, D_FF) and w_in_ab.shape[1:] == (D_MODEL, 3 * (D_A + D_B))
    nb_p, nb_s = x_prompt.shape[0], x_sample.shape[0]
    nb = nb_p + nb_s
    x = (x_prompt.reshape(nb_p * SEQ, D_MODEL), x_sample.reshape(nb_s * SEQ, D_MODEL))
    tables = tuple(jnp.asarray(a) for a in _dft_tables())

    w_in_ab, w_out_ab, w_in_c, w_out_c, w_mlp1, w_mlp2, w_s = (
        w.astype(BF16) for w in (w_in_ab, w_out_ab, w_in_c, w_out_c, w_mlp1, w_mlp2, w_s))

    for i in range(DEPTH):
        j = i // 2
        if i % 2 == 0:
            p3 = in_proj(x, norm_mix[i], w_in_ab, j, n=3 * (D_A + D_B), mode="qk_norm",
                         qk_gains=(q_gain[j], k_gain[j])).reshape(nb, SEQ, 3 * (D_A + D_B))
            a = natten(p3, natten_bias_table(rpb[j]))
            hf, hb = hyena_filter(f_w1[j], f_b1[j], f_w2[j], f_b2[j], f_w3[j], f_b3[j],
                                  f_wout[j], f_freq[j])
            h_spec = filter_spectrum(hf, hb, tables[0], tables[1])
            bo = hyena(p3, sc_w[j], sc_b[j], h_spec, h_bias[j], tables)
            x = (proj_residual(x, a.reshape(nb * SEQ, D_A), bo.reshape(nb * SEQ, D_B), w_out_ab, j),)
        else:
            zz = in_proj(x, norm_mix[i], w_in_c, j, n=2 * D_C, mode="gelu")
            b_lanes = jnp.broadcast_to(b_s[j][:, :, None], (G_C, CHUNK, LANES))
            x = (gate_proj_residual(x[0], zz, v_gain[j], w_s[j], b_lanes, w_out_c, j),)
        if i < DEPTH - 1:
            x = (mlp_residual(x[0], norm_mlp[i], w_mlp1, w_mlp2, i),)

    y_p, y_s = mlp_residual(x[0], norm_mlp[DEPTH - 1], w_mlp1, w_mlp2, DEPTH - 1,
                            out_rows=(nb_p * SEQ, nb_s * SEQ))
    return y_p.reshape(nb_p, SEQ, D_MODEL), y_s.reshape(nb_s, SEQ, D_MODEL)
```

```python
import functools
import math

import numpy as np
import jax
import jax.numpy as jnp
from jax import lax
from jax.experimental import pallas as pl
from jax.experimental.pallas import tpu as pltpu

F32 = jnp.float32
BF16 = jnp.bfloat16

D_MODEL = 2048
DEPTH = 4
SEQ = 4096
GRID_W = 64
WIN_H = 8
WIN_W = 16
DH = 128
H_A = 8
D_A = H_A * DH
D_B = 1024
EMB = 33
FO = 64
FAST_DECAY = 0.3
SLOW_DECAY = 1.5
DECAY_TARGET = 1e-2
D_C = D_MODEL
G_C = 8
DG_C = D_C // G_C
CHUNK = 128
D_FF = 4 * D_MODEL
EPS = 1e-6

LANES = 128
VMEM_LIMIT_BYTES = 52 * 1024 * 1024

N_FFT = 2 * SEQ
N_FAST = 128
N_SLOW = N_FFT // N_FAST
N_IN = SEQ // N_FAST
N_K1 = N_SLOW // 2 + 1


def _cparams(sem, vmem_limit_bytes=VMEM_LIMIT_BYTES):
    return pltpu.CompilerParams(dimension_semantics=sem, vmem_limit_bytes=vmem_limit_bytes)


def _tile_ranges(parts, tm):
    ranges, lo = [], 0
    for p in parts:
        ranges.append((lo, lo + p.shape[0] // tm))
        lo = ranges[-1][1]
    return ranges


def _part_spec(block, rng, col_fn, cols_outer=False):
    lo, hi = rng

    def index(i, j):
        return jnp.clip(i - lo, 0, hi - lo - 1), jnp.where(_in_range(i, rng), col_fn(j), 0)

    return pl.BlockSpec(block, (lambda j, i: index(i, j)) if cols_outer else index)


def _in_range(i, rng):
    return (i >= rng[0]) & (i < rng[1])


def _rms_scale(x, g):
    return x * lax.rsqrt(jnp.mean(x * x, axis=-1, keepdims=True) + EPS) * g


def _gelu_tanh(x):
    return 0.5 * x * (1.0 + jnp.tanh(math.sqrt(2.0 / math.pi) * (x + 0.044715 * (x * x * x))))


def _split_bf16(x):
    hi = x.astype(BF16)
    lo = (x - hi.astype(F32)).astype(BF16)
    return hi, lo


def _dot3(a, b):
    ah, al = _split_bf16(a)
    bh, bl = _split_bf16(b)
    d = functools.partial(jnp.dot, preferred_element_type=F32)
    return d(ah, bh) + (d(ah, bl) + d(al, bh))


def _in_proj_kernel(*refs, mode, ranges):
    x_refs, refs = refs[:len(ranges)], refs[len(ranges):]
    if mode == "gelu":
        g_ref, w_ref, o_ref, h_ref = refs
    else:
        g_ref, w_ref, qg_ref, kg_ref, o_ref, h_ref = refs
    i, j = pl.program_id(0), pl.program_id(1)

    def tile(rows):
        acc = jnp.dot(h_ref[rows, :], w_ref[...], preferred_element_type=F32)
        if mode == "gelu":
            o_ref[rows, :] = _gelu_tanh(acc).astype(BF16)
        else:
            gain = jnp.where(j == 0, qg_ref[...] * (DH ** -0.5), kg_ref[...])
            for c in range(0, o_ref.shape[1], DH):
                seg = acc[:, c:c + DH]
                o_ref[rows, c:c + DH] = jnp.where(j < 2, _rms_scale(seg, gain), seg).astype(BF16)

    half = o_ref.shape[0] // 2
    for x_ref, rng in zip(x_refs, ranges):
        @pl.when((j == 0) & _in_range(i, rng))
        def _():
            for rows in (slice(0, half), slice(half, 2 * half)):
                h_ref[rows, :] = _rms_scale(x_ref[rows, :], g_ref[...]).astype(BF16)
                tile(rows)

    @pl.when(j > 0)
    def _():
        tile(slice(None))


def in_proj(x_parts, g, w, layer, *, n, mode, qk_gains=(), tn=1024):
    d = x_parts[0].shape[1]
    tm = 1024
    t = sum(p.shape[0] for p in x_parts)
    x_bytes = len(x_parts) * 2 * tm * d * 4
    other_bytes = 2 * d * tn * 2 + 2 * tm * tn * 2 + tm * d * 2 + tm * tn * 4
    vmem_limit = max(VMEM_LIMIT_BYTES, x_bytes + other_bytes + (4 << 20))
    ranges = _tile_ranges(x_parts, tm)
    return pl.pallas_call(
        functools.partial(_in_proj_kernel, mode=mode, ranges=ranges),
        out_shape=jax.ShapeDtypeStruct((t, n), BF16),
        grid=(t // tm, n // tn),
        in_specs=[_part_spec((tm, d), rng, lambda j: 0) for rng in ranges] + [
            pl.BlockSpec((1, d), lambda i, j: (0, 0)),
            pl.BlockSpec((None, d, tn), lambda i, j: (layer, 0, j)),
        ] + [pl.BlockSpec((1, DH), lambda i, j: (0, 0))] * len(qk_gains),
        out_specs=pl.BlockSpec((tm, tn), lambda i, j: (i, j)),
        scratch_shapes=[pltpu.VMEM((tm, d), BF16)],
        compiler_params=_cparams(("parallel", "arbitrary"), vmem_limit),
        name="in_proj",
    )(*x_parts, g.reshape(1, d), w, *[gain.reshape(1, DH) for gain in qk_gains])


def _mlp_kernel(x_ref, g_ref, w1_ref, w2_ref, *refs, ranges):
    o_refs, h_ref = refs[:-1], refs[-1]

    def run(o_ref):
        def tile(rows, first):
            a = jnp.dot(h_ref[rows, :], w1_ref[...], preferred_element_type=F32)
            a = jnp.square(jnp.maximum(a, 0.0)).astype(BF16)
            y = jnp.dot(a, w2_ref[...], preferred_element_type=F32)
            o_ref[rows, :] = (x_ref[rows, :] if first else o_ref[rows, :]) + y

        half = o_ref.shape[0] // 2

        @pl.when(pl.program_id(1) == 0)
        def _():
            for rows in (slice(0, half), slice(half, 2 * half)):
                h_ref[rows, :] = _rms_scale(x_ref[rows, :], g_ref[...]).astype(BF16)
                tile(rows, True)

        @pl.when(pl.program_id(1) > 0)
        def _():
            tile(slice(None), False)

    if len(o_refs) == 1:
        run(o_refs[0])
    else:
        for o_ref, rng in zip(o_refs, ranges):
            pl.when(_in_range(pl.program_id(0), rng))(functools.partial(run, o_ref))


def mlp_residual(x, g, w1, w2, layer, *, out_rows=None):
    t, d = x.shape
    f = w1.shape[2]
    out_rows = (t,) if out_rows is None else out_rows
    tm = 512
    tf = 2048 if len(out_rows) == 1 else 1024
    vmem_need = 2 * 2 * d * tf * 2 + (1 + len(out_rows)) * 2 * tm * d * 4 + tm * d * 2 + tm * tf * 6
    vmem_limit = max(VMEM_LIMIT_BYTES, vmem_need + (4 << 20))
    outs = [jax.ShapeDtypeStruct((r, d), F32) for r in out_rows]
    ranges = _tile_ranges(outs, tm)
    res = pl.pallas_call(
        functools.partial(_mlp_kernel, ranges=ranges),
        out_shape=outs,
        grid=(t // tm, f // tf),
        in_specs=[
            pl.BlockSpec((tm, d), lambda i, k: (i, 0)),
            pl.BlockSpec((1, d), lambda i, k: (0, 0)),
            pl.BlockSpec((None, d, tf), lambda i, k: (layer, 0, k)),
            pl.BlockSpec((None, tf, d), lambda i, k: (layer, k, 0)),
        ],
        out_specs=[_part_spec((tm, d), rng, lambda k: 0) for rng in ranges],
        scratch_shapes=[pltpu.VMEM((tm, d), BF16)],
        compiler_params=_cparams(("arbitrary", "arbitrary"), vmem_limit),
        name="mlp_residual",
    )(x, g.reshape(1, d), w1, w2)
    return res[0] if len(res) == 1 else res


def _proj_residual_kernel(*refs, ranges):
    x_refs, (a_ref, b_ref, wa_ref, wb_ref, o_ref) = refs[:len(ranges)], refs[len(ranges):]
    acc = jnp.dot(a_ref[...], wa_ref[...], preferred_element_type=F32)
    acc += jnp.dot(b_ref[...], wb_ref[...], preferred_element_type=F32)
    x = x_refs[0][...]
    for x_ref, rng in zip(x_refs[1:], ranges[1:]):
        x = jnp.where(pl.program_id(1) >= rng[0], x_ref[...], x)
    o_ref[...] = x + acc


def proj_residual(x_parts, a, b, w, layer, *, tm=1024, tn=1024):
    n = x_parts[0].shape[1]
    t = sum(p.shape[0] for p in x_parts)
    ranges = _tile_ranges(x_parts, tm)
    k = w.shape[1] // 2
    return pl.pallas_call(
        functools.partial(_proj_residual_kernel, ranges=ranges),
        out_shape=jax.ShapeDtypeStruct((t, n), F32),
        grid=(n // tn, t // tm),
        in_specs=[_part_spec((tm, tn), rng, lambda j: j, cols_outer=True) for rng in ranges] + [
            pl.BlockSpec((tm, k), lambda j, i: (i, 0)),
            pl.BlockSpec((tm, k), lambda j, i: (i, 0)),
            pl.BlockSpec((None, k, tn), lambda j, i: (layer, 0, j)),
            pl.BlockSpec((None, k, tn), lambda j, i: (layer, 1, j)),
        ],
        out_specs=pl.BlockSpec((tm, tn), lambda j, i: (i, j)),
        compiler_params=_cparams(("arbitrary", "arbitrary")),
        name="proj_residual",
    )(*x_parts, a, b, w, w)


def _gate_proj_kernel(x_ref, zz_ref, vg_ref, ws_ref, bs_ref, w_ref, o_ref, vn_ref, m_ref, *, tm):
    for c in range(tm // CHUNK):
        rows = slice(c * CHUNK, (c + 1) * CHUNK)
        vn_ref[rows, :] = _rms_scale(zz_ref[rows, D_C:].astype(F32), vg_ref[...]).astype(BF16)
        for g in range(G_C):
            cols = slice(g * DG_C, (g + 1) * DG_C)
            s = jnp.dot(ws_ref[g], vn_ref[rows, cols], preferred_element_type=F32)
            bias = bs_ref[g]
            s = s + jnp.concatenate([bias] * (DG_C // LANES), axis=1)
            m_ref[rows, cols] = (zz_ref[rows, cols].astype(F32) * s).astype(BF16)
    o_ref[...] = x_ref[...] + jnp.dot(m_ref[...], w_ref[...], preferred_element_type=F32)


def gate_proj_residual(x, zz, v_gain, w_s, b_s_lanes, w, layer, *, tm=512):
    t = zz.shape[0]
    return pl.pallas_call(
        functools.partial(_gate_proj_kernel, tm=tm),
        out_shape=jax.ShapeDtypeStruct((t, D_MODEL), F32),
        grid=(t // tm,),
        in_specs=[
            pl.BlockSpec((tm, D_MODEL), lambda i: (i, 0)),
            pl.BlockSpec((tm, 2 * D_C), lambda i: (i, 0)),
            pl.BlockSpec((1, D_C), lambda i: (0, 0)),
            pl.BlockSpec((G_C, CHUNK, CHUNK), lambda i: (0, 0, 0)),
            pl.BlockSpec((G_C, CHUNK, LANES), lambda i: (0, 0, 0)),
            pl.BlockSpec((None, D_C, D_MODEL), lambda i: (layer, 0, 0)),
        ],
        out_specs=pl.BlockSpec((tm, D_MODEL), lambda i: (i, 0)),
        scratch_shapes=[pltpu.VMEM((tm, D_C), BF16), pltpu.VMEM((tm, D_C), BF16)],
        compiler_params=_cparams(("parallel",)),
        name="gate_proj_residual",
    )(x, zz, v_gain.reshape(1, D_C), w_s, b_s_lanes, w)


N_ROWS = SEQ // GRID_W
GROUP_ROWS = 4
N_GROUPS = N_ROWS // GROUP_ROWS
KEY_ROWS = 12
NQ = GROUP_ROWS * GRID_W
NK = KEY_ROWS * GRID_W


def _natten_kernel(q_ref, k_ref, v_ref, bias_ref, o_ref, s_ref, p_ref):
    def q_rows(g):
        return pl.ds(g * NQ, NQ)

    def k_rows(g):
        ws = min(max(GROUP_ROWS * g - WIN_H // 2, 0), N_ROWS - KEY_ROWS)
        return pl.ds(ws * GRID_W, NK)

    def scores(g, slot):
        s_ref[slot] = lax.dot_general(q_ref[0, q_rows(g), :], k_ref[0, k_rows(g), :],
                                      (((1,), (1,)), ((), ())), preferred_element_type=F32)

    def softmax(g, slot):
        kind = 0 if g == 0 else (2 if g == N_GROUPS - 1 else 1)
        for a in range(GROUP_ROWS):
            rows = slice(a * GRID_W, (a + 1) * GRID_W)
            w0 = (0, a, KEY_ROWS - WIN_H)[kind]
            lo = (w0 * GRID_W) // LANES * LANES
            hi = -(-(w0 + WIN_H) * GRID_W // LANES) * LANES
            s = s_ref[slot, rows, lo:hi] + bias_ref[0, kind, rows, lo:hi]
            p = jnp.exp(s - jnp.max(s, axis=-1, keepdims=True))
            p = p * (1.0 / jnp.sum(p, axis=-1, keepdims=True))
            pieces = [jnp.zeros((GRID_W, lo), BF16), p.astype(BF16), jnp.zeros((GRID_W, NK - hi), BF16)]
            p_ref[slot, rows, :] = jnp.concatenate([x for x in pieces if x.shape[1]], axis=1)

    def values(g, slot):
        o = jnp.dot(p_ref[slot], v_ref[0, k_rows(g), :], preferred_element_type=F32)
        o_ref[0, q_rows(g), :] = o.astype(BF16)

    for t in range(N_GROUPS + 2):
        if t < N_GROUPS:
            scores(t, t % 2)
        if 1 <= t <= N_GROUPS:
            softmax(t - 1, (t - 1) % 2)
        if t >= 2:
            values(t - 2, t % 2)


def natten(qkv3, bias_tab):
    b = qkv3.shape[0]
    blk = (1, SEQ, DH)
    return pl.pallas_call(
        _natten_kernel,
        out_shape=jax.ShapeDtypeStruct((b, SEQ, D_A), BF16),
        grid=(H_A, b),
        in_specs=[
            pl.BlockSpec(blk, lambda h, i: (i, 0, h)),
            pl.BlockSpec(blk, lambda h, i: (i, 0, H_A + h)),
            pl.BlockSpec(blk, lambda h, i: (i, 0, 2 * H_A + h)),
            pl.BlockSpec((1, 3, NQ, NK), lambda h, i: (h, 0, 0, 0)),
        ],
        out_specs=pl.BlockSpec(blk, lambda h, i: (i, 0, h)),
        scratch_shapes=[pltpu.VMEM((2, NQ, NK), F32), pltpu.VMEM((2, NQ, NK), BF16)],
        compiler_params=_cparams(("parallel", "arbitrary")),
        name="natten",
    )(qkv3, qkv3, qkv3, bias_tab)


def natten_bias_table(rpb):
    col = np.arange(GRID_W)
    cs = np.clip(col - WIN_W // 2, 0, GRID_W - WIN_W)
    qc, kc = col[:, None], col[None, :]
    col_ok = (kc >= cs[:, None]) & (kc < cs[:, None] + WIN_W)
    dc = np.clip(kc - qc, -(WIN_W - 1), WIN_W - 1) + WIN_W - 1
    bias_c = rpb.astype(F32)[:, :, dc]
    a = np.arange(GROUP_ROWS)
    q_off = np.stack([a, a + WIN_H // 2, a + KEY_ROWS - GROUP_ROWS])
    w_off = np.stack([0 * a, a, 0 * a + KEY_ROWS - WIN_H])
    i = np.arange(KEY_ROWS)
    row_ok = (i >= w_off[..., None]) & (i < w_off[..., None] + WIN_H)
    dr = np.clip(i - q_off[..., None] + WIN_H - 1, 0, 2 * WIN_H - 2)
    tab = bias_c[:, dr]
    ok = row_ok[None, :, :, :, None, None] & col_ok[None, None, None, None]
    tab = jnp.where(ok, tab, -jnp.inf)
    return tab.transpose(0, 1, 2, 4, 3, 5).reshape(H_A, 3, NQ, NK)


FILT_ROWS = 512


def _filter_kernel(z_ref, t_ref, w1_ref, b1_ref, w2_ref, b2_ref, w3_ref, b3_ref, fr_ref,
                   wf_ref, wb_ref, dl_ref, hf_ref, hb_ref, h3_ref):
    @pl.when(pl.program_id(0) == 0)
    def _():
        def mlp(i, carry):
            rows = pl.ds(pl.multiple_of(i * FILT_ROWS, FILT_ROWS), FILT_ROWS)
            fr = fr_ref[...]
            h = jnp.sin(fr * (_dot3(z_ref[rows, :], w1_ref[...]) + b1_ref[...]))
            h = jnp.sin(fr * (_dot3(h, w2_ref[...]) + b2_ref[...]))
            h3_ref[rows, :] = jnp.sin(fr * (_dot3(h, w3_ref[...]) + b3_ref[...]))
            return carry

        lax.fori_loop(0, SEQ // FILT_ROWS, mlp, 0)

    def taps(i, carry):
        sf, sb = carry
        rows = pl.ds(pl.multiple_of(i * FILT_ROWS, FILT_ROWS), FILT_ROWS)
        h3 = h3_ref[rows, :]
        decay = jnp.exp(-t_ref[rows, :] * dl_ref[...])
        hf = _dot3(h3, wf_ref[...]) * decay
        hb = _dot3(h3, wb_ref[...]) * decay
        t_idx = lax.broadcasted_iota(jnp.int32, hb.shape, 0) + i * FILT_ROWS
        hb = jnp.where(t_idx == 0, 0.0, hb)
        hf_ref[rows, :] = hf
        hb_ref[rows, :] = hb
        return (sf + jnp.sum(jnp.abs(hf), axis=0, keepdims=True),
                sb + jnp.sum(jnp.abs(hb), axis=0, keepdims=True))

    zero = jnp.zeros((1, LANES), F32)
    sf, sb = lax.fori_loop(0, SEQ // FILT_ROWS, taps, (zero, zero))
    inv = 1.0 / (sf + sb)

    def scale(i, carry):
        rows = pl.ds(pl.multiple_of(i * FILT_ROWS, FILT_ROWS), FILT_ROWS)
        hf_ref[rows, :] = hf_ref[rows, :] * inv
        hb_ref[rows, :] = hb_ref[rows, :] * inv
        return carry

    lax.fori_loop(0, SEQ // FILT_ROWS, scale, 0)


def _filter_constants():
    t = np.linspace(0.0, 1.0, SEQ, dtype=np.float32).astype(np.float64)[:, None]
    bands = (EMB - 1) // 2
    w = 2.0 * math.pi * np.arange(SEQ, dtype=np.float64)[:, None] / SEQ
    f = np.linspace(1e-4, bands - 1, bands, dtype=np.float32).astype(np.float64)[None, :]
    z = np.concatenate([t, np.cos(f * w), -np.sin(f * w)], axis=-1)
    z_pad = np.zeros((SEQ, LANES), np.float32)
    z_pad[:, :EMB] = z
    t_lanes = np.broadcast_to(t.astype(np.float32), (SEQ, LANES)).copy()
    deltas = np.abs(np.linspace(math.log(DECAY_TARGET) / FAST_DECAY,
                                math.log(DECAY_TARGET) / SLOW_DECAY, D_B, dtype=np.float32))
    return z_pad, t_lanes, deltas.reshape(1, D_B)


def hyena_filter(f_w1, f_b1, f_w2, f_b2, f_w3, f_b3, f_wout, f_freq):
    z_pad, t_lanes, deltas = _filter_constants()
    w1_pad = jnp.zeros((LANES, FO), F32).at[:EMB].set(f_w1)
    nblk = D_B // LANES
    const = lambda shape: pl.BlockSpec(shape, lambda c: (0,) * len(shape))
    return pl.pallas_call(
        _filter_kernel,
        out_shape=[jax.ShapeDtypeStruct((SEQ, D_B), F32)] * 2,
        grid=(nblk,),
        in_specs=[
            const((SEQ, LANES)), const((SEQ, LANES)),
            const((LANES, FO)), const((1, FO)),
            const((FO, FO)), const((1, FO)),
            const((FO, FO)), const((1, FO)),
            const((1, FO)),
            pl.BlockSpec((FO, LANES), lambda c: (0, c)),
            pl.BlockSpec((FO, LANES), lambda c: (0, nblk + c)),
            pl.BlockSpec((1, LANES), lambda c: (0, c)),
        ],
        out_specs=[pl.BlockSpec((SEQ, LANES), lambda c: (0, c))] * 2,
        scratch_shapes=[pltpu.VMEM((SEQ, FO), F32)],
        compiler_params=_cparams(("arbitrary",)),
        name="hyena_filter",
    )(jnp.asarray(z_pad), jnp.asarray(t_lanes), w1_pad, f_b1.reshape(1, FO), f_w2,
      f_b2.reshape(1, FO), f_w3, f_b3.reshape(1, FO), f_freq.reshape(1, FO), f_wout, f_wout,
      jnp.asarray(deltas))


ROW_TILE = 8
N_K1_PAD = N_K1 + 1


@functools.lru_cache(maxsize=None)
def _dft_tables():
    k1 = np.arange(N_K1_PAD)
    n2 = np.arange(N_FAST)
    ang = 2.0 * math.pi * np.outer(k1, n2) / N_FFT
    tw = np.concatenate([np.cos(ang), -np.sin(ang)], axis=1)
    tw = np.broadcast_to(tw[:, :, None], (N_K1_PAD, 2 * N_FAST, LANES))
    ang2 = 2.0 * math.pi * np.outer(n2, n2) / N_FAST
    fr, fi = np.cos(ang2), -np.sin(ang2)
    f_fwd = np.block([[fr, -fi], [fi, fr]])
    f_inv = np.block([[fr, fi], [-fi, fr]])
    f32 = lambda a: np.ascontiguousarray(a, dtype=np.float32)
    return f32(tw), f32(f_fwd), f32(f_inv)


def _cmul_const(x, w):
    xr, xi = x
    c, s = w.real, w.imag
    eps = 1e-12
    if abs(s) < eps and abs(c - 1.0) < eps:
        return xr, xi
    if abs(c) < eps and abs(s - 1.0) < eps:
        return ("neg", xi), xr
    if abs(c) < eps and abs(s + 1.0) < eps:
        return xi, ("neg", xr)
    return xr * c - xi * s, xr * s + xi * c


def _is_neg(v):
    return isinstance(v, tuple) and v[0] == "neg"


def _plain(v):
    return -v[1] if _is_neg(v) else v


def _add(a, b):
    return a - b[1] if _is_neg(b) else a + b


def _sub(a, b):
    return a + b[1] if _is_neg(b) else a - b


def _fft_dit(xs, sign):
    n = len(xs)
    if n == 1:
        return xs
    ev = _fft_dit(xs[0::2], sign)
    od = _fft_dit(xs[1::2], sign)
    out = [None] * n
    for k in range(n // 2):
        w = complex(math.cos(2 * math.pi * k / n), sign * math.sin(2 * math.pi * k / n))
        tr, ti = _cmul_const(od[k], w)
        er, ei = ev[k]
        out[k] = (_add(er, tr), _add(ei, ti))
        out[k + n // 2] = (_sub(er, tr), _sub(ei, ti))
    return out


def _slow_forward(x):
    half, quarter = N_SLOW // 2, N_SLOW // 4
    z = [(x[2 * m], x[2 * m + 1]) for m in range(quarter)]
    zt = [z[0]]
    for m in range(1, quarter):
        tr, ti = _cmul_const(z[m], complex(math.cos(2 * math.pi * m / half), -math.sin(2 * math.pi * m / half)))
        zt.append((_plain(tr), _plain(ti)))
    ze = _fft_dit(z, -1)
    zo = _fft_dit(zt, -1)
    zc = [None] * half
    for j in range(quarter):
        zc[2 * j] = ze[j]
        zc[2 * j + 1] = zo[j]
    out = [None] * (half + 1)
    ar, ai = zc[0]
    out[0] = ((ar + ai) * 2.0, None)
    out[half] = ((ar - ai) * 2.0, None)
    ar, ai = zc[quarter]
    out[quarter] = (ar * 2.0, ai * -2.0)
    for k in range(1, quarter):
        ar, ai = zc[k]
        br, bi = zc[half - k]
        c, s = math.cos(2 * math.pi * k / N_SLOW), math.sin(2 * math.pi * k / N_SLOW)
        p, q, r, t = ar + br, ai - bi, ai + bi, ar - br
        u = r * c - t * s
        vn = t * (-c) - r * s
        out[k] = (p + u, q + vn)
        out[half - k] = (p - u, vn - q)
    return out


def _slow_inverse(xs):
    half, quarter = N_SLOW // 2, N_SLOW // 4
    zc = [None] * half
    ar, br = xs[0][0], xs[half][0]
    zc[0] = (ar + br, ar - br)
    ar, ai = xs[quarter]
    zc[quarter] = (ar * 2.0, ai * -2.0)
    for k in range(1, quarter):
        ar, ai = xs[k]
        br, bi = xs[half - k]
        c, s = math.cos(2 * math.pi * k / N_SLOW), math.sin(2 * math.pi * k / N_SLOW)
        p, q, t, r = ar + br, ai - bi, ar - br, ai + bi
        o_r = t * c - r * s
        o_i = t * s + r * c
        zc[k] = (p - o_i, q + o_r)
        zc[half - k] = (p + o_i, o_r - q)
    ze = _fft_dit(zc[0::2], +1)
    zo = _fft_dit(zc[1::2], +1)
    out = [None] * N_IN
    for m in range(quarter):
        tr, ti = _cmul_const(zo[m], complex(math.cos(2 * math.pi * m / half), math.sin(2 * math.pi * m / half)))
        out[2 * m] = _add(ze[m][0], tr)
        out[2 * m + 1] = _add(ze[m][1], ti)
    return out


def _slow_forward_pass(src_ref, a_ref):
    lanes = src_ref.shape[-1]
    zeros = jnp.zeros((ROW_TILE, lanes), F32)

    def body(j, carry):
        rows = pl.ds(pl.multiple_of(j * ROW_TILE, ROW_TILE), ROW_TILE)
        rows_im = pl.ds(pl.multiple_of(N_FAST + j * ROW_TILE, ROW_TILE), ROW_TILE)
        out = _slow_forward([src_ref[n1, rows, :] for n1 in range(N_IN)])
        for k1, (re, im) in enumerate(out):
            a_ref[k1, rows, :] = re
            a_ref[k1, rows_im, :] = zeros if im is None else im
        a_ref[N_K1, rows, :] = zeros
        a_ref[N_K1, rows_im, :] = zeros
        return carry

    lax.fori_loop(0, N_FAST // ROW_TILE, body, 0)


def _cmul(a, w):
    ar, ai, wr, wi = a[:N_FAST], a[N_FAST:], w[:N_FAST], w[N_FAST:]
    return ar * wr - ai * wi, ar * wi + ai * wr


def _cmul_conj(a, w):
    ar, ai, wr, wi = a[:N_FAST], a[N_FAST:], w[:N_FAST], w[N_FAST:]
    return ar * wr + ai * wi, ai * wr - ar * wi


def _spectrum_kernel(hf_ref, hb_ref, tw_ref, f_ref, o_ref, af_ref, ab_ref):
    _slow_forward_pass(hf_ref, af_ref)
    _slow_forward_pass(hb_ref, ab_ref)

    def body(k1, carry):
        tw = tw_ref[k1]
        f = f_ref[...]
        both = jnp.concatenate([jnp.concatenate(_cmul(af_ref[k1], tw), axis=0),
                                jnp.concatenate(_cmul(ab_ref[k1], tw), axis=0)], axis=1)
        x = _dot3(f, both)
        xf, xb = x[:, :LANES], x[:, LANES:]
        re = xf[:N_FAST] + xb[:N_FAST]
        im = xf[N_FAST:] - xb[N_FAST:]
        o_ref[k1] = jnp.concatenate([re, im], axis=0) * (1.0 / (4 * N_FFT))
        return carry

    lax.fori_loop(0, N_K1_PAD, body, 0)


def filter_spectrum(hf, hb, tw, f_fwd):
    nblk = D_B // LANES
    slabs = lambda a: a.reshape(N_IN, N_FAST, D_B)
    whole = pl.BlockSpec(memory_space=pltpu.VMEM)
    spec_blk = (N_K1_PAD, 2 * N_FAST, LANES)
    return pl.pallas_call(
        _spectrum_kernel,
        out_shape=jax.ShapeDtypeStruct((N_K1_PAD, 2 * N_FAST, D_B), F32),
        grid=(nblk,),
        in_specs=[
            pl.BlockSpec((N_IN, N_FAST, LANES), lambda c: (0, 0, c)),
            pl.BlockSpec((N_IN, N_FAST, LANES), lambda c: (0, 0, c)),
            whole, whole,
        ],
        out_specs=pl.BlockSpec(spec_blk, lambda c: (0, 0, c)),
        scratch_shapes=[pltpu.VMEM(spec_blk, F32)] * 2,
        compiler_params=_cparams(("arbitrary",)),
        name="filter_spectrum",
    )(slabs(hf), slabs(hb), tw, f_fwd)


def _hyena_kernel(x0_ref, x1_ref, v_ref, w0_ref, w1_ref, wv_ref, b0_ref, b1_ref, bv_ref,
                  hs_ref, hbias_ref, tw_ref, ff_ref, fi_ref, o_ref, u_ref, g0_ref, a_ref, z_ref, st_ref):
    zero_rows = jnp.zeros((ROW_TILE, LANES), F32)
    for op in range(3):
        st_ref[op, 0:ROW_TILE, :] = zero_rows
        st_ref[op, SEQ + ROW_TILE:SEQ + 2 * ROW_TILE, :] = zero_rows

    def stage(n1, carry):
        r0 = pl.multiple_of(n1 * N_FAST, N_FAST)
        for op, x_ref in enumerate((x0_ref, x1_ref, v_ref)):
            st_ref[op, pl.ds(r0 + ROW_TILE, N_FAST), :] = x_ref[0, pl.ds(r0, N_FAST), :].astype(F32)
        return carry

    lax.fori_loop(0, N_IN, stage, 0)

    def tap3(op, w_ref, b_ref, n1):
        r0 = n1 * N_FAST + ROW_TILE
        prev = st_ref[op, pl.ds(r0 - 1, N_FAST), :]
        cur = st_ref[op, pl.ds(r0, N_FAST), :]
        nxt = st_ref[op, pl.ds(r0 + 1, N_FAST), :]
        return prev * w_ref[0:1, :] + cur * w_ref[1:2, :] + nxt * w_ref[2:3, :] + b_ref[...]

    def conv(n1, carry):
        u_ref[n1] = tap3(2, wv_ref, bv_ref, n1) * tap3(1, w1_ref, b1_ref, n1)
        g0_ref[n1] = tap3(0, w0_ref, b0_ref, n1)
        return carry

    lax.fori_loop(0, N_IN, conv, 0)
    _slow_forward_pass(u_ref, a_ref)

    def freq(i, carry):
        ks = (2 * i, 2 * i + 1)
        tws = [tw_ref[k] for k in ks]
        a = [jnp.concatenate(_cmul(a_ref[k], tw), axis=0).astype(BF16) for k, tw in zip(ks, tws)]
        x = jnp.dot(ff_ref[...], jnp.concatenate(a, axis=1), preferred_element_type=F32)
        y = [jnp.concatenate(_cmul(x[:, j * LANES:(j + 1) * LANES], hs_ref[k]), axis=0).astype(BF16)
             for j, k in enumerate(ks)]
        z = jnp.dot(fi_ref[...], jnp.concatenate(y, axis=1), preferred_element_type=F32)
        for j, (k, tw) in enumerate(zip(ks, tws)):
            z_ref[k] = jnp.concatenate(_cmul_conj(z[:, j * LANES:(j + 1) * LANES], tw), axis=0)
        return carry

    lax.fori_loop(0, N_K1_PAD // 2, freq, 0, unroll=True)

    def back(j, carry):
        rows = pl.ds(pl.multiple_of(j * ROW_TILE, ROW_TILE), ROW_TILE)
        rows_im = pl.ds(pl.multiple_of(N_FAST + j * ROW_TILE, ROW_TILE), ROW_TILE)
        y = _slow_inverse([(z_ref[k1, rows, :], z_ref[k1, rows_im, :]) for k1 in range(N_K1)])
        for n1 in range(N_IN):
            u_ref[n1, rows, :] = y[n1] + u_ref[n1, rows, :] * hbias_ref[...]
        return carry

    lax.fori_loop(0, N_FAST // ROW_TILE, back, 0)

    def gate(n1, carry):
        rows = pl.ds(pl.multiple_of(n1 * N_FAST, N_FAST), N_FAST)
        o_ref[0, rows, :] = (u_ref[n1] * g0_ref[n1]).astype(BF16)
        return carry

    lax.fori_loop(0, N_IN, gate, 0)


def hyena(p3, sc_w, sc_b, h_spec, h_bias, tables):
    tw, f_fwd, f_inv = tables
    b = p3.shape[0]
    nblk = D_B // LANES
    base = 3 * D_A // LANES
    xblk = lambda off: pl.BlockSpec((1, SEQ, LANES), lambda c, i: (i, 0, base + off * nblk + c))
    wblk = lambda off: pl.BlockSpec((3, LANES), lambda c, i: (0, off * nblk + c))
    bblk = lambda off: pl.BlockSpec((1, LANES), lambda c, i: (0, off * nblk + c))
    whole = pl.BlockSpec(memory_space=pltpu.VMEM)
    sc_b2 = sc_b.reshape(1, 3 * D_B)
    spec_blk = (N_K1_PAD, 2 * N_FAST, LANES)
    return pl.pallas_call(
        _hyena_kernel,
        out_shape=jax.ShapeDtypeStruct((b, SEQ, D_B), BF16),
        grid=(nblk, b),
        in_specs=[
            xblk(0), xblk(1), xblk(2),
            wblk(0), wblk(1), wblk(2),
            bblk(0), bblk(1), bblk(2),
            pl.BlockSpec(spec_blk, lambda c, i: (0, 0, c)),
            pl.BlockSpec((1, LANES), lambda c, i: (0, c)),
            whole, whole, whole,
        ],
        out_specs=pl.BlockSpec((1, SEQ, LANES), lambda c, i: (i, 0, c)),
        scratch_shapes=[
            pltpu.VMEM((N_IN, N_FAST, LANES), F32),
            pltpu.VMEM((N_IN, N_FAST, LANES), F32),
            pltpu.VMEM(spec_blk, F32),
            pltpu.VMEM(spec_blk, F32),
            pltpu.VMEM((3, SEQ + 2 * ROW_TILE, LANES), F32),
        ],
        compiler_params=_cparams(("parallel", "arbitrary")),
        name="hyena",
    )(p3, p3, p3, sc_w, sc_w, sc_w, sc_b2, sc_b2, sc_b2,
      h_spec, h_bias.reshape(1, D_B), tw, f_fwd.astype(BF16), f_inv.astype(BF16))


def kernel(x_prompt, x_sample, norm_mix, norm_mlp, w_in_ab, sc_w, sc_b, q_gain, k_gain, rpb,
           f_w1, f_b1, f_w2, f_b2, f_w3, f_b3, f_wout, f_freq, h_bias, w_out_ab, w_in_c,
           v_gain, w_s, b_s, w_out_c, w_mlp1, w_mlp2):
    assert x_prompt.shape[1:] == (SEQ, D_MODEL) and x_sample.shape[1:] == (SEQ, D_MODEL)
    assert w_mlp1.shape == (DEPTH, D_MODEL, D_FF) and w_in_ab.shape[1:] == (D_MODEL, 3 * (D_A + D_B))
    nb_p, nb_s = x_prompt.shape[0], x_sample.shape[0]
    nb = nb_p + nb_s
    x = (x_prompt.reshape(nb_p * SEQ, D_MODEL), x_sample.reshape(nb_s * SEQ, D_MODEL))
    tables = tuple(jnp.asarray(a) for a in _dft_tables())

    w_in_ab, w_out_ab, w_in_c, w_out_c, w_mlp1, w_mlp2, w_s = (
        w.astype(BF16) for w in (w_in_ab, w_out_ab, w_in_c, w_out_c, w_mlp1, w_mlp2, w_s))

    for i in range(DEPTH):
        j = i // 2
        if i % 2 == 0:
            p3 = in_proj(x, norm_mix[i], w_in_ab, j, n=3 * (D_A + D_B), mode="qk_norm",
                         qk_gains=(q_gain[j], k_gain[j])).reshape(nb, SEQ, 3 * (D_A + D_B))
            a = natten(p3, natten_bias_table(rpb[j]))
            hf, hb = hyena_filter(f_w1[j], f_b1[j], f_w2[j], f_b2[j], f_w3[j], f_b3[j],
                                  f_wout[j], f_freq[j])
            h_spec = filter_spectrum(hf, hb, tables[0], tables[1])
            bo = hyena(p3, sc_w[j], sc_b[j], h_spec, h_bias[j], tables)
            x = (proj_residual(x, a.reshape(nb * SEQ, D_A), bo.reshape(nb * SEQ, D_B), w_out_ab, j),)
        else:
            zz = in_proj(x, norm_mix[i], w_in_c, j, n=2 * D_C, mode="gelu")
            b_lanes = jnp.broadcast_to(b_s[j][:, :, None], (G_C, CHUNK, LANES))
            x = (gate_proj_residual(x[0], zz, v_gain[j], w_s[j], b_lanes, w_out_c, j),)
        if i < DEPTH - 1:
            x = (mlp_residual(x[0], norm_mlp[i], w_mlp1, w_mlp2, i),)

    y_p, y_s = mlp_residual(x[0], norm_mlp[DEPTH - 1], w_mlp1, w_mlp2, DEPTH - 1,
                            out_rows=(nb_p * SEQ, nb_s * SEQ))
    return y_p.reshape(nb_p, SEQ, D_MODEL), y_s.reshape(nb_s, SEQ, D_MODEL)
```

```python
import functools
import math

import numpy as np
import jax
import jax.numpy as jnp
from jax import lax
from jax.experimental import pallas as pl
from jax.experimental.pallas import tpu as pltpu

F32 = jnp.float32
BF16 = jnp.bfloat16

D_MODEL = 2048
DEPTH = 4
SEQ = 4096
GRID_W = 64
WIN_H = 8
WIN_W = 16
DH = 128
H_A = 8
D_A = H_A * DH
D_B = 1024
EMB = 33
FO = 64
FAST_DECAY = 0.3
SLOW_DECAY = 1.5
DECAY_TARGET = 1e-2
D_C = D_MODEL
G_C = 8
DG_C = D_C // G_C
CHUNK = 128
D_FF = 4 * D_MODEL
EPS = 1e-6

LANES = 128
VMEM_LIMIT_BYTES = 52 * 1024 * 1024

N_FFT = 2 * SEQ
N_FAST = 128
N_SLOW = N_FFT // N_FAST
N_IN = SEQ // N_FAST
N_K1 = N_SLOW // 2 + 1


def _cparams(sem, vmem_limit_bytes=VMEM_LIMIT_BYTES):
    return pltpu.CompilerParams(dimension_semantics=sem, vmem_limit_bytes=vmem_limit_bytes)


def _tile_ranges(parts, tm):
    ranges, lo = [], 0
    for p in parts:
        ranges.append((lo, lo + p.shape[0] // tm))
        lo = ranges[-1][1]
    return ranges


def _part_spec(block, rng, col_fn, cols_outer=False):
    lo, hi = rng

    def index(i, j):
        return jnp.clip(i - lo, 0, hi - lo - 1), jnp.where(_in_range(i, rng), col_fn(j), 0)

    return pl.BlockSpec(block, (lambda j, i: index(i, j)) if cols_outer else index)


def _in_range(i, rng):
    return (i >= rng[0]) & (i < rng[1])


def _rms_scale(x, g):
    return x * lax.rsqrt(jnp.mean(x * x, axis=-1, keepdims=True) + EPS) * g


def _gelu_tanh(x):
    return 0.5 * x * (1.0 + jnp.tanh(math.sqrt(2.0 / math.pi) * (x + 0.044715 * (x * x * x))))


def _split_bf16(x):
    hi = x.astype(BF16)
    lo = (x - hi.astype(F32)).astype(BF16)
    return hi, lo


def _dot3(a, b):
    ah, al = _split_bf16(a)
    bh, bl = _split_bf16(b)
    d = functools.partial(jnp.dot, preferred_element_type=F32)
    return d(ah, bh) + (d(ah, bl) + d(al, bh))


def _in_proj_kernel(*refs, mode, ranges):
    x_refs, refs = refs[:len(ranges)], refs[len(ranges):]
    if mode == "gelu":
        g_ref, w_ref, o_ref, h_ref = refs
    else:
        g_ref, w_ref, qg_ref, kg_ref, o_ref, h_ref = refs
    i, j = pl.program_id(0), pl.program_id(1)

    def tile(rows):
        acc = jnp.dot(h_ref[rows, :], w_ref[...], preferred_element_type=F32)
        if mode == "gelu":
            o_ref[rows, :] = _gelu_tanh(acc).astype(BF16)
        else:
            gain = jnp.where(j == 0, qg_ref[...] * (DH ** -0.5), kg_ref[...])
            for c in range(0, o_ref.shape[1], DH):
                seg = acc[:, c:c + DH]
                o_ref[rows, c:c + DH] = jnp.where(j < 2, _rms_scale(seg, gain), seg).astype(BF16)

    half = o_ref.shape[0] // 2
    for x_ref, rng in zip(x_refs, ranges):
        @pl.when((j == 0) & _in_range(i, rng))
        def _():
            for rows in (slice(0, half), slice(half, 2 * half)):
                h_ref[rows, :] = _rms_scale(x_ref[rows, :], g_ref[...]).astype(BF16)
                tile(rows)

    @pl.when(j > 0)
    def _():
        tile(slice(None))


def in_proj(x_parts, g, w, layer, *, n, mode, qk_gains=(), tn=1024):
    d = x_parts[0].shape[1]
    tm = 1024
    t = sum(p.shape[0] for p in x_parts)
    x_bytes = len(x_parts) * 2 * tm * d * 4
    other_bytes = 2 * d * tn * 2 + 2 * tm * tn * 2 + tm * d * 2 + tm * tn * 4
    vmem_limit = max(VMEM_LIMIT_BYTES, x_bytes + other_bytes + (4 << 20))
    ranges = _tile_ranges(x_parts, tm)
    return pl.pallas_call(
        functools.partial(_in_proj_kernel, mode=mode, ranges=ranges),
        out_shape=jax.ShapeDtypeStruct((t, n), BF16),
        grid=(t // tm, n // tn),
        in_specs=[_part_spec((tm, d), rng, lambda j: 0) for rng in ranges] + [
            pl.BlockSpec((1, d), lambda i, j: (0, 0)),
            pl.BlockSpec((None, d, tn), lambda i, j: (layer, 0, j)),
        ] + [pl.BlockSpec((1, DH), lambda i, j: (0, 0))] * len(qk_gains),
        out_specs=pl.BlockSpec((tm, tn), lambda i, j: (i, j)),
        scratch_shapes=[pltpu.VMEM((tm, d), BF16)],
        compiler_params=_cparams(("parallel", "arbitrary"), vmem_limit),
        name="in_proj",
    )(*x_parts, g.reshape(1, d), w, *[gain.reshape(1, DH) for gain in qk_gains])


def _mlp_kernel(x_ref, g_ref, w1_ref, w2_ref, o_ref, h_ref):
    def tile(rows, first):
        a = jnp.dot(h_ref[rows, :], w1_ref[...], preferred_element_type=F32)
        a = jnp.square(jnp.maximum(a, 0.0)).astype(BF16)
        y = jnp.dot(a, w2_ref[...], preferred_element_type=F32)
        o_ref[rows, :] = (x_ref[rows, :] if first else o_ref[rows, :]) + y

    half = o_ref.shape[0] // 2

    @pl.when(pl.program_id(1) == 0)
    def _():
        for rows in (slice(0, half), slice(half, 2 * half)):
            h_ref[rows, :] = _rms_scale(x_ref[rows, :], g_ref[...]).astype(BF16)
            tile(rows, True)

    @pl.when(pl.program_id(1) > 0)
    def _():
        tile(slice(None), False)


def mlp_residual(x, g, w1, w2, layer, *, row_start=0, rows=None, tm=512, tf=2048):
    t, d = x.shape
    f = w1.shape[2]
    rows = t - row_start if rows is None else rows
    first = row_start // tm
    vmem_need = 2 * 2 * d * tf * 2 + 2 * 2 * tm * d * 4 + tm * d * 2 + tm * tf * 6
    vmem_limit = max(VMEM_LIMIT_BYTES, vmem_need + (4 << 20))
    return pl.pallas_call(
        _mlp_kernel,
        out_shape=jax.ShapeDtypeStruct((rows, d), F32),
        grid=(rows // tm, f // tf),
        in_specs=[
            pl.BlockSpec((tm, d), lambda i, k: (first + i, 0)),
            pl.BlockSpec((1, d), lambda i, k: (0, 0)),
            pl.BlockSpec((None, d, tf), lambda i, k: (layer, 0, k)),
            pl.BlockSpec((None, tf, d), lambda i, k: (layer, k, 0)),
        ],
        out_specs=pl.BlockSpec((tm, d), lambda i, k: (i, 0)),
        scratch_shapes=[pltpu.VMEM((tm, d), BF16)],
        compiler_params=_cparams(("parallel", "arbitrary"), vmem_limit),
        name="mlp_residual",
    )(x, g.reshape(1, d), w1, w2)


def _proj_residual_kernel(*refs, ranges):
    x_refs, (a_ref, b_ref, wa_ref, wb_ref, o_ref) = refs[:len(ranges)], refs[len(ranges):]
    acc = jnp.dot(a_ref[...], wa_ref[...], preferred_element_type=F32)
    acc += jnp.dot(b_ref[...], wb_ref[...], preferred_element_type=F32)
    x = x_refs[0][...]
    for x_ref, rng in zip(x_refs[1:], ranges[1:]):
        x = jnp.where(pl.program_id(1) >= rng[0], x_ref[...], x)
    o_ref[...] = x + acc


def proj_residual(x_parts, a, b, w, layer, *, tm=1024, tn=1024):
    n = x_parts[0].shape[1]
    t = sum(p.shape[0] for p in x_parts)
    ranges = _tile_ranges(x_parts, tm)
    k = w.shape[1] // 2
    return pl.pallas_call(
        functools.partial(_proj_residual_kernel, ranges=ranges),
        out_shape=jax.ShapeDtypeStruct((t, n), F32),
        grid=(n // tn, t // tm),
        in_specs=[_part_spec((tm, tn), rng, lambda j: j, cols_outer=True) for rng in ranges] + [
            pl.BlockSpec((tm, k), lambda j, i: (i, 0)),
            pl.BlockSpec((tm, k), lambda j, i: (i, 0)),
            pl.BlockSpec((None, k, tn), lambda j, i: (layer, 0, j)),
            pl.BlockSpec((None, k, tn), lambda j, i: (layer, 1, j)),
        ],
        out_specs=pl.BlockSpec((tm, tn), lambda j, i: (i, j)),
        compiler_params=_cparams(("arbitrary", "arbitrary")),
        name="proj_residual",
    )(*x_parts, a, b, w, w)


def _gate_proj_kernel(x_ref, zz_ref, vg_ref, ws_ref, bs_ref, w_ref, o_ref, vn_ref, m_ref, *, tm):
    for c in range(tm // CHUNK):
        rows = slice(c * CHUNK, (c + 1) * CHUNK)
        vn_ref[rows, :] = _rms_scale(zz_ref[rows, D_C:].astype(F32), vg_ref[...]).astype(BF16)
        for g in range(G_C):
            cols = slice(g * DG_C, (g + 1) * DG_C)
            s = jnp.dot(ws_ref[g], vn_ref[rows, cols], preferred_element_type=F32)
            bias = bs_ref[g]
            s = s + jnp.concatenate([bias] * (DG_C // LANES), axis=1)
            m_ref[rows, cols] = (zz_ref[rows, cols].astype(F32) * s).astype(BF16)
    o_ref[...] = x_ref[...] + jnp.dot(m_ref[...], w_ref[...], preferred_element_type=F32)


def gate_proj_residual(x, zz, v_gain, w_s, b_s_lanes, w, layer, *, tm=512):
    t = zz.shape[0]
    return pl.pallas_call(
        functools.partial(_gate_proj_kernel, tm=tm),
        out_shape=jax.ShapeDtypeStruct((t, D_MODEL), F32),
        grid=(t // tm,),
        in_specs=[
            pl.BlockSpec((tm, D_MODEL), lambda i: (i, 0)),
            pl.BlockSpec((tm, 2 * D_C), lambda i: (i, 0)),
            pl.BlockSpec((1, D_C), lambda i: (0, 0)),
            pl.BlockSpec((G_C, CHUNK, CHUNK), lambda i: (0, 0, 0)),
            pl.BlockSpec((G_C, CHUNK, LANES), lambda i: (0, 0, 0)),
            pl.BlockSpec((None, D_C, D_MODEL), lambda i: (layer, 0, 0)),
        ],
        out_specs=pl.BlockSpec((tm, D_MODEL), lambda i: (i, 0)),
        scratch_shapes=[pltpu.VMEM((tm, D_C), BF16), pltpu.VMEM((tm, D_C), BF16)],
        compiler_params=_cparams(("parallel",)),
        name="gate_proj_residual",
    )(x, zz, v_gain.reshape(1, D_C), w_s, b_s_lanes, w)


N_ROWS = SEQ // GRID_W
GROUP_ROWS = 4
N_GROUPS = N_ROWS // GROUP_ROWS
KEY_ROWS = 12
NQ = GROUP_ROWS * GRID_W
NK = KEY_ROWS * GRID_W


def _natten_kernel(q_ref, k_ref, v_ref, bias_ref, o_ref, s_ref, p_ref):
    def q_rows(g):
        return pl.ds(g * NQ, NQ)

    def k_rows(g):
        ws = min(max(GROUP_ROWS * g - WIN_H // 2, 0), N_ROWS - KEY_ROWS)
        return pl.ds(ws * GRID_W, NK)

    def scores(g, slot):
        s_ref[slot] = lax.dot_general(q_ref[0, q_rows(g), :], k_ref[0, k_rows(g), :],
                                      (((1,), (1,)), ((), ())), preferred_element_type=F32)

    def softmax(g, slot):
        kind = 0 if g == 0 else (2 if g == N_GROUPS - 1 else 1)
        for a in range(GROUP_ROWS):
            rows = slice(a * GRID_W, (a + 1) * GRID_W)
            w0 = (0, a, KEY_ROWS - WIN_H)[kind]
            lo = (w0 * GRID_W) // LANES * LANES
            hi = -(-(w0 + WIN_H) * GRID_W // LANES) * LANES
            s = s_ref[slot, rows, lo:hi] + bias_ref[0, kind, rows, lo:hi]
            p = jnp.exp(s - jnp.max(s, axis=-1, keepdims=True))
            p = p * (1.0 / jnp.sum(p, axis=-1, keepdims=True))
            pieces = [jnp.zeros((GRID_W, lo), BF16), p.astype(BF16), jnp.zeros((GRID_W, NK - hi), BF16)]
            p_ref[slot, rows, :] = jnp.concatenate([x for x in pieces if x.shape[1]], axis=1)

    def values(g, slot):
        o = jnp.dot(p_ref[slot], v_ref[0, k_rows(g), :], preferred_element_type=F32)
        o_ref[0, q_rows(g), :] = o.astype(BF16)

    for t in range(N_GROUPS + 2):
        if t < N_GROUPS:
            scores(t, t % 2)
        if 1 <= t <= N_GROUPS:
            softmax(t - 1, (t - 1) % 2)
        if t >= 2:
            values(t - 2, t % 2)


def natten(qkv3, bias_tab):
    b = qkv3.shape[0]
    blk = (1, SEQ, DH)
    return pl.pallas_call(
        _natten_kernel,
        out_shape=jax.ShapeDtypeStruct((b, SEQ, D_A), BF16),
        grid=(H_A, b),
        in_specs=[
            pl.BlockSpec(blk, lambda h, i: (i, 0, h)),
            pl.BlockSpec(blk, lambda h, i: (i, 0, H_A + h)),
            pl.BlockSpec(blk, lambda h, i: (i, 0, 2 * H_A + h)),
            pl.BlockSpec((1, 3, NQ, NK), lambda h, i: (h, 0, 0, 0)),
        ],
        out_specs=pl.BlockSpec(blk, lambda h, i: (i, 0, h)),
        scratch_shapes=[pltpu.VMEM((2, NQ, NK), F32), pltpu.VMEM((2, NQ, NK), BF16)],
        compiler_params=_cparams(("parallel", "arbitrary")),
        name="natten",
    )(qkv3, qkv3, qkv3, bias_tab)


def natten_bias_table(rpb):
    col = np.arange(GRID_W)
    cs = np.clip(col - WIN_W // 2, 0, GRID_W - WIN_W)
    qc, kc = col[:, None], col[None, :]
    col_ok = (kc >= cs[:, None]) & (kc < cs[:, None] + WIN_W)
    dc = np.clip(kc - qc, -(WIN_W - 1), WIN_W - 1) + WIN_W - 1
    bias_c = rpb.astype(F32)[:, :, dc]
    a = np.arange(GROUP_ROWS)
    q_off = np.stack([a, a + WIN_H // 2, a + KEY_ROWS - GROUP_ROWS])
    w_off = np.stack([0 * a, a, 0 * a + KEY_ROWS - WIN_H])
    i = np.arange(KEY_ROWS)
    row_ok = (i >= w_off[..., None]) & (i < w_off[..., None] + WIN_H)
    dr = np.clip(i - q_off[..., None] + WIN_H - 1, 0, 2 * WIN_H - 2)
    tab = bias_c[:, dr]
    ok = row_ok[None, :, :, :, None, None] & col_ok[None, None, None, None]
    tab = jnp.where(ok, tab, -jnp.inf)
    return tab.transpose(0, 1, 2, 4, 3, 5).reshape(H_A, 3, NQ, NK)


FILT_ROWS = 512


def _filter_kernel(z_ref, t_ref, w1_ref, b1_ref, w2_ref, b2_ref, w3_ref, b3_ref, fr_ref,
                   wf_ref, wb_ref, dl_ref, hf_ref, hb_ref, h3_ref):
    @pl.when(pl.program_id(0) == 0)
    def _():
        def mlp(i, carry):
            rows = pl.ds(pl.multiple_of(i * FILT_ROWS, FILT_ROWS), FILT_ROWS)
            fr = fr_ref[...]
            h = jnp.sin(fr * (_dot3(z_ref[rows, :], w1_ref[...]) + b1_ref[...]))
            h = jnp.sin(fr * (_dot3(h, w2_ref[...]) + b2_ref[...]))
            h3_ref[rows, :] = jnp.sin(fr * (_dot3(h, w3_ref[...]) + b3_ref[...]))
            return carry

        lax.fori_loop(0, SEQ // FILT_ROWS, mlp, 0)

    def taps(i, carry):
        sf, sb = carry
        rows = pl.ds(pl.multiple_of(i * FILT_ROWS, FILT_ROWS), FILT_ROWS)
        h3 = h3_ref[rows, :]
        decay = jnp.exp(-t_ref[rows, :] * dl_ref[...])
        hf = _dot3(h3, wf_ref[...]) * decay
        hb = _dot3(h3, wb_ref[...]) * decay
        t_idx = lax.broadcasted_iota(jnp.int32, hb.shape, 0) + i * FILT_ROWS
        hb = jnp.where(t_idx == 0, 0.0, hb)
        hf_ref[rows, :] = hf
        hb_ref[rows, :] = hb
        return (sf + jnp.sum(jnp.abs(hf), axis=0, keepdims=True),
                sb + jnp.sum(jnp.abs(hb), axis=0, keepdims=True))

    zero = jnp.zeros((1, LANES), F32)
    sf, sb = lax.fori_loop(0, SEQ // FILT_ROWS, taps, (zero, zero))
    inv = 1.0 / (sf + sb)

    def scale(i, carry):
        rows = pl.ds(pl.multiple_of(i * FILT_ROWS, FILT_ROWS), FILT_ROWS)
        hf_ref[rows, :] = hf_ref[rows, :] * inv
        hb_ref[rows, :] = hb_ref[rows, :] * inv
        return carry

    lax.fori_loop(0, SEQ // FILT_ROWS, scale, 0)


def _filter_constants():
    t = np.linspace(0.0, 1.0, SEQ, dtype=np.float32).astype(np.float64)[:, None]
    bands = (EMB - 1) // 2
    w = 2.0 * math.pi * np.arange(SEQ, dtype=np.float64)[:, None] / SEQ
    f = np.linspace(1e-4, bands - 1, bands, dtype=np.float32).astype(np.float64)[None, :]
    z = np.concatenate([t, np.cos(f * w), -np.sin(f * w)], axis=-1)
    z_pad = np.zeros((SEQ, LANES), np.float32)
    z_pad[:, :EMB] = z
    t_lanes = np.broadcast_to(t.astype(np.float32), (SEQ, LANES)).copy()
    deltas = np.abs(np.linspace(math.log(DECAY_TARGET) / FAST_DECAY,
                                math.log(DECAY_TARGET) / SLOW_DECAY, D_B, dtype=np.float32))
    return z_pad, t_lanes, deltas.reshape(1, D_B)


def hyena_filter(f_w1, f_b1, f_w2, f_b2, f_w3, f_b3, f_wout, f_freq):
    z_pad, t_lanes, deltas = _filter_constants()
    w1_pad = jnp.zeros((LANES, FO), F32).at[:EMB].set(f_w1)
    nblk = D_B // LANES
    const = lambda shape: pl.BlockSpec(shape, lambda c: (0,) * len(shape))
    return pl.pallas_call(
        _filter_kernel,
        out_shape=[jax.ShapeDtypeStruct((SEQ, D_B), F32)] * 2,
        grid=(nblk,),
        in_specs=[
            const((SEQ, LANES)), const((SEQ, LANES)),
            const((LANES, FO)), const((1, FO)),
            const((FO, FO)), const((1, FO)),
            const((FO, FO)), const((1, FO)),
            const((1, FO)),
            pl.BlockSpec((FO, LANES), lambda c: (0, c)),
            pl.BlockSpec((FO, LANES), lambda c: (0, nblk + c)),
            pl.BlockSpec((1, LANES), lambda c: (0, c)),
        ],
        out_specs=[pl.BlockSpec((SEQ, LANES), lambda c: (0, c))] * 2,
        scratch_shapes=[pltpu.VMEM((SEQ, FO), F32)],
        compiler_params=_cparams(("arbitrary",)),
        name="hyena_filter",
    )(jnp.asarray(z_pad), jnp.asarray(t_lanes), w1_pad, f_b1.reshape(1, FO), f_w2,
      f_b2.reshape(1, FO), f_w3, f_b3.reshape(1, FO), f_freq.reshape(1, FO), f_wout, f_wout,
      jnp.asarray(deltas))


ROW_TILE = 8
N_K1_PAD = N_K1 + 1


@functools.lru_cache(maxsize=None)
def _dft_tables():
    k1 = np.arange(N_K1_PAD)
    n2 = np.arange(N_FAST)
    ang = 2.0 * math.pi * np.outer(k1, n2) / N_FFT
    tw = np.concatenate([np.cos(ang), -np.sin(ang)], axis=1)
    tw = np.broadcast_to(tw[:, :, None], (N_K1_PAD, 2 * N_FAST, LANES))
    ang2 = 2.0 * math.pi * np.outer(n2, n2) / N_FAST
    fr, fi = np.cos(ang2), -np.sin(ang2)
    f_fwd = np.block([[fr, -fi], [fi, fr]])
    f_inv = np.block([[fr, fi], [-fi, fr]])
    f32 = lambda a: np.ascontiguousarray(a, dtype=np.float32)
    return f32(tw), f32(f_fwd), f32(f_inv)


def _cmul_const(x, w):
    xr, xi = x
    c, s = w.real, w.imag
    eps = 1e-12
    if abs(s) < eps and abs(c - 1.0) < eps:
        return xr, xi
    if abs(c) < eps and abs(s - 1.0) < eps:
        return ("neg", xi), xr
    if abs(c) < eps and abs(s + 1.0) < eps:
        return xi, ("neg", xr)
    return xr * c - xi * s, xr * s + xi * c


def _is_neg(v):
    return isinstance(v, tuple) and v[0] == "neg"


def _plain(v):
    return -v[1] if _is_neg(v) else v


def _add(a, b):
    return a - b[1] if _is_neg(b) else a + b


def _sub(a, b):
    return a + b[1] if _is_neg(b) else a - b


def _fft_dit(xs, sign):
    n = len(xs)
    if n == 1:
        return xs
    ev = _fft_dit(xs[0::2], sign)
    od = _fft_dit(xs[1::2], sign)
    out = [None] * n
    for k in range(n // 2):
        w = complex(math.cos(2 * math.pi * k / n), sign * math.sin(2 * math.pi * k / n))
        tr, ti = _cmul_const(od[k], w)
        er, ei = ev[k]
        out[k] = (_add(er, tr), _add(ei, ti))
        out[k + n // 2] = (_sub(er, tr), _sub(ei, ti))
    return out


def _slow_forward(x):
    half, quarter = N_SLOW // 2, N_SLOW // 4
    z = [(x[2 * m], x[2 * m + 1]) for m in range(quarter)]
    zt = [z[0]]
    for m in range(1, quarter):
        tr, ti = _cmul_const(z[m], complex(math.cos(2 * math.pi * m / half), -math.sin(2 * math.pi * m / half)))
        zt.append((_plain(tr), _plain(ti)))
    ze = _fft_dit(z, -1)
    zo = _fft_dit(zt, -1)
    zc = [None] * half
    for j in range(quarter):
        zc[2 * j] = ze[j]
        zc[2 * j + 1] = zo[j]
    out = [None] * (half + 1)
    ar, ai = zc[0]
    out[0] = ((ar + ai) * 2.0, None)
    out[half] = ((ar - ai) * 2.0, None)
    ar, ai = zc[quarter]
    out[quarter] = (ar * 2.0, ai * -2.0)
    for k in range(1, quarter):
        ar, ai = zc[k]
        br, bi = zc[half - k]
        c, s = math.cos(2 * math.pi * k / N_SLOW), math.sin(2 * math.pi * k / N_SLOW)
        p, q, r, t = ar + br, ai - bi, ai + bi, ar - br
        u = r * c - t * s
        vn = t * (-c) - r * s
        out[k] = (p + u, q + vn)
        out[half - k] = (p - u, vn - q)
    return out


def _slow_inverse(xs):
    half, quarter = N_SLOW // 2, N_SLOW // 4
    zc = [None] * half
    ar, br = xs[0][0], xs[half][0]
    zc[0] = (ar + br, ar - br)
    ar, ai = xs[quarter]
    zc[quarter] = (ar * 2.0, ai * -2.0)
    for k in range(1, quarter):
        ar, ai = xs[k]
        br, bi = xs[half - k]
        c, s = math.cos(2 * math.pi * k / N_SLOW), math.sin(2 * math.pi * k / N_SLOW)
        p, q, t, r = ar + br, ai - bi, ar - br, ai + bi
        o_r = t * c - r * s
        o_i = t * s + r * c
        zc[k] = (p - o_i, q + o_r)
        zc[half - k] = (p + o_i, o_r - q)
    ze = _fft_dit(zc[0::2], +1)
    zo = _fft_dit(zc[1::2], +1)
    out = [None] * N_IN
    for m in range(quarter):
        tr, ti = _cmul_const(zo[m], complex(math.cos(2 * math.pi * m / half), math.sin(2 * math.pi * m / half)))
        out[2 * m] = _add(ze[m][0], tr)
        out[2 * m + 1] = _add(ze[m][1], ti)
    return out


def _slow_forward_pass(src_ref, a_ref):
    lanes = src_ref.shape[-1]
    zeros = jnp.zeros((ROW_TILE, lanes), F32)

    def body(j, carry):
        rows = pl.ds(pl.multiple_of(j * ROW_TILE, ROW_TILE), ROW_TILE)
        rows_im = pl.ds(pl.multiple_of(N_FAST + j * ROW_TILE, ROW_TILE), ROW_TILE)
        out = _slow_forward([src_ref[n1, rows, :] for n1 in range(N_IN)])
        for k1, (re, im) in enumerate(out):
            a_ref[k1, rows, :] = re
            a_ref[k1, rows_im, :] = zeros if im is None else im
        a_ref[N_K1, rows, :] = zeros
        a_ref[N_K1, rows_im, :] = zeros
        return carry

    lax.fori_loop(0, N_FAST // ROW_TILE, body, 0)


def _cmul(a, w):
    ar, ai, wr, wi = a[:N_FAST], a[N_FAST:], w[:N_FAST], w[N_FAST:]
    return ar * wr - ai * wi, ar * wi + ai * wr


def _cmul_conj(a, w):
    ar, ai, wr, wi = a[:N_FAST], a[N_FAST:], w[:N_FAST], w[N_FAST:]
    return ar * wr + ai * wi, ai * wr - ar * wi


def _spectrum_kernel(hf_ref, hb_ref, tw_ref, f_ref, o_ref, af_ref, ab_ref):
    _slow_forward_pass(hf_ref, af_ref)
    _slow_forward_pass(hb_ref, ab_ref)

    def body(k1, carry):
        tw = tw_ref[k1]
        f = f_ref[...]
        both = jnp.concatenate([jnp.concatenate(_cmul(af_ref[k1], tw), axis=0),
                                jnp.concatenate(_cmul(ab_ref[k1], tw), axis=0)], axis=1)
        x = _dot3(f, both)
        xf, xb = x[:, :LANES], x[:, LANES:]
        re = xf[:N_FAST] + xb[:N_FAST]
        im = xf[N_FAST:] - xb[N_FAST:]
        o_ref[k1] = jnp.concatenate([re, im], axis=0) * (1.0 / (4 * N_FFT))
        return carry

    lax.fori_loop(0, N_K1_PAD, body, 0)


def filter_spectrum(hf, hb, tw, f_fwd):
    nblk = D_B // LANES
    slabs = lambda a: a.reshape(N_IN, N_FAST, D_B)
    whole = pl.BlockSpec(memory_space=pltpu.VMEM)
    spec_blk = (N_K1_PAD, 2 * N_FAST, LANES)
    return pl.pallas_call(
        _spectrum_kernel,
        out_shape=jax.ShapeDtypeStruct((N_K1_PAD, 2 * N_FAST, D_B), F32),
        grid=(nblk,),
        in_specs=[
            pl.BlockSpec((N_IN, N_FAST, LANES), lambda c: (0, 0, c)),
            pl.BlockSpec((N_IN, N_FAST, LANES), lambda c: (0, 0, c)),
            whole, whole,
        ],
        out_specs=pl.BlockSpec(spec_blk, lambda c: (0, 0, c)),
        scratch_shapes=[pltpu.VMEM(spec_blk, F32)] * 2,
        compiler_params=_cparams(("arbitrary",)),
        name="filter_spectrum",
    )(slabs(hf), slabs(hb), tw, f_fwd)


def _hyena_kernel(x0_ref, x1_ref, v_ref, w0_ref, w1_ref, wv_ref, b0_ref, b1_ref, bv_ref,
                  hs_ref, hbias_ref, tw_ref, ff_ref, fi_ref, o_ref, u_ref, g0_ref, a_ref, z_ref, st_ref):
    zero_rows = jnp.zeros((ROW_TILE, LANES), F32)
    for op in range(3):
        st_ref[op, 0:ROW_TILE, :] = zero_rows
        st_ref[op, SEQ + ROW_TILE:SEQ + 2 * ROW_TILE, :] = zero_rows

    def stage(n1, carry):
        r0 = pl.multiple_of(n1 * N_FAST, N_FAST)
        for op, x_ref in enumerate((x0_ref, x1_ref, v_ref)):
            st_ref[op, pl.ds(r0 + ROW_TILE, N_FAST), :] = x_ref[0, pl.ds(r0, N_FAST), :].astype(F32)
        return carry

    lax.fori_loop(0, N_IN, stage, 0)

    def tap3(op, w_ref, b_ref, n1):
        r0 = n1 * N_FAST + ROW_TILE
        prev = st_ref[op, pl.ds(r0 - 1, N_FAST), :]
        cur = st_ref[op, pl.ds(r0, N_FAST), :]
        nxt = st_ref[op, pl.ds(r0 + 1, N_FAST), :]
        return prev * w_ref[0:1, :] + cur * w_ref[1:2, :] + nxt * w_ref[2:3, :] + b_ref[...]

    def conv(n1, carry):
        u_ref[n1] = tap3(2, wv_ref, bv_ref, n1) * tap3(1, w1_ref, b1_ref, n1)
        g0_ref[n1] = tap3(0, w0_ref, b0_ref, n1)
        return carry

    lax.fori_loop(0, N_IN, conv, 0)
    _slow_forward_pass(u_ref, a_ref)

    def freq(i, carry):
        ks = (2 * i, 2 * i + 1)
        tws = [tw_ref[k] for k in ks]
        a = [jnp.concatenate(_cmul(a_ref[k], tw), axis=0).astype(BF16) for k, tw in zip(ks, tws)]
        x = jnp.dot(ff_ref[...], jnp.concatenate(a, axis=1), preferred_element_type=F32)
        y = [jnp.concatenate(_cmul(x[:, j * LANES:(j + 1) * LANES], hs_ref[k]), axis=0).astype(BF16)
             for j, k in enumerate(ks)]
        z = jnp.dot(fi_ref[...], jnp.concatenate(y, axis=1), preferred_element_type=F32)
        for j, (k, tw) in enumerate(zip(ks, tws)):
            z_ref[k] = jnp.concatenate(_cmul_conj(z[:, j * LANES:(j + 1) * LANES], tw), axis=0)
        return carry

    lax.fori_loop(0, N_K1_PAD // 2, freq, 0, unroll=True)

    def back(j, carry):
        rows = pl.ds(pl.multiple_of(j * ROW_TILE, ROW_TILE), ROW_TILE)
        rows_im = pl.ds(pl.multiple_of(N_FAST + j * ROW_TILE, ROW_TILE), ROW_TILE)
        y = _slow_inverse([(z_ref[k1, rows, :], z_ref[k1, rows_im, :]) for k1 in range(N_K1)])
        for n1 in range(N_IN):
            u_ref[n1, rows, :] = y[n1] + u_ref[n1, rows, :] * hbias_ref[...]
        return carry

    lax.fori_loop(0, N_FAST // ROW_TILE, back, 0)

    def gate(n1, carry):
        rows = pl.ds(pl.multiple_of(n1 * N_FAST, N_FAST), N_FAST)
        o_ref[0, rows, :] = (u_ref[n1] * g0_ref[n1]).astype(BF16)
        return carry

    lax.fori_loop(0, N_IN, gate, 0)


def hyena(p3, sc_w, sc_b, h_spec, h_bias, tables):
    tw, f_fwd, f_inv = tables
    b = p3.shape[0]
    nblk = D_B // LANES
    base = 3 * D_A // LANES
    xblk = lambda off: pl.BlockSpec((1, SEQ, LANES), lambda c, i: (i, 0, base + off * nblk + c))
    wblk = lambda off: pl.BlockSpec((3, LANES), lambda c, i: (0, off * nblk + c))
    bblk = lambda off: pl.BlockSpec((1, LANES), lambda c, i: (0, off * nblk + c))
    whole = pl.BlockSpec(memory_space=pltpu.VMEM)
    sc_b2 = sc_b.reshape(1, 3 * D_B)
    spec_blk = (N_K1_PAD, 2 * N_FAST, LANES)
    return pl.pallas_call(
        _hyena_kernel,
        out_shape=jax.ShapeDtypeStruct((b, SEQ, D_B), BF16),
        grid=(nblk, b),
        in_specs=[
            xblk(0), xblk(1), xblk(2),
            wblk(0), wblk(1), wblk(2),
            bblk(0), bblk(1), bblk(2),
            pl.BlockSpec(spec_blk, lambda c, i: (0, 0, c)),
            pl.BlockSpec((1, LANES), lambda c, i: (0, c)),
            whole, whole, whole,
        ],
        out_specs=pl.BlockSpec((1, SEQ, LANES), lambda c, i: (i, 0, c)),
        scratch_shapes=[
            pltpu.VMEM((N_IN, N_FAST, LANES), F32),
            pltpu.VMEM((N_IN, N_FAST, LANES), F32),
            pltpu.VMEM(spec_blk, F32),
            pltpu.VMEM(spec_blk, F32),
            pltpu.VMEM((3, SEQ + 2 * ROW_TILE, LANES), F32),
        ],
        compiler_params=_cparams(("parallel", "arbitrary")),
        name="hyena",
    )(p3, p3, p3, sc_w, sc_w, sc_w, sc_b2, sc_b2, sc_b2,
      h_spec, h_bias.reshape(1, D_B), tw, f_fwd.astype(BF16), f_inv.astype(BF16))


def kernel(x_prompt, x_sample, norm_mix, norm_mlp, w_in_ab, sc_w, sc_b, q_gain, k_gain, rpb,
           f_w1, f_b1, f_w2, f_b2, f_w3, f_b3, f_wout, f_freq, h_bias, w_out_ab, w_in_c,
           v_gain, w_s, b_s, w_out_c, w_mlp1, w_mlp2):
    assert x_prompt.shape[1:] == (SEQ, D_MODEL) and x_sample.shape[1:] == (SEQ, D_MODEL)
    assert w_mlp1.shape == (DEPTH, D_MODEL, D_FF) and w_in_ab.shape[1:] == (D_MODEL, 3 * (D_A + D_B))
    nb_p, nb_s = x_prompt.shape[0], x_sample.shape[0]
    nb = nb_p + nb_s
    x = (x_prompt.reshape(nb_p * SEQ, D_MODEL), x_sample.reshape(nb_s * SEQ, D_MODEL))
    tables = tuple(jnp.asarray(a) for a in _dft_tables())

    w_in_ab, w_out_ab, w_in_c, w_out_c, w_mlp1, w_mlp2, w_s = (
        w.astype(BF16) for w in (w_in_ab, w_out_ab, w_in_c, w_out_c, w_mlp1, w_mlp2, w_s))

    for i in range(DEPTH):
        j = i // 2
        if i % 2 == 0:
            p3 = in_proj(x, norm_mix[i], w_in_ab, j, n=3 * (D_A + D_B), mode="qk_norm",
                         qk_gains=(q_gain[j], k_gain[j])).reshape(nb, SEQ, 3 * (D_A + D_B))
            a = natten(p3, natten_bias_table(rpb[j]))
            hf, hb = hyena_filter(f_w1[j], f_b1[j], f_w2[j], f_b2[j], f_w3[j], f_b3[j],
                                  f_wout[j], f_freq[j])
            h_spec = filter_spectrum(hf, hb, tables[0], tables[1])
            bo = hyena(p3, sc_w[j], sc_b[j], h_spec, h_bias[j], tables)
            x = (proj_residual(x, a.reshape(nb * SEQ, D_A), bo.reshape(nb * SEQ, D_B), w_out_ab, j),)
        else:
            zz = in_proj(x, norm_mix[i], w_in_c, j, n=2 * D_C, mode="gelu")
            b_lanes = jnp.broadcast_to(b_s[j][:, :, None], (G_C, CHUNK, LANES))
            x = (gate_proj_residual(x[0], zz, v_gain[j], w_s[j], b_lanes, w_out_c, j),)
        if i < DEPTH - 1:
            x = (mlp_residual(x[0], norm_mlp[i], w_mlp1, w_mlp2, i),)

    last = DEPTH - 1
    y_p = mlp_residual(x[0], norm_mlp[last], w_mlp1, w_mlp2, last, rows=nb_p * SEQ)
    y_s = mlp_residual(x[0], norm_mlp[last], w_mlp1, w_mlp2, last, row_start=nb_p * SEQ)
    return y_p.reshape(nb_p, SEQ, D_MODEL), y_s.reshape(nb_s, SEQ, D_MODEL)
```

```python
import functools
import math

import numpy as np
import jax
import jax.numpy as jnp
from jax import lax
from jax.experimental import pallas as pl
from jax.experimental.pallas import tpu as pltpu

F32 = jnp.float32
BF16 = jnp.bfloat16

D_MODEL = 2048
DEPTH = 4
SEQ = 4096
GRID_W = 64
WIN_H = 8
WIN_W = 16
DH = 128
H_A = 8
D_A = H_A * DH
D_B = 1024
EMB = 33
FO = 64
FAST_DECAY = 0.3
SLOW_DECAY = 1.5
DECAY_TARGET = 1e-2
D_C = D_MODEL
G_C = 8
DG_C = D_C // G_C
CHUNK = 128
D_FF = 4 * D_MODEL
EPS = 1e-6

LANES = 128
VMEM_LIMIT_BYTES = 52 * 1024 * 1024

N_FFT = 2 * SEQ
N_FAST = 128
N_SLOW = N_FFT // N_FAST
N_IN = SEQ // N_FAST
N_K1 = N_SLOW // 2 + 1


def _cparams(sem, vmem_limit_bytes=VMEM_LIMIT_BYTES):
    return pltpu.CompilerParams(dimension_semantics=sem, vmem_limit_bytes=vmem_limit_bytes)


def _tile_ranges(parts, tm):
    ranges, lo = [], 0
    for p in parts:
        ranges.append((lo, lo + p.shape[0] // tm))
        lo = ranges[-1][1]
    return ranges


def _part_spec(block, rng, col_fn, cols_outer=False):
    lo, hi = rng

    def index(i, j):
        return jnp.clip(i - lo, 0, hi - lo - 1), jnp.where(_in_range(i, rng), col_fn(j), 0)

    return pl.BlockSpec(block, (lambda j, i: index(i, j)) if cols_outer else index)


def _in_range(i, rng):
    return (i >= rng[0]) & (i < rng[1])


def _rms_scale(x, g):
    return x * lax.rsqrt(jnp.mean(x * x, axis=-1, keepdims=True) + EPS) * g


def _gelu_tanh(x):
    return 0.5 * x * (1.0 + jnp.tanh(math.sqrt(2.0 / math.pi) * (x + 0.044715 * (x * x * x))))


def _split_bf16(x):
    hi = x.astype(BF16)
    lo = (x - hi.astype(F32)).astype(BF16)
    return hi, lo


def _dot3(a, b):
    ah, al = _split_bf16(a)
    bh, bl = _split_bf16(b)
    d = functools.partial(jnp.dot, preferred_element_type=F32)
    return d(ah, bh) + (d(ah, bl) + d(al, bh))


def _in_proj_kernel(*refs, mode, ranges):
    x_refs, refs = refs[:len(ranges)], refs[len(ranges):]
    if mode == "gelu":
        g_ref, w_ref, o_ref, h_ref = refs
    else:
        g_ref, w_ref, qg_ref, kg_ref, o_ref, h_ref = refs
    i, j = pl.program_id(0), pl.program_id(1)

    def tile(rows):
        acc = jnp.dot(h_ref[rows, :], w_ref[...], preferred_element_type=F32)
        if mode == "gelu":
            o_ref[rows, :] = _gelu_tanh(acc).astype(BF16)
        else:
            gain = jnp.where(j == 0, qg_ref[...] * (DH ** -0.5), kg_ref[...])
            for c in range(0, o_ref.shape[1], DH):
                seg = acc[:, c:c + DH]
                o_ref[rows, c:c + DH] = jnp.where(j < 2, _rms_scale(seg, gain), seg).astype(BF16)

    half = o_ref.shape[0] // 2
    for x_ref, rng in zip(x_refs, ranges):
        @pl.when((j == 0) & _in_range(i, rng))
        def _():
            for rows in (slice(0, half), slice(half, 2 * half)):
                h_ref[rows, :] = _rms_scale(x_ref[rows, :], g_ref[...]).astype(BF16)
                tile(rows)

    @pl.when(j > 0)
    def _():
        tile(slice(None))


def in_proj(x_parts, g, w, layer, *, n, mode, qk_gains=(), tn=1024):
    d = x_parts[0].shape[1]
    tm = 1024
    t = sum(p.shape[0] for p in x_parts)
    x_bytes = len(x_parts) * 2 * tm * d * 4
    other_bytes = 2 * d * tn * 2 + 2 * tm * tn * 2 + tm * d * 2 + tm * tn * 4
    vmem_limit = max(VMEM_LIMIT_BYTES, x_bytes + other_bytes + (4 << 20))
    ranges = _tile_ranges(x_parts, tm)
    return pl.pallas_call(
        functools.partial(_in_proj_kernel, mode=mode, ranges=ranges),
        out_shape=jax.ShapeDtypeStruct((t, n), BF16),
        grid=(t // tm, n // tn),
        in_specs=[_part_spec((tm, d), rng, lambda j: 0) for rng in ranges] + [
            pl.BlockSpec((1, d), lambda i, j: (0, 0)),
            pl.BlockSpec((None, d, tn), lambda i, j: (layer, 0, j)),
        ] + [pl.BlockSpec((1, DH), lambda i, j: (0, 0))] * len(qk_gains),
        out_specs=pl.BlockSpec((tm, tn), lambda i, j: (i, j)),
        scratch_shapes=[pltpu.VMEM((tm, d), BF16)],
        compiler_params=_cparams(("parallel", "arbitrary"), vmem_limit),
        name="in_proj",
    )(*x_parts, g.reshape(1, d), w, *[gain.reshape(1, DH) for gain in qk_gains])


def _mlp_kernel(x_ref, g_ref, w1_ref, w2_ref, o_ref, h_ref):
    def tile(rows, first):
        a = jnp.dot(h_ref[rows, :], w1_ref[...], preferred_element_type=F32)
        a = jnp.square(jnp.maximum(a, 0.0)).astype(BF16)
        y = jnp.dot(a, w2_ref[...], preferred_element_type=F32)
        o_ref[rows, :] = (x_ref[rows, :] if first else o_ref[rows, :]) + y

    half = o_ref.shape[0] // 2

    @pl.when(pl.program_id(1) == 0)
    def _():
        for rows in (slice(0, half), slice(half, 2 * half)):
            h_ref[rows, :] = _rms_scale(x_ref[rows, :], g_ref[...]).astype(BF16)
            tile(rows, True)

    @pl.when(pl.program_id(1) > 0)
    def _():
        tile(slice(None), False)


def mlp_residual(x, g, w1, w2, layer, *, row_start=0, rows=None, tm=512, tf=2048):
    t, d = x.shape
    f = w1.shape[2]
    rows = t - row_start if rows is None else rows
    first = row_start // tm
    vmem_need = 2 * 2 * d * tf * 2 + 2 * 2 * tm * d * 4 + tm * d * 2 + tm * tf * 6
    vmem_limit = max(VMEM_LIMIT_BYTES, vmem_need + (4 << 20))
    return pl.pallas_call(
        _mlp_kernel,
        out_shape=jax.ShapeDtypeStruct((rows, d), F32),
        grid=(rows // tm, f // tf),
        in_specs=[
            pl.BlockSpec((tm, d), lambda i, k: (first + i, 0)),
            pl.BlockSpec((1, d), lambda i, k: (0, 0)),
            pl.BlockSpec((None, d, tf), lambda i, k: (layer, 0, k)),
            pl.BlockSpec((None, tf, d), lambda i, k: (layer, k, 0)),
        ],
        out_specs=pl.BlockSpec((tm, d), lambda i, k: (i, 0)),
        scratch_shapes=[pltpu.VMEM((tm, d), BF16)],
        compiler_params=_cparams(("parallel", "arbitrary"), vmem_limit),
        name="mlp_residual",
    )(x, g.reshape(1, d), w1, w2)


def _proj_residual_kernel(*refs, ranges):
    x_refs, (a_ref, b_ref, wa_ref, wb_ref, o_ref) = refs[:len(ranges)], refs[len(ranges):]
    acc = jnp.dot(a_ref[...], wa_ref[...], preferred_element_type=F32)
    acc += jnp.dot(b_ref[...], wb_ref[...], preferred_element_type=F32)
    x = x_refs[0][...]
    for x_ref, rng in zip(x_refs[1:], ranges[1:]):
        x = jnp.where(pl.program_id(1) >= rng[0], x_ref[...], x)
    o_ref[...] = x + acc


def proj_residual(x_parts, a, b, w, layer, *, tm=1024, tn=1024):
    n = x_parts[0].shape[1]
    t = sum(p.shape[0] for p in x_parts)
    ranges = _tile_ranges(x_parts, tm)
    k = w.shape[1] // 2
    return pl.pallas_call(
        functools.partial(_proj_residual_kernel, ranges=ranges),
        out_shape=jax.ShapeDtypeStruct((t, n), F32),
        grid=(n // tn, t // tm),
        in_specs=[_part_spec((tm, tn), rng, lambda j: j, cols_outer=True) for rng in ranges] + [
            pl.BlockSpec((tm, k), lambda j, i: (i, 0)),
            pl.BlockSpec((tm, k), lambda j, i: (i, 0)),
            pl.BlockSpec((None, k, tn), lambda j, i: (layer, 0, j)),
            pl.BlockSpec((None, k, tn), lambda j, i: (layer, 1, j)),
        ],
        out_specs=pl.BlockSpec((tm, tn), lambda j, i: (i, j)),
        compiler_params=_cparams(("arbitrary", "arbitrary")),
        name="proj_residual",
    )(*x_parts, a, b, w, w)


def _gate_proj_kernel(x_ref, zz_ref, vg_ref, ws_ref, bs_ref, w_ref, o_ref, vn_ref, m_ref, *, tm):
    for c in range(tm // CHUNK):
        rows = slice(c * CHUNK, (c + 1) * CHUNK)
        vn_ref[rows, :] = _rms_scale(zz_ref[rows, D_C:].astype(F32), vg_ref[...]).astype(BF16)
        for g in range(G_C):
            cols = slice(g * DG_C, (g + 1) * DG_C)
            s = jnp.dot(ws_ref[g], vn_ref[rows, cols], preferred_element_type=F32)
            bias = bs_ref[g]
            s = s + jnp.concatenate([bias] * (DG_C // LANES), axis=1)
            m_ref[rows, cols] = (zz_ref[rows, cols].astype(F32) * s).astype(BF16)
    o_ref[...] = x_ref[...] + jnp.dot(m_ref[...], w_ref[...], preferred_element_type=F32)


def gate_proj_residual(x, zz, v_gain, w_s, b_s_lanes, w, layer, *, tm=512):
    t = zz.shape[0]
    return pl.pallas_call(
        functools.partial(_gate_proj_kernel, tm=tm),
        out_shape=jax.ShapeDtypeStruct((t, D_MODEL), F32),
        grid=(t // tm,),
        in_specs=[
            pl.BlockSpec((tm, D_MODEL), lambda i: (i, 0)),
            pl.BlockSpec((tm, 2 * D_C), lambda i: (i, 0)),
            pl.BlockSpec((1, D_C), lambda i: (0, 0)),
            pl.BlockSpec((G_C, CHUNK, CHUNK), lambda i: (0, 0, 0)),
            pl.BlockSpec((G_C, CHUNK, LANES), lambda i: (0, 0, 0)),
            pl.BlockSpec((None, D_C, D_MODEL), lambda i: (layer, 0, 0)),
        ],
        out_specs=pl.BlockSpec((tm, D_MODEL), lambda i: (i, 0)),
        scratch_shapes=[pltpu.VMEM((tm, D_C), BF16), pltpu.VMEM((tm, D_C), BF16)],
        compiler_params=_cparams(("parallel",)),
        name="gate_proj_residual",
    )(x, zz, v_gain.reshape(1, D_C), w_s, b_s_lanes, w)


N_ROWS = SEQ // GRID_W
GROUP_ROWS = 4
N_GROUPS = N_ROWS // GROUP_ROWS
KEY_ROWS = 12
NQ = GROUP_ROWS * GRID_W
NK = KEY_ROWS * GRID_W


def _natten_kernel(q_ref, k_ref, v_ref, bias_ref, o_ref, s_ref, p_ref):
    def q_rows(g):
        return pl.ds(g * NQ, NQ)

    def k_rows(g):
        ws = min(max(GROUP_ROWS * g - WIN_H // 2, 0), N_ROWS - KEY_ROWS)
        return pl.ds(ws * GRID_W, NK)

    def scores(g, slot):
        s_ref[slot] = lax.dot_general(q_ref[0, q_rows(g), :], k_ref[0, k_rows(g), :],
                                      (((1,), (1,)), ((), ())), preferred_element_type=F32)

    def softmax(g, slot):
        kind = 0 if g == 0 else (2 if g == N_GROUPS - 1 else 1)
        for a in range(GROUP_ROWS):
            rows = slice(a * GRID_W, (a + 1) * GRID_W)
            w0 = (0, a, KEY_ROWS - WIN_H)[kind]
            lo = (w0 * GRID_W) // LANES * LANES
            hi = -(-(w0 + WIN_H) * GRID_W // LANES) * LANES
            s = s_ref[slot, rows, lo:hi] + bias_ref[0, kind, rows, lo:hi]
            p = jnp.exp(s - jnp.max(s, axis=-1, keepdims=True))
            p = p * (1.0 / jnp.sum(p, axis=-1, keepdims=True))
            pieces = [jnp.zeros((GRID_W, lo), BF16), p.astype(BF16), jnp.zeros((GRID_W, NK - hi), BF16)]
            p_ref[slot, rows, :] = jnp.concatenate([x for x in pieces if x.shape[1]], axis=1)

    def values(g, slot):
        o = jnp.dot(p_ref[slot], v_ref[0, k_rows(g), :], preferred_element_type=F32)
        o_ref[0, q_rows(g), :] = o.astype(BF16)

    for t in range(N_GROUPS + 2):
        if t < N_GROUPS:
            scores(t, t % 2)
        if 1 <= t <= N_GROUPS:
            softmax(t - 1, (t - 1) % 2)
        if t >= 2:
            values(t - 2, t % 2)


def natten(qkv3, bias_tab):
    b = qkv3.shape[0]
    blk = (1, SEQ, DH)
    return pl.pallas_call(
        _natten_kernel,
        out_shape=jax.ShapeDtypeStruct((b, SEQ, D_A), BF16),
        grid=(H_A, b),
        in_specs=[
            pl.BlockSpec(blk, lambda h, i: (i, 0, h)),
            pl.BlockSpec(blk, lambda h, i: (i, 0, H_A + h)),
            pl.BlockSpec(blk, lambda h, i: (i, 0, 2 * H_A + h)),
            pl.BlockSpec((1, 3, NQ, NK), lambda h, i: (h, 0, 0, 0)),
        ],
        out_specs=pl.BlockSpec(blk, lambda h, i: (i, 0, h)),
        scratch_shapes=[pltpu.VMEM((2, NQ, NK), F32), pltpu.VMEM((2, NQ, NK), BF16)],
        compiler_params=_cparams(("parallel", "arbitrary")),
        name="natten",
    )(qkv3, qkv3, qkv3, bias_tab)


def natten_bias_table(rpb):
    col = np.arange(GRID_W)
    cs = np.clip(col - WIN_W // 2, 0, GRID_W - WIN_W)
    qc, kc = col[:, None], col[None, :]
    col_ok = (kc >= cs[:, None]) & (kc < cs[:, None] + WIN_W)
    dc = np.clip(kc - qc, -(WIN_W - 1), WIN_W - 1) + WIN_W - 1
    bias_c = rpb.astype(F32)[:, :, dc]
    a = np.arange(GROUP_ROWS)
    q_off = np.stack([a, a + WIN_H // 2, a + KEY_ROWS - GROUP_ROWS])
    w_off = np.stack([0 * a, a, 0 * a + KEY_ROWS - WIN_H])
    i = np.arange(KEY_ROWS)
    row_ok = (i >= w_off[..., None]) & (i < w_off[..., None] + WIN_H)
    dr = np.clip(i - q_off[..., None] + WIN_H - 1, 0, 2 * WIN_H - 2)
    tab = bias_c[:, dr]
    ok = row_ok[None, :, :, :, None, None] & col_ok[None, None, None, None]
    tab = jnp.where(ok, tab, -jnp.inf)
    return tab.transpose(0, 1, 2, 4, 3, 5).reshape(H_A, 3, NQ, NK)


FILT_ROWS = 512


def _filter_kernel(z_ref, t_ref, w1_ref, b1_ref, w2_ref, b2_ref, w3_ref, b3_ref, fr_ref,
                   wf_ref, wb_ref, dl_ref, hf_ref, hb_ref, h3_ref):
    @pl.when(pl.program_id(0) == 0)
    def _():
        def mlp(i, carry):
            rows = pl.ds(pl.multiple_of(i * FILT_ROWS, FILT_ROWS), FILT_ROWS)
            fr = fr_ref[...]
            h = jnp.sin(fr * (_dot3(z_ref[rows, :], w1_ref[...]) + b1_ref[...]))
            h = jnp.sin(fr * (_dot3(h, w2_ref[...]) + b2_ref[...]))
            h3_ref[rows, :] = jnp.sin(fr * (_dot3(h, w3_ref[...]) + b3_ref[...]))
            return carry

        lax.fori_loop(0, SEQ // FILT_ROWS, mlp, 0)

    def taps(i, carry):
        sf, sb = carry
        rows = pl.ds(pl.multiple_of(i * FILT_ROWS, FILT_ROWS), FILT_ROWS)
        h3 = h3_ref[rows, :]
        decay = jnp.exp(-t_ref[rows, :] * dl_ref[...])
        hf = _dot3(h3, wf_ref[...]) * decay
        hb = _dot3(h3, wb_ref[...]) * decay
        t_idx = lax.broadcasted_iota(jnp.int32, hb.shape, 0) + i * FILT_ROWS
        hb = jnp.where(t_idx == 0, 0.0, hb)
        hf_ref[rows, :] = hf
        hb_ref[rows, :] = hb
        return (sf + jnp.sum(jnp.abs(hf), axis=0, keepdims=True),
                sb + jnp.sum(jnp.abs(hb), axis=0, keepdims=True))

    zero = jnp.zeros((1, LANES), F32)
    sf, sb = lax.fori_loop(0, SEQ // FILT_ROWS, taps, (zero, zero))
    inv = 1.0 / (sf + sb)

    def scale(i, carry):
        rows = pl.ds(pl.multiple_of(i * FILT_ROWS, FILT_ROWS), FILT_ROWS)
        hf_ref[rows, :] = hf_ref[rows, :] * inv
        hb_ref[rows, :] = hb_ref[rows, :] * inv
        return carry

    lax.fori_loop(0, SEQ // FILT_ROWS, scale, 0)


def _filter_constants():
    t = np.linspace(0.0, 1.0, SEQ, dtype=np.float32).astype(np.float64)[:, None]
    bands = (EMB - 1) // 2
    w = 2.0 * math.pi * np.arange(SEQ, dtype=np.float64)[:, None] / SEQ
    f = np.linspace(1e-4, bands - 1, bands, dtype=np.float32).astype(np.float64)[None, :]
    z = np.concatenate([t, np.cos(f * w), -np.sin(f * w)], axis=-1)
    z_pad = np.zeros((SEQ, LANES), np.float32)
    z_pad[:, :EMB] = z
    t_lanes = np.broadcast_to(t.astype(np.float32), (SEQ, LANES)).copy()
    deltas = np.abs(np.linspace(math.log(DECAY_TARGET) / FAST_DECAY,
                                math.log(DECAY_TARGET) / SLOW_DECAY, D_B, dtype=np.float32))
    return z_pad, t_lanes, deltas.reshape(1, D_B)


def hyena_filter(f_w1, f_b1, f_w2, f_b2, f_w3, f_b3, f_wout, f_freq):
    z_pad, t_lanes, deltas = _filter_constants()
    w1_pad = jnp.zeros((LANES, FO), F32).at[:EMB].set(f_w1)
    nblk = D_B // LANES
    const = lambda shape: pl.BlockSpec(shape, lambda c: (0,) * len(shape))
    return pl.pallas_call(
        _filter_kernel,
        out_shape=[jax.ShapeDtypeStruct((SEQ, D_B), F32)] * 2,
        grid=(nblk,),
        in_specs=[
            const((SEQ, LANES)), const((SEQ, LANES)),
            const((LANES, FO)), const((1, FO)),
            const((FO, FO)), const((1, FO)),
            const((FO, FO)), const((1, FO)),
            const((1, FO)),
            pl.BlockSpec((FO, LANES), lambda c: (0, c)),
            pl.BlockSpec((FO, LANES), lambda c: (0, nblk + c)),
            pl.BlockSpec((1, LANES), lambda c: (0, c)),
        ],
        out_specs=[pl.BlockSpec((SEQ, LANES), lambda c: (0, c))] * 2,
        scratch_shapes=[pltpu.VMEM((SEQ, FO), F32)],
        compiler_params=_cparams(("arbitrary",)),
        name="hyena_filter",
    )(jnp.asarray(z_pad), jnp.asarray(t_lanes), w1_pad, f_b1.reshape(1, FO), f_w2,
      f_b2.reshape(1, FO), f_w3, f_b3.reshape(1, FO), f_freq.reshape(1, FO), f_wout, f_wout,
      jnp.asarray(deltas))


ROW_TILE = 8
N_K1_PAD = N_K1 + 1


@functools.lru_cache(maxsize=None)
def _dft_tables():
    k1 = np.arange(N_K1_PAD)
    n2 = np.arange(N_FAST)
    ang = 2.0 * math.pi * np.outer(k1, n2) / N_FFT
    tw = np.concatenate([np.cos(ang), -np.sin(ang)], axis=1)
    tw = np.broadcast_to(tw[:, :, None], (N_K1_PAD, 2 * N_FAST, LANES))
    ang2 = 2.0 * math.pi * np.outer(n2, n2) / N_FAST
    fr, fi = np.cos(ang2), -np.sin(ang2)
    f_fwd = np.block([[fr, -fi], [fi, fr]])
    f_inv = np.block([[fr, fi], [-fi, fr]])
    f32 = lambda a: np.ascontiguousarray(a, dtype=np.float32)
    return f32(tw), f32(f_fwd), f32(f_inv)


def _cmul_const(x, w):
    xr, xi = x
    c, s = w.real, w.imag
    eps = 1e-12
    if abs(s) < eps and abs(c - 1.0) < eps:
        return xr, xi
    if abs(c) < eps and abs(s - 1.0) < eps:
        return ("neg", xi), xr
    if abs(c) < eps and abs(s + 1.0) < eps:
        return xi, ("neg", xr)
    return xr * c - xi * s, xr * s + xi * c


def _is_neg(v):
    return isinstance(v, tuple) and v[0] == "neg"


def _plain(v):
    return -v[1] if _is_neg(v) else v


def _add(a, b):
    return a - b[1] if _is_neg(b) else a + b


def _sub(a, b):
    return a + b[1] if _is_neg(b) else a - b


def _fft_dit(xs, sign):
    n = len(xs)
    if n == 1:
        return xs
    ev = _fft_dit(xs[0::2], sign)
    od = _fft_dit(xs[1::2], sign)
    out = [None] * n
    for k in range(n // 2):
        w = complex(math.cos(2 * math.pi * k / n), sign * math.sin(2 * math.pi * k / n))
        tr, ti = _cmul_const(od[k], w)
        er, ei = ev[k]
        out[k] = (_add(er, tr), _add(ei, ti))
        out[k + n // 2] = (_sub(er, tr), _sub(ei, ti))
    return out


def _slow_forward(x):
    half, quarter = N_SLOW // 2, N_SLOW // 4
    z = [(x[2 * m], x[2 * m + 1]) for m in range(quarter)]
    zt = [z[0]]
    for m in range(1, quarter):
        tr, ti = _cmul_const(z[m], complex(math.cos(2 * math.pi * m / half), -math.sin(2 * math.pi * m / half)))
        zt.append((_plain(tr), _plain(ti)))
    ze = _fft_dit(z, -1)
    zo = _fft_dit(zt, -1)
    zc = [None] * half
    for j in range(quarter):
        zc[2 * j] = ze[j]
        zc[2 * j + 1] = zo[j]
    out = [None] * (half + 1)
    ar, ai = zc[0]
    out[0] = ((ar + ai) * 2.0, None)
    out[half] = ((ar - ai) * 2.0, None)
    ar, ai = zc[quarter]
    out[quarter] = (ar * 2.0, ai * -2.0)
    for k in range(1, quarter):
        ar, ai = zc[k]
        br, bi = zc[half - k]
        c, s = math.cos(2 * math.pi * k / N_SLOW), math.sin(2 * math.pi * k / N_SLOW)
        p, q, r, t = ar + br, ai - bi, ai + bi, ar - br
        u = r * c - t * s
        vn = t * (-c) - r * s
        out[k] = (p + u, q + vn)
        out[half - k] = (p - u, vn - q)
    return out


def _slow_inverse(xs):
    half, quarter = N_SLOW // 2, N_SLOW // 4
    zc = [None] * half
    ar, br = xs[0][0], xs[half][0]
    zc[0] = (ar + br, ar - br)
    ar, ai = xs[quarter]
    zc[quarter] = (ar * 2.0, ai * -2.0)
    for k in range(1, quarter):
        ar, ai = xs[k]
        br, bi = xs[half - k]
        c, s = math.cos(2 * math.pi * k / N_SLOW), math.sin(2 * math.pi * k / N_SLOW)
        p, q, t, r = ar + br, ai - bi, ar - br, ai + bi
        o_r = t * c - r * s
        o_i = t * s + r * c
        zc[k] = (p - o_i, q + o_r)
        zc[half - k] = (p + o_i, o_r - q)
    ze = _fft_dit(zc[0::2], +1)
    zo = _fft_dit(zc[1::2], +1)
    out = [None] * N_IN
    for m in range(quarter):
        tr, ti = _cmul_const(zo[m], complex(math.cos(2 * math.pi * m / half), math.sin(2 * math.pi * m / half)))
        out[2 * m] = _add(ze[m][0], tr)
        out[2 * m + 1] = _add(ze[m][1], ti)
    return out


def _slow_forward_pass(src_ref, a_ref):
    lanes = src_ref.shape[-1]
    zeros = jnp.zeros((ROW_TILE, lanes), F32)

    def body(j, carry):
        rows = pl.ds(pl.multiple_of(j * ROW_TILE, ROW_TILE), ROW_TILE)
        rows_im = pl.ds(pl.multiple_of(N_FAST + j * ROW_TILE, ROW_TILE), ROW_TILE)
        out = _slow_forward([src_ref[n1, rows, :] for n1 in range(N_IN)])
        for k1, (re, im) in enumerate(out):
            a_ref[k1, rows, :] = re
            a_ref[k1, rows_im, :] = zeros if im is None else im
        a_ref[N_K1, rows, :] = zeros
        a_ref[N_K1, rows_im, :] = zeros
        return carry

    lax.fori_loop(0, N_FAST // ROW_TILE, body, 0)


def _cmul(a, w):
    ar, ai, wr, wi = a[:N_FAST], a[N_FAST:], w[:N_FAST], w[N_FAST:]
    return ar * wr - ai * wi, ar * wi + ai * wr


def _cmul_conj(a, w):
    ar, ai, wr, wi = a[:N_FAST], a[N_FAST:], w[:N_FAST], w[N_FAST:]
    return ar * wr + ai * wi, ai * wr - ar * wi


def _spectrum_kernel(hf_ref, hb_ref, tw_ref, f_ref, o_ref, af_ref, ab_ref):
    _slow_forward_pass(hf_ref, af_ref)
    _slow_forward_pass(hb_ref, ab_ref)

    def body(k1, carry):
        tw = tw_ref[k1]
        f = f_ref[...]
        both = jnp.concatenate([jnp.concatenate(_cmul(af_ref[k1], tw), axis=0),
                                jnp.concatenate(_cmul(ab_ref[k1], tw), axis=0)], axis=1)
        x = _dot3(f, both)
        xf, xb = x[:, :LANES], x[:, LANES:]
        re = xf[:N_FAST] + xb[:N_FAST]
        im = xf[N_FAST:] - xb[N_FAST:]
        o_ref[k1] = jnp.concatenate([re, im], axis=0) * (1.0 / (4 * N_FFT))
        return carry

    lax.fori_loop(0, N_K1_PAD, body, 0)


def filter_spectrum(hf, hb, tw, f_fwd):
    nblk = D_B // LANES
    slabs = lambda a: a.reshape(N_IN, N_FAST, D_B)
    whole = pl.BlockSpec(memory_space=pltpu.VMEM)
    spec_blk = (N_K1_PAD, 2 * N_FAST, LANES)
    return pl.pallas_call(
        _spectrum_kernel,
        out_shape=jax.ShapeDtypeStruct((N_K1_PAD, 2 * N_FAST, D_B), F32),
        grid=(nblk,),
        in_specs=[
            pl.BlockSpec((N_IN, N_FAST, LANES), lambda c: (0, 0, c)),
            pl.BlockSpec((N_IN, N_FAST, LANES), lambda c: (0, 0, c)),
            whole, whole,
        ],
        out_specs=pl.BlockSpec(spec_blk, lambda c: (0, 0, c)),
        scratch_shapes=[pltpu.VMEM(spec_blk, F32)] * 2,
        compiler_params=_cparams(("arbitrary",)),
        name="filter_spectrum",
    )(slabs(hf), slabs(hb), tw, f_fwd)


def _hyena_kernel(x0_ref, x1_ref, v_ref, w0_ref, w1_ref, wv_ref, b0_ref, b1_ref, bv_ref,
                  hs_ref, hbias_ref, tw_ref, ff_ref, fi_ref, o_ref, u_ref, g0_ref, a_ref, z_ref, st_ref):
    zero_rows = jnp.zeros((ROW_TILE, LANES), F32)
    for op in range(3):
        st_ref[op, 0:ROW_TILE, :] = zero_rows
        st_ref[op, SEQ + ROW_TILE:SEQ + 2 * ROW_TILE, :] = zero_rows

    def stage(n1, carry):
        r0 = pl.multiple_of(n1 * N_FAST, N_FAST)
        for op, x_ref in enumerate((x0_ref, x1_ref, v_ref)):
            st_ref[op, pl.ds(r0 + ROW_TILE, N_FAST), :] = x_ref[0, pl.ds(r0, N_FAST), :].astype(F32)
        return carry

    lax.fori_loop(0, N_IN, stage, 0)

    def tap3(op, w_ref, b_ref, n1):
        r0 = n1 * N_FAST + ROW_TILE
        prev = st_ref[op, pl.ds(r0 - 1, N_FAST), :]
        cur = st_ref[op, pl.ds(r0, N_FAST), :]
        nxt = st_ref[op, pl.ds(r0 + 1, N_FAST), :]
        return prev * w_ref[0:1, :] + cur * w_ref[1:2, :] + nxt * w_ref[2:3, :] + b_ref[...]

    def conv(n1, carry):
        u_ref[n1] = tap3(2, wv_ref, bv_ref, n1) * tap3(1, w1_ref, b1_ref, n1)
        g0_ref[n1] = tap3(0, w0_ref, b0_ref, n1)
        return carry

    lax.fori_loop(0, N_IN, conv, 0)
    _slow_forward_pass(u_ref, a_ref)

    def freq(i, carry):
        ks = (2 * i, 2 * i + 1)
        tws = [tw_ref[k] for k in ks]
        a = [jnp.concatenate(_cmul(a_ref[k], tw), axis=0).astype(BF16) for k, tw in zip(ks, tws)]
        x = jnp.dot(ff_ref[...], jnp.concatenate(a, axis=1), preferred_element_type=F32)
        y = [jnp.concatenate(_cmul(x[:, j * LANES:(j + 1) * LANES], hs_ref[k]), axis=0).astype(BF16)
             for j, k in enumerate(ks)]
        z = jnp.dot(fi_ref[...], jnp.concatenate(y, axis=1), preferred_element_type=F32)
        for j, (k, tw) in enumerate(zip(ks, tws)):
            z_ref[k] = jnp.concatenate(_cmul_conj(z[:, j * LANES:(j + 1) * LANES], tw), axis=0)
        return carry

    lax.fori_loop(0, N_K1_PAD // 2, freq, 0, unroll=True)

    def back(j, carry):
        rows = pl.ds(pl.multiple_of(j * ROW_TILE, ROW_TILE), ROW_TILE)
        rows_im = pl.ds(pl.multiple_of(N_FAST + j * ROW_TILE, ROW_TILE), ROW_TILE)
        y = _slow_inverse([(z_ref[k1, rows, :], z_ref[k1, rows_im, :]) for k1 in range(N_K1)])
        for n1 in range(N_IN):
            u_ref[n1, rows, :] = y[n1] + u_ref[n1, rows, :] * hbias_ref[...]
        return carry

    lax.fori_loop(0, N_FAST // ROW_TILE, back, 0)

    def gate(n1, carry):
        rows = pl.ds(pl.multiple_of(n1 * N_FAST, N_FAST), N_FAST)
        o_ref[0, rows, :] = (u_ref[n1] * g0_ref[n1]).astype(BF16)
        return carry

    lax.fori_loop(0, N_IN, gate, 0)


def hyena(p3, sc_w, sc_b, h_spec, h_bias, tables):
    tw, f_fwd, f_inv = tables
    b = p3.shape[0]
    nblk = D_B // LANES
    base = 3 * D_A // LANES
    xblk = lambda off: pl.BlockSpec((1, SEQ, LANES), lambda c, i: (i, 0, base + off * nblk + c))
    wblk = lambda off: pl.BlockSpec((3, LANES), lambda c, i: (0, off * nblk + c))
    bblk = lambda off: pl.BlockSpec((1, LANES), lambda c, i: (0, off * nblk + c))
    whole = pl.BlockSpec(memory_space=pltpu.VMEM)
    sc_b2 = sc_b.reshape(1, 3 * D_B)
    spec_blk = (N_K1_PAD, 2 * N_FAST, LANES)
    return pl.pallas_call(
        _hyena_kernel,
        out_shape=jax.ShapeDtypeStruct((b, SEQ, D_B), BF16),
        grid=(nblk, b),
        in_specs=[
            xblk(0), xblk(1), xblk(2),
            wblk(0), wblk(1), wblk(2),
            bblk(0), bblk(1), bblk(2),
            pl.BlockSpec(spec_blk, lambda c, i: (0, 0, c)),
            pl.BlockSpec((1, LANES), lambda c, i: (0, c)),
            whole, whole, whole,
        ],
        out_specs=pl.BlockSpec((1, SEQ, LANES), lambda c, i: (i, 0, c)),
        scratch_shapes=[
            pltpu.VMEM((N_IN, N_FAST, LANES), F32),
            pltpu.VMEM((N_IN, N_FAST, LANES), F32),
            pltpu.VMEM(spec_blk, F32),
            pltpu.VMEM(spec_blk, F32),
            pltpu.VMEM((3, SEQ + 2 * ROW_TILE, LANES), F32),
        ],
        compiler_params=_cparams(("parallel", "arbitrary")),
        name="hyena",
    )(p3, p3, p3, sc_w, sc_w, sc_w, sc_b2, sc_b2, sc_b2,
      h_spec, h_bias.reshape(1, D_B), tw, f_fwd.astype(BF16), f_inv.astype(BF16))


def kernel(x_prompt, x_sample, norm_mix, norm_mlp, w_in_ab, sc_w, sc_b, q_gain, k_gain, rpb,
           f_w1, f_b1, f_w2, f_b2, f_w3, f_b3, f_wout, f_freq, h_bias, w_out_ab, w_in_c,
           v_gain, w_s, b_s, w_out_c, w_mlp1, w_mlp2):
    assert x_prompt.shape[1:] == (SEQ, D_MODEL) and x_sample.shape[1:] == (SEQ, D_MODEL)
    assert w_mlp1.shape == (DEPTH, D_MODEL, D_FF) and w_in_ab.shape[1:] == (D_MODEL, 3 * (D_A + D_B))
    nb_p, nb_s = x_prompt.shape[0], x_sample.shape[0]
    nb = nb_p + nb_s
    x = (x_prompt.reshape(nb_p * SEQ, D_MODEL), x_sample.reshape(nb_s * SEQ, D_MODEL))
    tables = tuple(jnp.asarray(a) for a in _dft_tables())

    w_in_ab, w_out_ab, w_in_c, w_out_c, w_mlp1, w_mlp2, w_s = (
        w.astype(BF16) for w in (w_in_ab, w_out_ab, w_in_c, w_out_c, w_mlp1, w_mlp2, w_s))

    for i in range(DEPTH):
        j = i // 2
        if i % 2 == 0:
            p3 = in_proj(x, norm_mix[i], w_in_ab, j, n=3 * (D_A + D_B), mode="qk_norm",
                         qk_gains=(q_gain[j], k_gain[j])).reshape(nb, SEQ, 3 * (D_A + D_B))
            a = natten(p3, natten_bias_table(rpb[j]))
            hf, hb = hyena_filter(f_w1[j], f_b1[j], f_w2[j], f_b2[j], f_w3[j], f_b3[j],
                                  f_wout[j], f_freq[j])
            h_spec = filter_spectrum(hf, hb, tables[0], tables[1])
            bo = hyena(p3, sc_w[j], sc_b[j], h_spec, h_bias[j], tables)
            x = (proj_residual(x, a.reshape(nb * SEQ, D_A), bo.reshape(nb * SEQ, D_B), w_out_ab, j),)
        else:
            zz = in_proj(x, norm_mix[i], w_in_c, j, n=2 * D_C, mode="gelu", tn=2048)
            b_lanes = jnp.broadcast_to(b_s[j][:, :, None], (G_C, CHUNK, LANES))
            x = (gate_proj_residual(x[0], zz, v_gain[j], w_s[j], b_lanes, w_out_c, j),)
        if i < DEPTH - 1:
            x = (mlp_residual(x[0], norm_mlp[i], w_mlp1, w_mlp2, i),)

    last = DEPTH - 1
    y_p = mlp_residual(x[0], norm_mlp[last], w_mlp1, w_mlp2, last, rows=nb_p * SEQ)
    y_s = mlp_residual(x[0], norm_mlp[last], w_mlp1, w_mlp2, last, row_start=nb_p * SEQ)
    return y_p.reshape(nb_p, SEQ, D_MODEL), y_s.reshape(nb_s, SEQ, D_MODEL)
```
